```python
import math
import jax, jax.numpy as jnp
from jax import lax
import numpy as np

D_MODEL = 1024
BATCH = 2
SEQ = 8192
DEPTH = 2

N_GROUPS = 4
GROUP_WIDTH = D_MODEL // N_GROUPS
D_MIX = N_GROUPS * GROUP_WIDTH
CONV_KERNEL = 31
MLA_HEADS = 4
MLA_NOPE = 64
MLA_ROPE = 32
MLA_V = GROUP_WIDTH // MLA_HEADS
MLA_Q_RANK = 192
MLA_KV_RANK = 128
DIFF_HEADS = 4
DIFF_HALF = GROUP_WIDTH // DIFF_HEADS // 2
DIFF_V = 2 * DIFF_HALF
POOL_WINDOWS = (2, 4, 8, 16)
POOL_GROUPS = len(POOL_WINDOWS)
POOL_CH = GROUP_WIDTH // POOL_GROUPS
REL_BUCKETS = 32
REL_MAX_DIST = 128
MEM_LEN = 256
XA_HEADS = 4
XA_HEAD_DIM = D_MODEL // XA_HEADS
D_FF = 2816
FFN_CONV = 3
QBLK = 128
ROPE_BASE = 10000.0
NEG_INF = -1e30

IN_SIZES = (2 * GROUP_WIDTH,
            MLA_Q_RANK, MLA_KV_RANK, MLA_ROPE,
            GROUP_WIDTH, GROUP_WIDTH, GROUP_WIDTH,
            GROUP_WIDTH)
D_IN = sum(IN_SIZES)
IN_SPLITS = tuple(int(s) for s in np.cumsum(IN_SIZES)[:-1])

kernel_name = 'hybrid_parallel_head_group_block'


def rmsnorm(x, g, eps=1e-6):
    xf = x.astype(jnp.float32)
    y = xf * lax.rsqrt(jnp.mean(xf * xf, axis=-1, keepdims=True) + eps)
    return (y * g.astype(jnp.float32)).astype(x.dtype)


def layernorm(x, g, b, eps=1e-5):
    xf = x.astype(jnp.float32)
    mu = jnp.mean(xf, axis=-1, keepdims=True)
    var = jnp.mean(jnp.square(xf - mu), axis=-1, keepdims=True)
    y = (xf - mu) * lax.rsqrt(var + eps)
    return (y * g.astype(jnp.float32) + b.astype(jnp.float32)).astype(x.dtype)


def causal_dwconv(x, w, b):
    k = w.shape[0]
    y = lax.conv_general_dilated(x, w[:, None, :], window_strides=(1,),
                                 padding=[(k - 1, 0)],
                                 dimension_numbers=('NWC', 'WIO', 'NWC'),
                                 feature_group_count=x.shape[-1])
    return y + b


def rope(x, cos, sin):
    half = x.shape[-1] // 2
    xf = x.astype(jnp.float32)
    x1, x2 = xf[..., :half], xf[..., half:]
    return jnp.concatenate([x1 * cos - x2 * sin, x1 * sin + x2 * cos], axis=-1).astype(x.dtype)


def t5_bucket(rel):
    n = jnp.maximum(rel, 0)
    max_exact = REL_BUCKETS // 2
    nf = jnp.maximum(n, 1).astype(jnp.float32)
    large = max_exact + (jnp.log(nf / max_exact) / math.log(REL_MAX_DIST / max_exact)
                         * (REL_BUCKETS - max_exact)).astype(jnp.int32)
    large = jnp.minimum(large, REL_BUCKETS - 1)
    return jnp.where(n < max_exact, n, large)


def to_blocks(t):
    b, s = t.shape[0], t.shape[1]
    t = t.reshape((b, s // QBLK, QBLK) + t.shape[2:])
    return jnp.moveaxis(t, 1, 0)


def from_blocks(t):
    t = jnp.moveaxis(t, 0, 1)
    return t.reshape((t.shape[0], t.shape[1] * t.shape[2]) + t.shape[3:])


def causal_attention(q, k, v, positions, scale):
    pb = positions.reshape(-1, QBLK)

    def block(args):
        qi, pi = args
        s = jnp.einsum('bqhd,bkhd->bhqk', qi, k).astype(jnp.float32) * scale
        mask = (pi[:, None] >= positions[None, :])[None, None]
        p = jax.nn.softmax(jnp.where(mask, s, NEG_INF), axis=-1).astype(v.dtype)
        return jnp.einsum('bhqk,bkhd->bqhd', p, v)

    return from_blocks(lax.map(block, (to_blocks(q), pb)))


def conformer_conv(u, dw, dw_b, ln_g, ln_b, pw, pw_b):
    a, g = jnp.split(u, 2, axis=-1)
    h = a * jax.nn.sigmoid(g)
    h = causal_dwconv(h, dw, dw_b)
    h = jax.nn.silu(layernorm(h, ln_g, ln_b))
    return h @ pw + pw_b


def mla(cq, ckv, kr, q_norm, w_uq, kv_norm, w_ukv, cos, sin, positions):
    b, s, _ = cq.shape
    q = (rmsnorm(cq, q_norm) @ w_uq).reshape(b, s, MLA_HEADS, MLA_NOPE + MLA_ROPE)
    q = jnp.concatenate([q[..., :MLA_NOPE],
                         rope(q[..., MLA_NOPE:], cos[:, None, :], sin[:, None, :])], axis=-1)
    kv = (rmsnorm(ckv, kv_norm) @ w_ukv).reshape(b, s, MLA_HEADS, MLA_NOPE + MLA_V)
    k_nope, v = kv[..., :MLA_NOPE], kv[..., MLA_NOPE:]
    k_rope = jnp.broadcast_to(rope(kr, cos, sin)[:, :, None, :], (b, s, MLA_HEADS, MLA_ROPE))
    k = jnp.concatenate([k_nope, k_rope], axis=-1)
    o = causal_attention(q, k, v, positions, (MLA_NOPE + MLA_ROPE) ** -0.5)
    return o.reshape(b, s, MLA_HEADS * MLA_V)


def diff_attention(qc, kc, vc, positions, rel_bias, lq1, lk1, lq2, lk2, subln_g, lambda_init):
    b, s, _ = qc.shape
    q = qc.reshape(b, s, DIFF_HEADS, 2, DIFF_HALF)
    k = kc.reshape(b, s, DIFF_HEADS, 2, DIFF_HALF)
    v = vc.reshape(b, s, DIFF_HEADS, DIFF_V)
    lam = (jnp.exp(jnp.sum(lq1.astype(jnp.float32) * lk1.astype(jnp.float32)))
           - jnp.exp(jnp.sum(lq2.astype(jnp.float32) * lk2.astype(jnp.float32)))
           + lambda_init)
    scale = DIFF_HALF ** -0.5
    pb = positions.reshape(-1, QBLK)

    def block(args):
        qi, pi = args
        sc = jnp.einsum('bqhmd,bkhmd->bhmqk', qi, k).astype(jnp.float32) * scale
        rel = pi[:, None] - positions[None, :]
        bias = jnp.transpose(rel_bias[t5_bucket(rel)], (2, 0, 1)).astype(jnp.float32)
        sc = sc + bias[None, :, None]
        mask = (rel >= 0)[None, None, None]
        a = jax.nn.softmax(jnp.where(mask, sc, NEG_INF), axis=-1)
        p = (a[:, :, 0] - lam * a[:, :, 1]).astype(v.dtype)
        return jnp.einsum('bhqk,bkhd->bqhd', p, v)

    o = from_blocks(lax.map(block, (to_blocks(q), pb)))
    o = rmsnorm(o, subln_g, eps=1e-5) * (1.0 - lambda_init)
    return o.reshape(b, s, DIFF_HEADS * DIFF_V)


def multiscale_pool(u, pool_w, pool_scale):
    b, s, _ = u.shape
    uf = u.reshape(b, s, POOL_GROUPS, POOL_CH).astype(jnp.float32)
    cs = jnp.pad(jnp.cumsum(uf, axis=1), ((0, 0), (1, 0), (0, 0), (0, 0)))
    t = jnp.arange(s)
    pooled = []
    for g, w in enumerate(POOL_WINDOWS):
        lo = jnp.maximum(t + 1 - w, 0)
        win = cs[:, 1:, g] - cs[:, lo, g]
        cnt = (t + 1 - lo).astype(jnp.float32)
        pooled.append(win / cnt[None, :, None])
    d = (jnp.stack(pooled, axis=2) - uf).astype(u.dtype)
    y = jnp.einsum('bsgc,gcd->bsgd', d, pool_w).reshape(b, s, GROUP_WIDTH)
    return y * pool_scale


def memory_cross_attention(h, mem, mem_norm, wq, wk, wv, wo):
    b, s, _ = h.shape
    m = rmsnorm(mem, mem_norm)
    q = (h @ wq).reshape(b, s, XA_HEADS, XA_HEAD_DIM)
    k = (m @ wk).reshape(b, -1, XA_HEADS, XA_HEAD_DIM)
    v = (m @ wv).reshape(b, -1, XA_HEADS, XA_HEAD_DIM)
    sc = jnp.einsum('bshd,bmhd->bhsm', q, k).astype(jnp.float32) * XA_HEAD_DIM ** -0.5
    p = jax.nn.softmax(sc, axis=-1).astype(v.dtype)
    o = jnp.einsum('bhsm,bmhd->bshd', p, v).reshape(b, s, XA_HEADS * XA_HEAD_DIM)
    return o @ wo


def conv_glu_ffn(h, w_up, dw, dw_b, w_down):
    u = causal_dwconv(h @ w_up, dw, dw_b)
    a, g = jnp.split(u, 2, axis=-1)
    return (jax.nn.silu(g) * a) @ w_down


def setup_inputs(seed: int = 0) -> dict:
    key = jax.random.key(seed)
    ks = iter(jax.random.split(key, 48))

    def nrm(shape, fan_in):
        return jax.random.normal(next(ks), shape, jnp.float32) * fan_in ** -0.5

    def gain(shape):
        return 1.0 + 0.02 * jax.random.normal(next(ks), shape, jnp.float32)

    def small(shape, sc=0.02):
        return sc * jax.random.normal(next(ks), shape, jnp.float32)

    L = DEPTH
    return {
        'x': jax.random.normal(next(ks), (BATCH, SEQ, D_MODEL), jnp.float32),
        'mem': jax.random.normal(next(ks), (BATCH, MEM_LEN, D_MODEL), jnp.float32),
        'positions': jnp.arange(SEQ, dtype=jnp.int32),
        'rel_bias': small((REL_BUCKETS, DIFF_HEADS), 0.5),
        'norm_mix': gain((L, D_MODEL)),
        'w_in': nrm((L, D_MODEL, D_IN), D_MODEL),
        'w_out': nrm((L, D_MIX, D_MODEL), D_MIX),
        'conv_dw': nrm((L, CONV_KERNEL, GROUP_WIDTH), CONV_KERNEL),
        'conv_dw_b': small((L, GROUP_WIDTH)),
        'conv_ln_g': gain((L, GROUP_WIDTH)),
        'conv_ln_b': small((L, GROUP_WIDTH)),
        'conv_pw': nrm((L, GROUP_WIDTH, GROUP_WIDTH), GROUP_WIDTH),
        'conv_pw_b': small((L, GROUP_WIDTH)),
        'mla_q_norm': gain((L, MLA_Q_RANK)),
        'mla_w_uq': nrm((L, MLA_Q_RANK, MLA_HEADS * (MLA_NOPE + MLA_ROPE)), MLA_Q_RANK),
        'mla_kv_norm': gain((L, MLA_KV_RANK)),
        'mla_w_ukv': nrm((L, MLA_KV_RANK, MLA_HEADS * (MLA_NOPE + MLA_V)), MLA_KV_RANK),
        'diff_lq1': small((L, DIFF_HALF), 0.1),
        'diff_lk1': small((L, DIFF_HALF), 0.1),
        'diff_lq2': small((L, DIFF_HALF), 0.1),
        'diff_lk2': small((L, DIFF_HALF), 0.1),
        'diff_subln': gain((L, DIFF_V)),
        'pool_w': nrm((L, POOL_GROUPS, POOL_CH, POOL_CH), POOL_CH),
        'pool_scale': gain((L, GROUP_WIDTH)),
        'norm_xa': gain((L, D_MODEL)),
        'mem_norm': gain((L, D_MODEL)),
        'xa_wq': nrm((L, D_MODEL, D_MODEL), D_MODEL),
        'xa_wk': nrm((L, D_MODEL, D_MODEL), D_MODEL),
        'xa_wv': nrm((L, D_MODEL, D_MODEL), D_MODEL),
        'xa_wo': nrm((L, D_MODEL, D_MODEL), D_MODEL),
        'norm_ffn': gain((L, D_MODEL)),
        'ffn_up': nrm((L, D_MODEL, 2 * D_FF), D_MODEL),
        'ffn_dw': nrm((L, FFN_CONV, 2 * D_FF), FFN_CONV),
        'ffn_dw_b': small((L, 2 * D_FF)),
        'ffn_down': nrm((L, D_FF, D_MODEL), D_FF),
        'norm_final': gain((D_MODEL,)),
    }


def reference(x, mem, positions, rel_bias, norm_mix, w_in, w_out, conv_dw, conv_dw_b,
              conv_ln_g, conv_ln_b, conv_pw, conv_pw_b, mla_q_norm, mla_w_uq, mla_kv_norm,
              mla_w_ukv, diff_lq1, diff_lk1, diff_lq2, diff_lk2, diff_subln, pool_w,
              pool_scale, norm_xa, mem_norm, xa_wq, xa_wk, xa_wv, xa_wo, norm_ffn, ffn_up,
              ffn_dw, ffn_dw_b, ffn_down, norm_final):
    inv_freq = ROPE_BASE ** (-jnp.arange(0, MLA_ROPE, 2, dtype=jnp.float32) / MLA_ROPE)
    ang = positions.astype(jnp.float32)[:, None] * inv_freq[None, :]
    cos, sin = jnp.cos(ang), jnp.sin(ang)
    for l in range(DEPTH):
        h = rmsnorm(x, norm_mix[l])
        proj = h @ w_in[l]
        ua, cq, ckv, kr, qc, kc, vc, ud = jnp.split(proj, IN_SPLITS, axis=-1)
        ya = conformer_conv(ua, conv_dw[l], conv_dw_b[l], conv_ln_g[l], conv_ln_b[l],
                            conv_pw[l], conv_pw_b[l])
        yb = mla(cq, ckv, kr, mla_q_norm[l], mla_w_uq[l], mla_kv_norm[l], mla_w_ukv[l],
                 cos, sin, positions)
        lambda_init = 0.8 - 0.6 * math.exp(-0.3 * l)
        yc = diff_attention(qc, kc, vc, positions, rel_bias, diff_lq1[l], diff_lk1[l],
                            diff_lq2[l], diff_lk2[l], diff_subln[l], lambda_init)
        yd = multiscale_pool(ud, pool_w[l], pool_scale[l])
        x = x + jnp.concatenate([ya, yb, yc, yd], axis=-1) @ w_out[l]
        x = x + memory_cross_attention(rmsnorm(x, norm_xa[l]), mem, mem_norm[l],
                                       xa_wq[l], xa_wk[l], xa_wv[l], xa_wo[l])
        x = x + conv_glu_ffn(rmsnorm(x, norm_ffn[l]), ffn_up[l], ffn_dw[l], ffn_dw_b[l],
                             ffn_down[l])
    return rmsnorm(x, norm_final)
```

```python
import functools
import math

import jax
import jax.numpy as jnp
from jax import lax
from jax.experimental import pallas as pl
from jax.experimental.pallas import tpu as pltpu

N_GROUPS = 4
CONV_KERNEL = 31
MLA_HEADS = 4
MLA_NOPE = 64
MLA_ROPE = 32
MLA_V = 64
MLA_Q_RANK = 192
MLA_KV_RANK = 128
DIFF_HEADS = 4
DIFF_HALF = 32
DIFF_V = 64
POOL_WINDOWS = (2, 4, 8, 16)
POOL_CH = 64
REL_BUCKETS = 32
REL_MAX_DIST = 128
XA_HEADS = 4
FFN_CONV = 3
ROPE_BASE = 10000.0
NEG_INF = -1e30

LANES = 128
SUBLANES = 8
VMEM_LIMIT_BYTES = 56 * 1024 * 1024

TOKEN_TILE = 512
ATTN_TILE = 256
CONV_HALO = 32
POOL_HALO = 16
FFN_HALO = 8
ROW_CHUNK = 64
FFN_CHUNK = 256

BF16 = jnp.bfloat16
F32 = jnp.float32


def _dot(a, b):
    return jnp.dot(a, b, preferred_element_type=F32)


def _dot_nt(a, b):
    return lax.dot_general(a, b, (((1,), (1,)), ((), ())), preferred_element_type=F32)


def _sigmoid(x):
    return 1.0 / (1.0 + jnp.exp(-x))


def _const_spec(shape):
    nd = len(shape)
    return pl.BlockSpec(shape, lambda *_: (0,) * nd)


_A0, _A1 = 0, 512
_B0, _B1 = 512, 1280
_C0, _C1 = 1280, 2048
_D0, _D1 = 2048, 2304
W_IN_COLS = 2304


def _mix_in_kernel(x_ref, g_ref, win_ref, dw_ref, dwb_ref, lng_ref, lnb_ref, pw_ref, pwb_ref,
                   qn_ref, wuq_ref, wuqs_ref, kvn_ref, wukv_ref, ct_ref, st_ref,
                   poolw_ref, pools_ref,
                   ya_ref, qm_ref, km_ref, vm_ref, qd_ref, kd_ref, vd_ref, yd_ref,
                   hbuf, zbuf, ubuf, dbuf, *, tm):
    s = pl.program_id(1)

    @pl.when(s == 0)
    def _():
        hbuf[0:CONV_HALO, :] = jnp.zeros((CONV_HALO, 256), F32)
        ubuf[0:POOL_HALO, :] = jnp.zeros((POOL_HALO, 256), F32)

    x = x_ref[0]
    h = x * lax.rsqrt(jnp.mean(x * x, axis=-1, keepdims=True) + 1e-6) * g_ref[...]
    hb = h.astype(BF16)

    pa = _dot(hb, win_ref[:, _A0:_A1])
    hbuf[CONV_HALO:CONV_HALO + tm, :] = pa[:, 0:256] * _sigmoid(pa[:, 256:512])
    for c in range(tm // ROW_CHUNK):
        r0 = c * ROW_CHUNK
        acc = jnp.broadcast_to(dwb_ref[...], (ROW_CHUNK, 256))
        for k in range(CONV_KERNEL):
            off = r0 + CONV_HALO - (CONV_KERNEL - 1) + k
            acc = acc + dw_ref[k:k + 1, :] * hbuf[off:off + ROW_CHUNK, :]
        mu = jnp.mean(acc, axis=-1, keepdims=True)
        cen = acc - mu
        var = jnp.mean(cen * cen, axis=-1, keepdims=True)
        yn = cen * lax.rsqrt(var + 1e-5) * lng_ref[...] + lnb_ref[...]
        zbuf[r0:r0 + ROW_CHUNK, :] = (yn * _sigmoid(yn)).astype(BF16)
    hbuf[0:CONV_HALO, :] = hbuf[tm:tm + CONV_HALO, :]
    ya_ref[0] = (_dot(zbuf[...], pw_ref[...]) + pwb_ref[...]).astype(BF16)

    pb = _dot(hb, win_ref[:, _B0:_B1])
    ct = ct_ref[...]
    st = st_ref[...]
    cq = pb[:, 0:256]
    cqn = cq * lax.rsqrt(jnp.sum(cq * cq, axis=-1, keepdims=True) * (1.0 / MLA_Q_RANK) + 1e-6)
    cqb = (cqn * qn_ref[...]).astype(BF16)
    q = _dot(cqb, wuq_ref[...])
    qs = _dot(cqb, wuqs_ref[...])
    ct4 = jnp.concatenate([ct] * MLA_HEADS, axis=1)
    st4 = jnp.concatenate([st] * MLA_HEADS, axis=1)
    q_scale = (MLA_NOPE + MLA_ROPE) ** -0.5
    qm_ref[0] = ((q * ct4 + qs * st4) * q_scale).astype(BF16)
    ckv = pb[:, 256:384]
    ckvn = ckv * lax.rsqrt(jnp.mean(ckv * ckv, axis=-1, keepdims=True) + 1e-6)
    kv = _dot((ckvn * kvn_ref[...]).astype(BF16), wukv_ref[...])
    kr = pb[:, 384:512] * ct + pb[:, 512:640] * st
    km_ref[0] = (kv[:, 0:512] + jnp.concatenate([kr] * MLA_HEADS, axis=1)).astype(BF16)
    vm_ref[0] = kv[:, 512:768].astype(BF16)

    pc = _dot(hb, win_ref[:, _C0:_C1])
    qd_ref[0] = (pc[:, 0:256] * (DIFF_HALF ** -0.5)).astype(BF16)
    kd_ref[0] = pc[:, 256:512].astype(BF16)
    vd_ref[0] = pc[:, 512:768].astype(BF16)

    ubuf[POOL_HALO:POOL_HALO + tm, :] = _dot(hb, win_ref[:, _D0:_D1])
    lane = lax.broadcasted_iota(jnp.int32, (ROW_CHUNK, 256), 1)
    grp = lane // POOL_CH
    win_len = jnp.where(grp == 0, POOL_WINDOWS[0],
                        jnp.where(grp == 1, POOL_WINDOWS[1],
                                  jnp.where(grp == 2, POOL_WINDOWS[2], POOL_WINDOWS[3])))
    for c in range(tm // ROW_CHUNK):
        r0 = c * ROW_CHUNK
        u = ubuf[POOL_HALO + r0:POOL_HALO + r0 + ROW_CHUNK, :]
        run = u
        sums = {}
        for j in range(1, POOL_WINDOWS[-1]):
            run = run + ubuf[POOL_HALO + r0 - j:POOL_HALO + r0 - j + ROW_CHUNK, :]
            if j + 1 in POOL_WINDOWS:
                sums[j + 1] = run
        win = jnp.where(grp == 0, sums[2],
                        jnp.where(grp == 1, sums[4], jnp.where(grp == 2, sums[8], sums[16])))
        t_glob = s * tm + r0 + lax.broadcasted_iota(jnp.int32, (ROW_CHUNK, 256), 0)
        cnt = jnp.minimum(t_glob + 1, win_len).astype(F32)
        dbuf[r0:r0 + ROW_CHUNK, :] = (win / cnt - u).astype(BF16)
    ubuf[0:POOL_HALO, :] = ubuf[tm:tm + POOL_HALO, :]
    yd_ref[0] = (_dot(dbuf[...], poolw_ref[...]) * pools_ref[...]).astype(BF16)


def _mix_in(x, g, w_in, dw, dwb, lng, lnb, pw, pwb, qn, wuq, wuqs, kvn, wukv, ctab, stab,
            poolw, pools):
    b, s, d = x.shape
    tm = min(TOKEN_TILE, s)
    tok = lambda w: pl.BlockSpec((1, tm, w), lambda bi, si: (bi, si, 0))
    consts = [g, w_in, dw, dwb, lng, lnb, pw, pwb, qn, wuq, wuqs, kvn, wukv]
    in_specs = ([tok(d)] + [_const_spec(a.shape) for a in consts]
                + [pl.BlockSpec((tm, LANES), lambda bi, si: (si, 0))] * 2
                + [_const_spec(poolw.shape), _const_spec(pools.shape)])
    widths = (256, 512, 512, 256, 256, 256, 256, 256)
    return pl.pallas_call(
        functools.partial(_mix_in_kernel, tm=tm),
        grid=(b, s // tm),
        in_specs=in_specs,
        out_specs=[tok(w) for w in widths],
        out_shape=[jax.ShapeDtypeStruct((b, s, w), BF16) for w in widths],
        scratch_shapes=[pltpu.VMEM((CONV_HALO + tm, 256), F32),
                        pltpu.VMEM((tm, 256), BF16),
                        pltpu.VMEM((POOL_HALO + tm, 256), F32),
                        pltpu.VMEM((tm, 256), BF16)],
        compiler_params=pltpu.CompilerParams(
            dimension_semantics=("arbitrary", "arbitrary"),
            vmem_limit_bytes=VMEM_LIMIT_BYTES),
        name="mix_in",
    )(x, *consts, ctab, stab, poolw, pools)


def _head_lanes(vals, rows):
    lane = lax.broadcasted_iota(jnp.int32, (rows, 256), 1)
    return jnp.where(lane < 64, vals[0],
                     jnp.where(lane < 128, vals[1], jnp.where(lane < 192, vals[2], vals[3])))


def _stack_masked_v(v_blk, vst_ref, t):
    lane = lax.broadcasted_iota(jnp.int32, (t, 256), 1)
    for h in range(4):
        keep = (lane >= 64 * h) & (lane < 64 * (h + 1))
        vst_ref[h * t:(h + 1) * t, :] = jnp.where(keep, v_blk, jnp.zeros_like(v_blk))


def _softmax_step(sc, m_ref, l_ref, idx):
    m_old = m_ref[idx]
    m_new = jnp.maximum(m_old, jnp.max(sc, axis=-1, keepdims=True))
    alpha = jnp.exp(m_old - m_new)
    p = jnp.exp(sc - m_new)
    l_ref[idx] = alpha * l_ref[idx] + jnp.sum(p, axis=-1, keepdims=True)
    m_ref[idx] = m_new
    return p, alpha


def _mla_attn_kernel(q_ref, k_ref, v_ref, o_ref, vst_ref, p_ref, m_ref, l_ref, acc_ref, *, t):
    qi = pl.program_id(1)
    m_ref[...] = jnp.full(m_ref.shape, NEG_INF, F32)
    l_ref[...] = jnp.zeros(l_ref.shape, F32)
    acc_ref[...] = jnp.zeros(acc_ref.shape, F32)
    row = lax.broadcasted_iota(jnp.int32, (t, t), 0)
    col = lax.broadcasted_iota(jnp.int32, (t, t), 1)
    causal = row >= col

    def step(kj, diagonal):
        k0 = pl.multiple_of(kj * t, t)
        k_blk = k_ref[0, pl.ds(k0, t), :]
        _stack_masked_v(v_ref[0, pl.ds(k0, t), :], vst_ref, t)
        alphas = []
        for h in range(MLA_HEADS):
            sc = _dot_nt(q_ref[0, :, LANES * h:LANES * (h + 1)], k_blk[:, LANES * h:LANES * (h + 1)])
            if diagonal:
                sc = jnp.where(causal, sc, NEG_INF)
            p, alpha = _softmax_step(sc, m_ref, l_ref, h)
            p_ref[:, h * t:(h + 1) * t] = p.astype(BF16)
            alphas.append(alpha)
        acc_ref[...] = acc_ref[...] * _head_lanes(alphas, t) + _dot(p_ref[...], vst_ref[...])

    def body(kj, carry):
        step(kj, False)
        return carry

    lax.fori_loop(0, qi, body, 0)
    step(qi, True)
    inv_l = _head_lanes([1.0 / l_ref[h] for h in range(MLA_HEADS)], t)
    o_ref[0] = (acc_ref[...] * inv_l).astype(BF16)


def _mla_attn(q, k, v):
    b, s, _ = q.shape
    t = min(ATTN_TILE, s)
    return pl.pallas_call(
        functools.partial(_mla_attn_kernel, t=t),
        grid=(b, s // t),
        in_specs=[pl.BlockSpec((1, t, 512), lambda bi, qi: (bi, qi, 0)),
                  pl.BlockSpec((1, s, 512), lambda bi, qi: (bi, 0, 0)),
                  pl.BlockSpec((1, s, 256), lambda bi, qi: (bi, 0, 0))],
        out_specs=pl.BlockSpec((1, t, 256), lambda bi, qi: (bi, qi, 0)),
        out_shape=jax.ShapeDtypeStruct((b, s, 256), BF16),
        scratch_shapes=[pltpu.VMEM((4 * t, 256), BF16),
                        pltpu.VMEM((t, 4 * t), BF16),
                        pltpu.VMEM((MLA_HEADS, t, 1), F32),
                        pltpu.VMEM((MLA_HEADS, t, 1), F32),
                        pltpu.VMEM((t, 256), F32)],
        compiler_params=pltpu.CompilerParams(
            dimension_semantics=("arbitrary", "arbitrary"),
            vmem_limit_bytes=VMEM_LIMIT_BYTES),
        name="mla_attn",
    )(q, k, v)


def _bias_tiles_kernel(rb_ref, out_ref, *, t):
    row = lax.broadcasted_iota(jnp.int32, (t, t), 0)
    col = lax.broadcasted_iota(jnp.int32, (t, t), 1)
    max_exact = REL_BUCKETS // 2
    for j in range(2):
        rel = (1 - j) * t + row - col
        n = jnp.maximum(rel, 0)
        nf = jnp.maximum(n, 1).astype(F32)
        large = max_exact + (jnp.log(nf / max_exact) / math.log(REL_MAX_DIST / max_exact)
                             * (REL_BUCKETS - max_exact)).astype(jnp.int32)
        large = jnp.minimum(large, REL_BUCKETS - 1)
        bucket = jnp.where(n < max_exact, n, large)
        for h in range(DIFF_HEADS):
            far = rb_ref[REL_BUCKETS - 1, h]
            val = jnp.zeros((t, t), F32)
            for bkt in range(REL_BUCKETS - 1):
                val = jnp.where(bucket == bkt, rb_ref[bkt, h] - far, val)
            out_ref[h, j] = jnp.where(rel >= 0, val, NEG_INF)


def _bias_tiles(rel_bias, t):
    return pl.pallas_call(
        functools.partial(_bias_tiles_kernel, t=t),
        in_specs=[pl.BlockSpec(memory_space=pltpu.SMEM)],
        out_specs=pl.BlockSpec(memory_space=pltpu.VMEM),
        out_shape=jax.ShapeDtypeStruct((DIFF_HEADS, 2, t, t), F32),
        name="bias_tiles",
    )(rel_bias)


def _diff_attn_kernel(q_ref, k_ref, v_ref, bias_ref, lq1_ref, lk1_ref, lq2_ref, lk2_ref,
                      subln_ref, o_ref, qst_ref, vst_ref, p_ref, m_ref, l_ref, acc_ref,
                      *, t, lambda_init):
    qi = pl.program_id(1)
    nmaps = 2 * DIFF_HEADS
    m_ref[...] = jnp.full(m_ref.shape, NEG_INF, F32)
    l_ref[...] = jnp.zeros(l_ref.shape, F32)
    acc_ref[...] = jnp.zeros(acc_ref.shape, F32)
    q = q_ref[0]
    lane = lax.broadcasted_iota(jnp.int32, (t, 256), 1)
    for idx in range(nmaps):
        keep = (lane >= DIFF_HALF * idx) & (lane < DIFF_HALF * (idx + 1))
        qst_ref[idx * t:(idx + 1) * t, :] = jnp.where(keep, q, jnp.zeros_like(q))

    def step(kj, bias_tile):
        k0 = pl.multiple_of(kj * t, t)
        _stack_masked_v(v_ref[0, pl.ds(k0, t), :], vst_ref, t)
        sc_all = _dot_nt(qst_ref[...], k_ref[0, pl.ds(k0, t), :])
        alphas = [[], []]
        for h in range(DIFF_HEADS):
            for which in range(2):
                idx = 2 * h + which
                sc = sc_all[idx * t:(idx + 1) * t, :]
                if bias_tile is not None:
                    sc = sc + bias_ref[h, bias_tile]
                p, alpha = _softmax_step(sc, m_ref, l_ref, idx)
                p_ref[which * t:(which + 1) * t, h * t:(h + 1) * t] = p.astype(BF16)
                alphas[which].append(alpha)
        pv = _dot(p_ref[...], vst_ref[...])
        for which in range(2):
            rows = slice(which * t, (which + 1) * t)
            acc_ref[rows, :] = acc_ref[rows, :] * _head_lanes(alphas[which], t) + pv[rows, :]

    def body(kj, carry):
        step(kj, None)
        return carry

    lax.fori_loop(0, jnp.maximum(qi - 1, 0), body, 0)

    @pl.when(qi > 0)
    def _():
        step(qi - 1, 0)

    step(qi, 1)

    lam = (jnp.exp(jnp.sum(lq1_ref[...] * lk1_ref[...], axis=-1, keepdims=True))
           - jnp.exp(jnp.sum(lq2_ref[...] * lk2_ref[...], axis=-1, keepdims=True))
           + lambda_init)
    o1 = acc_ref[0:t, :] * _head_lanes([1.0 / l_ref[2 * h] for h in range(DIFF_HEADS)], t)
    o2 = acc_ref[t:2 * t, :] * _head_lanes([1.0 / l_ref[2 * h + 1] for h in range(DIFF_HEADS)], t)
    o = o1 - lam * o2
    osq = o * o
    ms = []
    for h in range(DIFF_HEADS):
        keep = (lane >= DIFF_V * h) & (lane < DIFF_V * (h + 1))
        ms.append(jnp.sum(jnp.where(keep, osq, 0.0), axis=-1, keepdims=True) * (1.0 / DIFF_V))
    on = o * lax.rsqrt(_head_lanes(ms, t) + 1e-5) * subln_ref[...]
    o_ref[0] = (on * (1.0 - lambda_init)).astype(BF16)


def _diff_attn(q, k, v, bias_tiles, lq1, lk1, lq2, lk2, subln, lambda_init):
    b, s, _ = q.shape
    t = min(ATTN_TILE, s)
    return pl.pallas_call(
        functools.partial(_diff_attn_kernel, t=t, lambda_init=lambda_init),
        grid=(b, s // t),
        in_specs=[pl.BlockSpec((1, t, 256), lambda bi, qi: (bi, qi, 0)),
                  pl.BlockSpec((1, s, 256), lambda bi, qi: (bi, 0, 0)),
                  pl.BlockSpec((1, s, 256), lambda bi, qi: (bi, 0, 0)),
                  _const_spec(bias_tiles.shape),
                  _const_spec(lq1.shape), _const_spec(lk1.shape),
                  _const_spec(lq2.shape), _const_spec(lk2.shape),
                  _const_spec(subln.shape)],
        out_specs=pl.BlockSpec((1, t, 256), lambda bi, qi: (bi, qi, 0)),
        out_shape=jax.ShapeDtypeStruct((b, s, 256), BF16),
        scratch_shapes=[pltpu.VMEM((8 * t, 256), BF16),
                        pltpu.VMEM((4 * t, 256), BF16),
                        pltpu.VMEM((2 * t, 4 * t), BF16),
                        pltpu.VMEM((2 * DIFF_HEADS, t, 1), F32),
                        pltpu.VMEM((2 * DIFF_HEADS, t, 1), F32),
                        pltpu.VMEM((2 * t, 256), F32)],
        compiler_params=pltpu.CompilerParams(
            dimension_semantics=("arbitrary", "arbitrary"),
            vmem_limit_bytes=VMEM_LIMIT_BYTES),
        name="diff_attn",
    )(q, k, v, bias_tiles, lq1, lk1, lq2, lk2, subln)


def _mem_kv_kernel(mem_ref, g_ref, wk_ref, wv_ref, k_ref, v_ref):
    m = mem_ref[0]
    mn = m * lax.rsqrt(jnp.mean(m * m, axis=-1, keepdims=True) + 1e-6) * g_ref[0]
    mb = mn.astype(BF16)
    k_ref[0, 0] = _dot(mb, wk_ref[0]).astype(BF16)
    v_ref[0, 0] = _dot(mb, wv_ref[0]).astype(BF16)


def _mem_kv(mem, g, wk, wv):
    b, m, d = mem.shape
    nl = wk.shape[0]
    out = jax.ShapeDtypeStruct((nl, b, m, d), BF16)
    return pl.pallas_call(
        _mem_kv_kernel,
        grid=(nl, b),
        in_specs=[pl.BlockSpec((1, m, d), lambda li, bi: (bi, 0, 0)),
                  pl.BlockSpec((1, 1, d), lambda li, bi: (li, 0, 0)),
                  pl.BlockSpec((1, d, d), lambda li, bi: (li, 0, 0)),
                  pl.BlockSpec((1, d, d), lambda li, bi: (li, 0, 0))],
        out_specs=[pl.BlockSpec((1, 1, m, d), lambda li, bi: (li, bi, 0, 0))] * 2,
        out_shape=[out, out],
        compiler_params=pltpu.CompilerParams(
            dimension_semantics=("arbitrary", "arbitrary"),
            vmem_limit_bytes=VMEM_LIMIT_BYTES),
        name="mem_kv",
    )(mem, g, wk, wv)


def _xa_kernel(x_ref, ya_ref, yb_ref, yc_ref, yd_ref, wout_ref, g_ref, wq_ref, km_ref, vm_ref,
               wo_ref, o_ref, obuf):
    x1 = x_ref[0]
    for i, y_ref in enumerate((ya_ref, yb_ref, yc_ref, yd_ref)):
        x1 = x1 + _dot(y_ref[0], wout_ref[256 * i:256 * (i + 1), :])
    hx = x1 * lax.rsqrt(jnp.mean(x1 * x1, axis=-1, keepdims=True) + 1e-6) * g_ref[...]
    hd = wq_ref.shape[1] // XA_HEADS
    q = (_dot(hx.astype(BF16), wq_ref[...]) * (hd ** -0.5)).astype(BF16)
    for h in range(XA_HEADS):
        cols = slice(hd * h, hd * (h + 1))
        sc = _dot_nt(q[:, cols], km_ref[0, :, cols])
        p = jnp.exp(sc - jnp.max(sc, axis=-1, keepdims=True))
        oh = _dot(p.astype(BF16), vm_ref[0, :, cols]) / jnp.sum(p, axis=-1, keepdims=True)
        obuf[:, cols] = oh.astype(BF16)
    o_ref[0] = x1 + _dot(obuf[...], wo_ref[...])


def _xa(x, ya, yb, yc, yd, wout, g, wq, kmem, vmem, wo):
    b, s, d = x.shape
    tm = min(TOKEN_TILE, s)
    m = kmem.shape[1]
    tok = lambda w: pl.BlockSpec((1, tm, w), lambda bi, si: (bi, si, 0))
    mem_spec = pl.BlockSpec((1, m, d), lambda bi, si: (bi, 0, 0))
    return pl.pallas_call(
        _xa_kernel,
        grid=(b, s // tm),
        in_specs=[tok(d), tok(256), tok(256), tok(256), tok(256),
                  _const_spec(wout.shape), _const_spec(g.shape), _const_spec(wq.shape),
                  mem_spec, mem_spec, _const_spec(wo.shape)],
        out_specs=tok(d),
        out_shape=jax.ShapeDtypeStruct((b, s, d), F32),
        scratch_shapes=[pltpu.VMEM((tm, d), BF16)],
        compiler_params=pltpu.CompilerParams(
            dimension_semantics=("arbitrary", "arbitrary"),
            vmem_limit_bytes=VMEM_LIMIT_BYTES),
        name="xa",
    )(x, ya, yb, yc, yd, wout, g, wq, kmem, vmem, wo)


def _ffn_kernel(x_ref, g_ref, wup_ref, dw_ref, dwb_ref, wdown_ref, gfin_ref, o_ref,
                cbuf, carry, acc_ref, *, tm, dff, final_norm):
    s = pl.program_id(1)

    @pl.when(s == 0)
    def _():
        carry[...] = jnp.zeros(carry.shape, F32)

    x = x_ref[0]
    h = x * lax.rsqrt(jnp.mean(x * x, axis=-1, keepdims=True) + 1e-6) * g_ref[...]
    hb = h.astype(BF16)
    nchunk = dff // FFN_CHUNK

    def conv(u, slot, col0):
        cbuf[slot, 0:FFN_HALO, :] = carry[slot, :, col0:col0 + FFN_CHUNK]
        cbuf[slot, FFN_HALO:FFN_HALO + tm, :] = u
        carry[slot, :, col0:col0 + FFN_CHUNK] = u[tm - FFN_HALO:tm, :]
        w = dw_ref[:, slot * dff + col0:slot * dff + col0 + FFN_CHUNK]
        out = dwb_ref[:, slot * dff + col0:slot * dff + col0 + FFN_CHUNK] + w[2:3, :] * u
        out = out + w[1:2, :] * cbuf[slot, FFN_HALO - 1:FFN_HALO - 1 + tm, :]
        return out + w[0:1, :] * cbuf[slot, FFN_HALO - 2:FFN_HALO - 2 + tm, :]

    for j in range(nchunk):
        col0 = j * FFN_CHUNK
        a = conv(_dot(hb, wup_ref[:, col0:col0 + FFN_CHUNK]), 0, col0)
        gt = conv(_dot(hb, wup_ref[:, dff + col0:dff + col0 + FFN_CHUNK]), 1, col0)
        act = (gt * _sigmoid(gt) * a).astype(BF16)
        part = _dot(act, wdown_ref[col0:col0 + FFN_CHUNK, :])
        if j == 0:
            acc_ref[...] = x + part
        else:
            acc_ref[...] += part
    y = acc_ref[...]
    if final_norm:
        y = y * lax.rsqrt(jnp.mean(y * y, axis=-1, keepdims=True) + 1e-6) * gfin_ref[...]
    o_ref[0] = y


def _ffn(x, g, wup, dw, dwb, wdown, gfin, final_norm):
    b, s, d = x.shape
    tm = min(TOKEN_TILE, s)
    dff = wdown.shape[0]
    tok = pl.BlockSpec((1, tm, d), lambda bi, si: (bi, si, 0))
    return pl.pallas_call(
        functools.partial(_ffn_kernel, tm=tm, dff=dff, final_norm=final_norm),
        grid=(b, s // tm),
        in_specs=[tok, _const_spec(g.shape), _const_spec(wup.shape), _const_spec(dw.shape),
                  _const_spec(dwb.shape), _const_spec(wdown.shape), _const_spec(gfin.shape)],
        out_specs=tok,
        out_shape=jax.ShapeDtypeStruct((b, s, d), F32),
        scratch_shapes=[pltpu.VMEM((2, FFN_HALO + tm, FFN_CHUNK), F32),
                        pltpu.VMEM((2, FFN_HALO, dff), F32),
                        pltpu.VMEM((tm, d), F32)],
        compiler_params=pltpu.CompilerParams(
            dimension_semantics=("arbitrary", "arbitrary"),
            vmem_limit_bytes=VMEM_LIMIT_BYTES),
        name="ffn",
    )(x, g, wup, dw, dwb, wdown, gfin)


def _pack_w_in(w):
    d = w.shape[0]
    ua, cq, ckv, kr, qc, kc, vc, ud = jnp.split(
        w, [512, 704, 832, 864, 1120, 1376, 1632], axis=1)
    z = lambda n: jnp.zeros((d, n), w.dtype)
    half = MLA_ROPE // 2
    kr_sw = jnp.concatenate([kr[:, half:], kr[:, :half]], axis=1)
    kr_blk = jnp.concatenate([z(MLA_NOPE), kr, z(LANES - MLA_NOPE - MLA_ROPE)], axis=1)
    kr_sw_blk = jnp.concatenate([z(MLA_NOPE), kr_sw, z(LANES - MLA_NOPE - MLA_ROPE)], axis=1)
    packed = jnp.concatenate(
        [ua, cq, z(256 - MLA_Q_RANK), ckv, kr_blk, kr_sw_blk, z(LANES), qc, kc, vc, ud], axis=1)
    return packed.astype(BF16)


def _pack_w_uq(w):
    w = w.reshape(MLA_Q_RANK, MLA_HEADS, MLA_NOPE + MLA_ROPE)
    nope, rp = w[..., :MLA_NOPE], w[..., MLA_NOPE:]
    half = MLA_ROPE // 2
    rp_sw = jnp.concatenate([rp[..., half:], rp[..., :half]], axis=-1)
    zn = jnp.zeros_like(nope)
    zp = jnp.zeros((MLA_Q_RANK, MLA_HEADS, LANES - MLA_NOPE - MLA_ROPE), w.dtype)

    def fin(a):
        a = a.reshape(MLA_Q_RANK, MLA_HEADS * LANES)
        return jnp.pad(a, ((0, 256 - MLA_Q_RANK), (0, 0))).astype(BF16)

    return (fin(jnp.concatenate([nope, rp, zp], axis=-1)),
            fin(jnp.concatenate([zn, rp_sw, zp], axis=-1)))


def _pack_w_ukv(w):
    w = w.reshape(MLA_KV_RANK, MLA_HEADS, MLA_NOPE + MLA_V)
    kn, v = w[..., :MLA_NOPE], w[..., MLA_NOPE:]
    kn = jnp.concatenate([kn, jnp.zeros_like(kn)], axis=-1).reshape(MLA_KV_RANK, MLA_HEADS * LANES)
    return jnp.concatenate([kn, v.reshape(MLA_KV_RANK, MLA_HEADS * MLA_V)], axis=1).astype(BF16)


def _rope_tables(positions):
    inv_freq = ROPE_BASE ** (-jnp.arange(0, MLA_ROPE, 2, dtype=F32) / MLA_ROPE)
    ang = positions.astype(F32)[:, None] * inv_freq[None, :]
    cos, sin = jnp.cos(ang), jnp.sin(ang)
    s = positions.shape[0]
    pad = jnp.zeros((s, LANES - MLA_NOPE - MLA_ROPE), F32)
    ctab = jnp.concatenate([jnp.ones((s, MLA_NOPE), F32), cos, cos, pad], axis=1)
    stab = jnp.concatenate([jnp.zeros((s, MLA_NOPE), F32), -sin, sin, pad], axis=1)
    return ctab, stab


def _block_diag(w):
    g, c, _ = w.shape
    out = jnp.zeros((g * c, g * c), w.dtype)
    for i in range(g):
        out = out.at[i * c:(i + 1) * c, i * c:(i + 1) * c].set(w[i])
    return out


def kernel(x, mem, positions, rel_bias, norm_mix, w_in, w_out, conv_dw, conv_dw_b, conv_ln_g, conv_ln_b, conv_pw, conv_pw_b, mla_q_norm, mla_w_uq, mla_kv_norm, mla_w_ukv, diff_lq1, diff_lk1, diff_lq2, diff_lk2, diff_subln, pool_w, pool_scale, norm_xa, mem_norm, xa_wq, xa_wk, xa_wv, xa_wo, norm_ffn, ffn_up, ffn_dw, ffn_dw_b, ffn_down, norm_final):
    depth = w_in.shape[0]
    s = x.shape[1]
    row = lambda a: a.reshape(1, -1)
    ctab, stab = _rope_tables(positions)
    bias_tiles = _bias_tiles(rel_bias, min(ATTN_TILE, s))
    kmem, vmem = _mem_kv(mem, mem_norm[:, None, :], xa_wk.astype(BF16), xa_wv.astype(BF16))
    for l in range(depth):
        wuq, wuqs = _pack_w_uq(mla_w_uq[l])
        qn = jnp.pad(mla_q_norm[l], (0, 256 - MLA_Q_RANK))
        ya, qm, km, vm, qd, kd, vd, yd = _mix_in(
            x, row(norm_mix[l]), _pack_w_in(w_in[l]), conv_dw[l], row(conv_dw_b[l]),
            row(conv_ln_g[l]), row(conv_ln_b[l]), conv_pw[l].astype(BF16), row(conv_pw_b[l]),
            row(qn), wuq, wuqs, row(mla_kv_norm[l]), _pack_w_ukv(mla_w_ukv[l]), ctab, stab,
            _block_diag(pool_w[l]).astype(BF16), row(pool_scale[l]))
        yb = _mla_attn(qm, km, vm)
        lambda_init = 0.8 - 0.6 * math.exp(-0.3 * l)
        yc = _diff_attn(qd, kd, vd, bias_tiles, row(diff_lq1[l]), row(diff_lk1[l]),
                        row(diff_lq2[l]), row(diff_lk2[l]),
                        row(jnp.tile(diff_subln[l], DIFF_HEADS)), lambda_init)
        x = _xa(x, ya, yb, yc, yd, w_out[l].astype(BF16), row(norm_xa[l]),
                xa_wq[l].astype(BF16), kmem[l], vmem[l], xa_wo[l].astype(BF16))
        x = _ffn(x, row(norm_ffn[l]), ffn_up[l].astype(BF16), ffn_dw[l], row(ffn_dw_b[l]),
                 ffn_down[l].astype(BF16), row(norm_final), l == depth - 1)
    return x
```

```python
import functools
import math

import jax
import jax.numpy as jnp
from jax import lax
from jax.experimental import pallas as pl
from jax.experimental.pallas import tpu as pltpu

N_GROUPS = 4
CONV_KERNEL = 31
MLA_HEADS = 4
MLA_NOPE = 64
MLA_ROPE = 32
MLA_V = 64
MLA_Q_RANK = 192
MLA_KV_RANK = 128
DIFF_HEADS = 4
DIFF_HALF = 32
DIFF_V = 64
POOL_WINDOWS = (2, 4, 8, 16)
POOL_CH = 64
REL_BUCKETS = 32
REL_MAX_DIST = 128
XA_HEADS = 4
FFN_CONV = 3
ROPE_BASE = 10000.0
NEG_INF = -1e30
LOG2_E = math.log2(math.e)

LANES = 128
SUBLANES = 8
VMEM_LIMIT_BYTES = 56 * 1024 * 1024

TOKEN_TILE = 512
ATTN_TILE = 256
CONV_HALO = 32
POOL_HALO = 16
FFN_HALO = 8
ROW_CHUNK = 64
FFN_CHUNK = 256

BF16 = jnp.bfloat16
F32 = jnp.float32


def _dot(a, b):
    return jnp.dot(a, b, preferred_element_type=F32)


def _dot_nt(a, b):
    return lax.dot_general(a, b, (((1,), (1,)), ((), ())), preferred_element_type=F32)


def _sigmoid(x):
    return 1.0 / (1.0 + jnp.exp(-x))


def _const_spec(shape):
    nd = len(shape)
    return pl.BlockSpec(shape, lambda *_: (0,) * nd)


_A0, _A1 = 0, 512
_B0, _B1 = 512, 1280
_C0, _C1 = 1280, 2048
_D0, _D1 = 2048, 2304
W_IN_COLS = 2304


def _mix_in_kernel(x_ref, g_ref, win_ref, dw_ref, dwb_ref, lng_ref, lnb_ref, pw_ref, pwb_ref,
                   qn_ref, wuq_ref, wuqs_ref, kvn_ref, wukv_ref, ct_ref, st_ref,
                   poolw_ref, pools_ref,
                   ya_ref, qm_ref, km_ref, vm_ref, qd_ref, kd_ref, vd_ref, yd_ref,
                   hbuf, zbuf, ubuf, dbuf, *, tm):
    s = pl.program_id(1)

    @pl.when(s == 0)
    def _():
        hbuf[0:CONV_HALO, :] = jnp.zeros((CONV_HALO, 256), F32)
        ubuf[0:POOL_HALO, :] = jnp.zeros((POOL_HALO, 256), F32)

    x = x_ref[0]
    h = x * lax.rsqrt(jnp.mean(x * x, axis=-1, keepdims=True) + 1e-6) * g_ref[...]
    hb = h.astype(BF16)

    pa = _dot(hb, win_ref[:, _A0:_A1])
    hbuf[CONV_HALO:CONV_HALO + tm, :] = pa[:, 0:256] * _sigmoid(pa[:, 256:512])
    for c in range(tm // ROW_CHUNK):
        r0 = c * ROW_CHUNK
        acc = jnp.broadcast_to(dwb_ref[...], (ROW_CHUNK, 256))
        for k in range(CONV_KERNEL):
            off = r0 + CONV_HALO - (CONV_KERNEL - 1) + k
            acc = acc + dw_ref[k:k + 1, :] * hbuf[off:off + ROW_CHUNK, :]
        mu = jnp.mean(acc, axis=-1, keepdims=True)
        cen = acc - mu
        var = jnp.mean(cen * cen, axis=-1, keepdims=True)
        yn = cen * lax.rsqrt(var + 1e-5) * lng_ref[...] + lnb_ref[...]
        zbuf[r0:r0 + ROW_CHUNK, :] = (yn * _sigmoid(yn)).astype(BF16)
    hbuf[0:CONV_HALO, :] = hbuf[tm:tm + CONV_HALO, :]
    ya_ref[0] = (_dot(zbuf[...], pw_ref[...]) + pwb_ref[...]).astype(BF16)

    pb = _dot(hb, win_ref[:, _B0:_B1])
    ct = ct_ref[...]
    st = st_ref[...]
    cq = pb[:, 0:256]
    cqn = cq * lax.rsqrt(jnp.sum(cq * cq, axis=-1, keepdims=True) * (1.0 / MLA_Q_RANK) + 1e-6)
    cqb = (cqn * qn_ref[...]).astype(BF16)
    q = _dot(cqb, wuq_ref[...])
    qs = _dot(cqb, wuqs_ref[...])
    ct4 = jnp.concatenate([ct] * MLA_HEADS, axis=1)
    st4 = jnp.concatenate([st] * MLA_HEADS, axis=1)
    q_scale = LOG2_E * (MLA_NOPE + MLA_ROPE) ** -0.5
    qm_ref[0] = ((q * ct4 + qs * st4) * q_scale).astype(BF16)
    ckv = pb[:, 256:384]
    ckvn = ckv * lax.rsqrt(jnp.mean(ckv * ckv, axis=-1, keepdims=True) + 1e-6)
    kv = _dot((ckvn * kvn_ref[...]).astype(BF16), wukv_ref[...])
    kr = pb[:, 384:512] * ct + pb[:, 512:640] * st
    km_ref[0] = (kv[:, 0:512] + jnp.concatenate([kr] * MLA_HEADS, axis=1)).astype(BF16)
    vm_ref[0] = kv[:, 512:768].astype(BF16)

    pc = _dot(hb, win_ref[:, _C0:_C1])
    qd_ref[0] = (pc[:, 0:256] * (LOG2_E * DIFF_HALF ** -0.5)).astype(BF16)
    kd_ref[0] = pc[:, 256:512].astype(BF16)
    vd_ref[0] = pc[:, 512:768].astype(BF16)

    ubuf[POOL_HALO:POOL_HALO + tm, :] = _dot(hb, win_ref[:, _D0:_D1])
    lane = lax.broadcasted_iota(jnp.int32, (ROW_CHUNK, 256), 1)
    grp = lane // POOL_CH
    win_len = jnp.where(grp == 0, POOL_WINDOWS[0],
                        jnp.where(grp == 1, POOL_WINDOWS[1],
                                  jnp.where(grp == 2, POOL_WINDOWS[2], POOL_WINDOWS[3])))
    for c in range(tm // ROW_CHUNK):
        r0 = c * ROW_CHUNK
        u = ubuf[POOL_HALO + r0:POOL_HALO + r0 + ROW_CHUNK, :]
        run = u
        sums = {}
        for j in range(1, POOL_WINDOWS[-1]):
            run = run + ubuf[POOL_HALO + r0 - j:POOL_HALO + r0 - j + ROW_CHUNK, :]
            if j + 1 in POOL_WINDOWS:
                sums[j + 1] = run
        win = jnp.where(grp == 0, sums[2],
                        jnp.where(grp == 1, sums[4], jnp.where(grp == 2, sums[8], sums[16])))
        t_glob = s * tm + r0 + lax.broadcasted_iota(jnp.int32, (ROW_CHUNK, 256), 0)
        cnt = jnp.minimum(t_glob + 1, win_len).astype(F32)
        dbuf[r0:r0 + ROW_CHUNK, :] = (win / cnt - u).astype(BF16)
    ubuf[0:POOL_HALO, :] = ubuf[tm:tm + POOL_HALO, :]
    yd_ref[0] = (_dot(dbuf[...], poolw_ref[...]) * pools_ref[...]).astype(BF16)


def _mix_in(x, g, w_in, dw, dwb, lng, lnb, pw, pwb, qn, wuq, wuqs, kvn, wukv, ctab, stab,
            poolw, pools):
    b, s, d = x.shape
    tm = min(TOKEN_TILE, s)
    tok = lambda w: pl.BlockSpec((1, tm, w), lambda bi, si: (bi, si, 0))
    consts = [g, w_in, dw, dwb, lng, lnb, pw, pwb, qn, wuq, wuqs, kvn, wukv]
    in_specs = ([tok(d)] + [_const_spec(a.shape) for a in consts]
                + [pl.BlockSpec((tm, LANES), lambda bi, si: (si, 0))] * 2
                + [_const_spec(poolw.shape), _const_spec(pools.shape)])
    widths = (256, 512, 512, 256, 256, 256, 256, 256)
    return pl.pallas_call(
        functools.partial(_mix_in_kernel, tm=tm),
        grid=(b, s // tm),
        in_specs=in_specs,
        out_specs=[tok(w) for w in widths],
        out_shape=[jax.ShapeDtypeStruct((b, s, w), BF16) for w in widths],
        scratch_shapes=[pltpu.VMEM((CONV_HALO + tm, 256), F32),
                        pltpu.VMEM((tm, 256), BF16),
                        pltpu.VMEM((POOL_HALO + tm, 256), F32),
                        pltpu.VMEM((tm, 256), BF16)],
        compiler_params=pltpu.CompilerParams(
            dimension_semantics=("arbitrary", "arbitrary"),
            vmem_limit_bytes=VMEM_LIMIT_BYTES),
        name="mix_in",
    )(x, *consts, ctab, stab, poolw, pools)


def _head_lanes(vals, rows):
    lane = lax.broadcasted_iota(jnp.int32, (rows, LANES), 1)
    return jnp.concatenate([jnp.where(lane < 64, vals[0], vals[1]),
                            jnp.where(lane < 64, vals[2], vals[3])], axis=1)


def _row_total(l_ref, idx):
    return jnp.sum(l_ref[idx], axis=-1, keepdims=True)


def _stack_masked_v(v_blk, vst_ref, t):
    lane = lax.broadcasted_iota(jnp.int32, (t, 256), 1)
    for h in range(4):
        keep = (lane >= 64 * h) & (lane < 64 * (h + 1))
        vst_ref[h * t:(h + 1) * t, :] = jnp.where(keep, v_blk, jnp.zeros_like(v_blk))


def _softmax_step(sc, m_ref, l_ref, idx):
    m_old = m_ref[idx]
    m_new = jnp.maximum(m_old, jnp.max(sc, axis=-1, keepdims=True))
    alpha = jnp.exp2(m_old - m_new)
    reps = sc.shape[1] // LANES
    p = jnp.exp2(sc - jnp.concatenate([m_new] * reps, axis=1))
    psum = p[:, 0:LANES]
    for r in range(1, reps):
        psum = psum + p[:, r * LANES:(r + 1) * LANES]
    l_ref[idx] = alpha * l_ref[idx] + psum
    m_ref[idx] = m_new
    return p, alpha


def _mla_attn_kernel(q_ref, k_ref, v_ref, o_ref, vst_ref, p_ref, m_ref, l_ref, acc_ref, *, t):
    qi = pl.program_id(1)
    m_ref[...] = jnp.full(m_ref.shape, NEG_INF, F32)
    l_ref[...] = jnp.zeros(l_ref.shape, F32)
    acc_ref[...] = jnp.zeros(acc_ref.shape, F32)
    row = lax.broadcasted_iota(jnp.int32, (t, t), 0)
    col = lax.broadcasted_iota(jnp.int32, (t, t), 1)
    causal = row >= col

    def step(kj, diagonal):
        k0 = pl.multiple_of(kj * t, t)
        k_blk = k_ref[0, pl.ds(k0, t), :]
        _stack_masked_v(v_ref[0, pl.ds(k0, t), :], vst_ref, t)
        alphas = []
        for h in range(MLA_HEADS):
            sc = _dot_nt(q_ref[0, :, LANES * h:LANES * (h + 1)], k_blk[:, LANES * h:LANES * (h + 1)])
            if diagonal:
                sc = jnp.where(causal, sc, NEG_INF)
            p, alpha = _softmax_step(sc, m_ref, l_ref, h)
            p_ref[:, h * t:(h + 1) * t] = p.astype(BF16)
            alphas.append(alpha)
        acc_ref[...] = acc_ref[...] * _head_lanes(alphas, t) + _dot(p_ref[...], vst_ref[...])

    def body(kj, carry):
        step(kj, False)
        return carry

    lax.fori_loop(0, qi, body, 0)
    step(qi, True)
    inv_l = _head_lanes([1.0 / _row_total(l_ref, h) for h in range(MLA_HEADS)], t)
    o_ref[0] = (acc_ref[...] * inv_l).astype(BF16)


def _mla_attn(q, k, v):
    b, s, _ = q.shape
    t = min(ATTN_TILE, s)
    return pl.pallas_call(
        functools.partial(_mla_attn_kernel, t=t),
        grid=(b, s // t),
        in_specs=[pl.BlockSpec((1, t, 512), lambda bi, qi: (bi, qi, 0)),
                  pl.BlockSpec((1, s, 512), lambda bi, qi: (bi, 0, 0)),
                  pl.BlockSpec((1, s, 256), lambda bi, qi: (bi, 0, 0))],
        out_specs=pl.BlockSpec((1, t, 256), lambda bi, qi: (bi, qi, 0)),
        out_shape=jax.ShapeDtypeStruct((b, s, 256), BF16),
        scratch_shapes=[pltpu.VMEM((4 * t, 256), BF16),
                        pltpu.VMEM((t, 4 * t), BF16),
                        pltpu.VMEM((MLA_HEADS, t, LANES), F32),
                        pltpu.VMEM((MLA_HEADS, t, LANES), F32),
                        pltpu.VMEM((t, 256), F32)],
        compiler_params=pltpu.CompilerParams(
            dimension_semantics=("arbitrary", "arbitrary"),
            vmem_limit_bytes=VMEM_LIMIT_BYTES),
        name="mla_attn",
    )(q, k, v)


def _bias_tiles_kernel(rb_ref, out_ref, *, t):
    row = lax.broadcasted_iota(jnp.int32, (t, t), 0)
    col = lax.broadcasted_iota(jnp.int32, (t, t), 1)
    max_exact = REL_BUCKETS // 2
    for j in range(2):
        rel = (1 - j) * t + row - col
        n = jnp.maximum(rel, 0)
        nf = jnp.maximum(n, 1).astype(F32)
        large = max_exact + (jnp.log(nf / max_exact) / math.log(REL_MAX_DIST / max_exact)
                             * (REL_BUCKETS - max_exact)).astype(jnp.int32)
        large = jnp.minimum(large, REL_BUCKETS - 1)
        bucket = jnp.where(n < max_exact, n, large)
        for h in range(DIFF_HEADS):
            far = rb_ref[REL_BUCKETS - 1, h]
            val = jnp.zeros((t, t), F32)
            for bkt in range(REL_BUCKETS - 1):
                val = jnp.where(bucket == bkt, (rb_ref[bkt, h] - far) * LOG2_E, val)
            out_ref[h, j] = jnp.where(rel >= 0, val, NEG_INF)


def _bias_tiles(rel_bias, t):
    return pl.pallas_call(
        functools.partial(_bias_tiles_kernel, t=t),
        in_specs=[pl.BlockSpec(memory_space=pltpu.SMEM)],
        out_specs=pl.BlockSpec(memory_space=pltpu.VMEM),
        out_shape=jax.ShapeDtypeStruct((DIFF_HEADS, 2, t, t), F32),
        name="bias_tiles",
    )(rel_bias)


def _diff_attn_kernel(q_ref, k_ref, v_ref, bias_ref, lq1_ref, lk1_ref, lq2_ref, lk2_ref,
                      subln_ref, o_ref, qst_ref, vst_ref, p_ref, m_ref, l_ref, acc_ref,
                      *, t, lambda_init):
    qi = pl.program_id(1)
    nmaps = 2 * DIFF_HEADS
    m_ref[...] = jnp.full(m_ref.shape, NEG_INF, F32)
    l_ref[...] = jnp.zeros(l_ref.shape, F32)
    acc_ref[...] = jnp.zeros(acc_ref.shape, F32)
    q = q_ref[0]
    lane = lax.broadcasted_iota(jnp.int32, (t, 256), 1)
    for idx in range(nmaps):
        keep = (lane >= DIFF_HALF * idx) & (lane < DIFF_HALF * (idx + 1))
        qst_ref[idx * t:(idx + 1) * t, :] = jnp.where(keep, q, jnp.zeros_like(q))

    def step(kj, bias_tile):
        k0 = pl.multiple_of(kj * t, t)
        _stack_masked_v(v_ref[0, pl.ds(k0, t), :], vst_ref, t)
        sc_all = _dot_nt(qst_ref[...], k_ref[0, pl.ds(k0, t), :])
        alphas = [[], []]
        for h in range(DIFF_HEADS):
            for which in range(2):
                idx = 2 * h + which
                sc = sc_all[idx * t:(idx + 1) * t, :]
                if bias_tile is not None:
                    sc = sc + bias_ref[h, bias_tile]
                p, alpha = _softmax_step(sc, m_ref, l_ref, idx)
                p_ref[which * t:(which + 1) * t, h * t:(h + 1) * t] = p.astype(BF16)
                alphas[which].append(alpha)
        pv = _dot(p_ref[...], vst_ref[...])
        for which in range(2):
            rows = slice(which * t, (which + 1) * t)
            acc_ref[rows, :] = acc_ref[rows, :] * _head_lanes(alphas[which], t) + pv[rows, :]

    def body(kj, carry):
        step(kj, None)
        return carry

    lax.fori_loop(0, jnp.maximum(qi - 1, 0), body, 0)

    @pl.when(qi > 0)
    def _():
        step(qi - 1, 0)

    step(qi, 1)

    lam = (jnp.exp(jnp.sum(lq1_ref[...] * lk1_ref[...], axis=-1, keepdims=True))
           - jnp.exp(jnp.sum(lq2_ref[...] * lk2_ref[...], axis=-1, keepdims=True))
           + lambda_init)
    o1 = acc_ref[0:t, :] * _head_lanes([1.0 / _row_total(l_ref, 2 * h) for h in range(DIFF_HEADS)], t)
    o2 = acc_ref[t:2 * t, :] * _head_lanes(
        [1.0 / _row_total(l_ref, 2 * h + 1) for h in range(DIFF_HEADS)], t)
    o = o1 - lam * o2
    osq = o * o
    ms = []
    for h in range(DIFF_HEADS):
        keep = (lane >= DIFF_V * h) & (lane < DIFF_V * (h + 1))
        ms.append(jnp.sum(jnp.where(keep, osq, 0.0), axis=-1, keepdims=True) * (1.0 / DIFF_V))
    on = o * lax.rsqrt(_head_lanes(ms, t) + 1e-5) * subln_ref[...]
    o_ref[0] = (on * (1.0 - lambda_init)).astype(BF16)


def _diff_attn(q, k, v, bias_tiles, lq1, lk1, lq2, lk2, subln, lambda_init):
    b, s, _ = q.shape
    t = min(ATTN_TILE, s)
    return pl.pallas_call(
        functools.partial(_diff_attn_kernel, t=t, lambda_init=lambda_init),
        grid=(b, s // t),
        in_specs=[pl.BlockSpec((1, t, 256), lambda bi, qi: (bi, qi, 0)),
                  pl.BlockSpec((1, s, 256), lambda bi, qi: (bi, 0, 0)),
                  pl.BlockSpec((1, s, 256), lambda bi, qi: (bi, 0, 0)),
                  _const_spec(bias_tiles.shape),
                  _const_spec(lq1.shape), _const_spec(lk1.shape),
                  _const_spec(lq2.shape), _const_spec(lk2.shape),
                  _const_spec(subln.shape)],
        out_specs=pl.BlockSpec((1, t, 256), lambda bi, qi: (bi, qi, 0)),
        out_shape=jax.ShapeDtypeStruct((b, s, 256), BF16),
        scratch_shapes=[pltpu.VMEM((8 * t, 256), BF16),
                        pltpu.VMEM((4 * t, 256), BF16),
                        pltpu.VMEM((2 * t, 4 * t), BF16),
                        pltpu.VMEM((2 * DIFF_HEADS, t, LANES), F32),
                        pltpu.VMEM((2 * DIFF_HEADS, t, LANES), F32),
                        pltpu.VMEM((2 * t, 256), F32)],
        compiler_params=pltpu.CompilerParams(
            dimension_semantics=("arbitrary", "arbitrary"),
            vmem_limit_bytes=VMEM_LIMIT_BYTES),
        name="diff_attn",
    )(q, k, v, bias_tiles, lq1, lk1, lq2, lk2, subln)


def _mem_kv_kernel(mem_ref, g_ref, wk_ref, wv_ref, k_ref, v_ref):
    m = mem_ref[0]
    mn = m * lax.rsqrt(jnp.mean(m * m, axis=-1, keepdims=True) + 1e-6) * g_ref[0]
    mb = mn.astype(BF16)
    k_ref[0, 0] = _dot(mb, wk_ref[0]).astype(BF16)
    v_ref[0, 0] = _dot(mb, wv_ref[0]).astype(BF16)


def _mem_kv(mem, g, wk, wv):
    b, m, d = mem.shape
    nl = wk.shape[0]
    out = jax.ShapeDtypeStruct((nl, b, m, d), BF16)
    return pl.pallas_call(
        _mem_kv_kernel,
        grid=(nl, b),
        in_specs=[pl.BlockSpec((1, m, d), lambda li, bi: (bi, 0, 0)),
                  pl.BlockSpec((1, 1, d), lambda li, bi: (li, 0, 0)),
                  pl.BlockSpec((1, d, d), lambda li, bi: (li, 0, 0)),
                  pl.BlockSpec((1, d, d), lambda li, bi: (li, 0, 0))],
        out_specs=[pl.BlockSpec((1, 1, m, d), lambda li, bi: (li, bi, 0, 0))] * 2,
        out_shape=[out, out],
        compiler_params=pltpu.CompilerParams(
            dimension_semantics=("arbitrary", "arbitrary"),
            vmem_limit_bytes=VMEM_LIMIT_BYTES),
        name="mem_kv",
    )(mem, g, wk, wv)


def _xa_kernel(x_ref, ya_ref, yb_ref, yc_ref, yd_ref, wout_ref, g_ref, wq_ref, km_ref, vm_ref,
               wo_ref, o_ref, obuf):
    x1 = x_ref[0]
    for i, y_ref in enumerate((ya_ref, yb_ref, yc_ref, yd_ref)):
        x1 = x1 + _dot(y_ref[0], wout_ref[256 * i:256 * (i + 1), :])
    hx = x1 * lax.rsqrt(jnp.mean(x1 * x1, axis=-1, keepdims=True) + 1e-6) * g_ref[...]
    hd = wq_ref.shape[1] // XA_HEADS
    q = (_dot(hx.astype(BF16), wq_ref[...]) * (hd ** -0.5)).astype(BF16)
    for h in range(XA_HEADS):
        cols = slice(hd * h, hd * (h + 1))
        sc = _dot_nt(q[:, cols], km_ref[0, :, cols])
        p = jnp.exp(sc - jnp.max(sc, axis=-1, keepdims=True))
        oh = _dot(p.astype(BF16), vm_ref[0, :, cols]) / jnp.sum(p, axis=-1, keepdims=True)
        obuf[:, cols] = oh.astype(BF16)
    o_ref[0] = x1 + _dot(obuf[...], wo_ref[...])


def _xa(x, ya, yb, yc, yd, wout, g, wq, kmem, vmem, wo):
    b, s, d = x.shape
    tm = min(TOKEN_TILE, s)
    m = kmem.shape[1]
    tok = lambda w: pl.BlockSpec((1, tm, w), lambda bi, si: (bi, si, 0))
    mem_spec = pl.BlockSpec((1, m, d), lambda bi, si: (bi, 0, 0))
    return pl.pallas_call(
        _xa_kernel,
        grid=(b, s // tm),
        in_specs=[tok(d), tok(256), tok(256), tok(256), tok(256),
                  _const_spec(wout.shape), _const_spec(g.shape), _const_spec(wq.shape),
                  mem_spec, mem_spec, _const_spec(wo.shape)],
        out_specs=tok(d),
        out_shape=jax.ShapeDtypeStruct((b, s, d), F32),
        scratch_shapes=[pltpu.VMEM((tm, d), BF16)],
        compiler_params=pltpu.CompilerParams(
            dimension_semantics=("arbitrary", "arbitrary"),
            vmem_limit_bytes=VMEM_LIMIT_BYTES),
        name="xa",
    )(x, ya, yb, yc, yd, wout, g, wq, kmem, vmem, wo)


def _ffn_kernel(x_ref, g_ref, wup_ref, dw_ref, dwb_ref, wdown_ref, gfin_ref, o_ref,
                cbuf, carry, acc_ref, *, tm, dff, final_norm):
    s = pl.program_id(1)

    @pl.when(s == 0)
    def _():
        carry[...] = jnp.zeros(carry.shape, F32)

    x = x_ref[0]
    h = x * lax.rsqrt(jnp.mean(x * x, axis=-1, keepdims=True) + 1e-6) * g_ref[...]
    hb = h.astype(BF16)
    nchunk = dff // FFN_CHUNK

    def conv(u, slot, col0):
        cbuf[slot, 0:FFN_HALO, :] = carry[slot, :, col0:col0 + FFN_CHUNK]
        cbuf[slot, FFN_HALO:FFN_HALO + tm, :] = u
        carry[slot, :, col0:col0 + FFN_CHUNK] = u[tm - FFN_HALO:tm, :]
        w = dw_ref[:, slot * dff + col0:slot * dff + col0 + FFN_CHUNK]
        out = dwb_ref[:, slot * dff + col0:slot * dff + col0 + FFN_CHUNK] + w[2:3, :] * u
        out = out + w[1:2, :] * cbuf[slot, FFN_HALO - 1:FFN_HALO - 1 + tm, :]
        return out + w[0:1, :] * cbuf[slot, FFN_HALO - 2:FFN_HALO - 2 + tm, :]

    for j in range(nchunk):
        col0 = j * FFN_CHUNK
        a = conv(_dot(hb, wup_ref[:, col0:col0 + FFN_CHUNK]), 0, col0)
        gt = conv(_dot(hb, wup_ref[:, dff + col0:dff + col0 + FFN_CHUNK]), 1, col0)
        act = (gt * _sigmoid(gt) * a).astype(BF16)
        part = _dot(act, wdown_ref[col0:col0 + FFN_CHUNK, :])
        if j == 0:
            acc_ref[...] = x + part
        else:
            acc_ref[...] += part
    y = acc_ref[...]
    if final_norm:
        y = y * lax.rsqrt(jnp.mean(y * y, axis=-1, keepdims=True) + 1e-6) * gfin_ref[...]
    o_ref[0] = y


def _ffn(x, g, wup, dw, dwb, wdown, gfin, final_norm):
    b, s, d = x.shape
    tm = min(TOKEN_TILE, s)
    dff = wdown.shape[0]
    tok = pl.BlockSpec((1, tm, d), lambda bi, si: (bi, si, 0))
    return pl.pallas_call(
        functools.partial(_ffn_kernel, tm=tm, dff=dff, final_norm=final_norm),
        grid=(b, s // tm),
        in_specs=[tok, _const_spec(g.shape), _const_spec(wup.shape), _const_spec(dw.shape),
                  _const_spec(dwb.shape), _const_spec(wdown.shape), _const_spec(gfin.shape)],
        out_specs=tok,
        out_shape=jax.ShapeDtypeStruct((b, s, d), F32),
        scratch_shapes=[pltpu.VMEM((2, FFN_HALO + tm, FFN_CHUNK), F32),
                        pltpu.VMEM((2, FFN_HALO, dff), F32),
                        pltpu.VMEM((tm, d), F32)],
        compiler_params=pltpu.CompilerParams(
            dimension_semantics=("arbitrary", "arbitrary"),
            vmem_limit_bytes=VMEM_LIMIT_BYTES),
        name="ffn",
    )(x, g, wup, dw, dwb, wdown, gfin)


def _pack_w_in(w):
    d = w.shape[0]
    ua, cq, ckv, kr, qc, kc, vc, ud = jnp.split(
        w, [512, 704, 832, 864, 1120, 1376, 1632], axis=1)
    z = lambda n: jnp.zeros((d, n), w.dtype)
    half = MLA_ROPE // 2
    kr_sw = jnp.concatenate([kr[:, half:], kr[:, :half]], axis=1)
    kr_blk = jnp.concatenate([z(MLA_NOPE), kr, z(LANES - MLA_NOPE - MLA_ROPE)], axis=1)
    kr_sw_blk = jnp.concatenate([z(MLA_NOPE), kr_sw, z(LANES - MLA_NOPE - MLA_ROPE)], axis=1)
    packed = jnp.concatenate(
        [ua, cq, z(256 - MLA_Q_RANK), ckv, kr_blk, kr_sw_blk, z(LANES), qc, kc, vc, ud], axis=1)
    return packed.astype(BF16)


def _pack_w_uq(w):
    w = w.reshape(MLA_Q_RANK, MLA_HEADS, MLA_NOPE + MLA_ROPE)
    nope, rp = w[..., :MLA_NOPE], w[..., MLA_NOPE:]
    half = MLA_ROPE // 2
    rp_sw = jnp.concatenate([rp[..., half:], rp[..., :half]], axis=-1)
    zn = jnp.zeros_like(nope)
    zp = jnp.zeros((MLA_Q_RANK, MLA_HEADS, LANES - MLA_NOPE - MLA_ROPE), w.dtype)

    def fin(a):
        a = a.reshape(MLA_Q_RANK, MLA_HEADS * LANES)
        return jnp.pad(a, ((0, 256 - MLA_Q_RANK), (0, 0))).astype(BF16)

    return (fin(jnp.concatenate([nope, rp, zp], axis=-1)),
            fin(jnp.concatenate([zn, rp_sw, zp], axis=-1)))


def _pack_w_ukv(w):
    w = w.reshape(MLA_KV_RANK, MLA_HEADS, MLA_NOPE + MLA_V)
    kn, v = w[..., :MLA_NOPE], w[..., MLA_NOPE:]
    kn = jnp.concatenate([kn, jnp.zeros_like(kn)], axis=-1).reshape(MLA_KV_RANK, MLA_HEADS * LANES)
    return jnp.concatenate([kn, v.reshape(MLA_KV_RANK, MLA_HEADS * MLA_V)], axis=1).astype(BF16)


def _rope_tables(positions):
    inv_freq = ROPE_BASE ** (-jnp.arange(0, MLA_ROPE, 2, dtype=F32) / MLA_ROPE)
    ang = positions.astype(F32)[:, None] * inv_freq[None, :]
    cos, sin = jnp.cos(ang), jnp.sin(ang)
    s = positions.shape[0]
    pad = jnp.zeros((s, LANES - MLA_NOPE - MLA_ROPE), F32)
    ctab = jnp.concatenate([jnp.ones((s, MLA_NOPE), F32), cos, cos, pad], axis=1)
    stab = jnp.concatenate([jnp.zeros((s, MLA_NOPE), F32), -sin, sin, pad], axis=1)
    return ctab, stab


def _block_diag(w):
    g, c, _ = w.shape
    out = jnp.zeros((g * c, g * c), w.dtype)
    for i in range(g):
        out = out.at[i * c:(i + 1) * c, i * c:(i + 1) * c].set(w[i])
    return out


def kernel(x, mem, positions, rel_bias, norm_mix, w_in, w_out, conv_dw, conv_dw_b, conv_ln_g, conv_ln_b, conv_pw, conv_pw_b, mla_q_norm, mla_w_uq, mla_kv_norm, mla_w_ukv, diff_lq1, diff_lk1, diff_lq2, diff_lk2, diff_subln, pool_w, pool_scale, norm_xa, mem_norm, xa_wq, xa_wk, xa_wv, xa_wo, norm_ffn, ffn_up, ffn_dw, ffn_dw_b, ffn_down, norm_final):
    depth = w_in.shape[0]
    s = x.shape[1]
    row = lambda a: a.reshape(1, -1)
    ctab, stab = _rope_tables(positions)
    bias_tiles = _bias_tiles(rel_bias, min(ATTN_TILE, s))
    kmem, vmem = _mem_kv(mem, mem_norm[:, None, :], xa_wk.astype(BF16), xa_wv.astype(BF16))
    for l in range(depth):
        wuq, wuqs = _pack_w_uq(mla_w_uq[l])
        qn = jnp.pad(mla_q_norm[l], (0, 256 - MLA_Q_RANK))
        ya, qm, km, vm, qd, kd, vd, yd = _mix_in(
            x, row(norm_mix[l]), _pack_w_in(w_in[l]), conv_dw[l], row(conv_dw_b[l]),
            row(conv_ln_g[l]), row(conv_ln_b[l]), conv_pw[l].astype(BF16), row(conv_pw_b[l]),
            row(qn), wuq, wuqs, row(mla_kv_norm[l]), _pack_w_ukv(mla_w_ukv[l]), ctab, stab,
            _block_diag(pool_w[l]).astype(BF16), row(pool_scale[l]))
        yb = _mla_attn(qm, km, vm)
        lambda_init = 0.8 - 0.6 * math.exp(-0.3 * l)
        yc = _diff_attn(qd, kd, vd, bias_tiles, row(diff_lq1[l]), row(diff_lk1[l]),
                        row(diff_lq2[l]), row(diff_lk2[l]),
                        row(jnp.tile(diff_subln[l], DIFF_HEADS)), lambda_init)
        x = _xa(x, ya, yb, yc, yd, w_out[l].astype(BF16), row(norm_xa[l]),
                xa_wq[l].astype(BF16), kmem[l], vmem[l], xa_wo[l].astype(BF16))
        x = _ffn(x, row(norm_ffn[l]), ffn_up[l].astype(BF16), ffn_dw[l], row(ffn_dw_b[l]),
                 ffn_down[l].astype(BF16), row(norm_final), l == depth - 1)
    return x
```

```python
import functools
import math

import jax
import jax.numpy as jnp
from jax import lax
from jax.experimental import pallas as pl
from jax.experimental.pallas import tpu as pltpu

N_GROUPS = 4
CONV_KERNEL = 31
MLA_HEADS = 4
MLA_NOPE = 64
MLA_ROPE = 32
MLA_V = 64
MLA_Q_RANK = 192
MLA_KV_RANK = 128
DIFF_HEADS = 4
DIFF_HALF = 32
DIFF_V = 64
POOL_WINDOWS = (2, 4, 8, 16)
POOL_CH = 64
REL_BUCKETS = 32
REL_MAX_DIST = 128
XA_HEADS = 4
FFN_CONV = 3
ROPE_BASE = 10000.0
NEG_INF = -1e30
LOG2_E = math.log2(math.e)
NORM_MARGIN = 1.02
SAFE_LOGIT_SPAN = 100.0

LANES = 128
SUBLANES = 8
VMEM_LIMIT_BYTES = 56 * 1024 * 1024

TOKEN_TILE = 512
ATTN_TILE = 256
MLA_FAR_KEYS = 1024
CONV_HALO = 32
POOL_HALO = 16
FFN_HALO = 8
ROW_CHUNK = 64
FFN_CHUNK = 256

BF16 = jnp.bfloat16
F32 = jnp.float32


def _dot(a, b):
    return jnp.dot(a, b, preferred_element_type=F32)


def _dot_nt(a, b):
    return lax.dot_general(a, b, (((1,), (1,)), ((), ())), preferred_element_type=F32)


def _sigmoid(x):
    return 1.0 / (1.0 + jnp.exp(-x))


def _const_spec(shape):
    nd = len(shape)
    return pl.BlockSpec(shape, lambda *_: (0,) * nd)


_A0, _A1 = 0, 512
_B0, _B1 = 512, 1280
_C0, _C1 = 1280, 2048
_D0, _D1 = 2048, 2304
W_IN_COLS = 2304


def _mix_in_kernel(x_ref, g_ref, win_ref, dw_ref, dwb_ref, lng_ref, lnb_ref, pw_ref, pwb_ref,
                   qn_ref, wuq_ref, wuqs_ref, kvn_ref, wukv_ref, ct_ref, st_ref,
                   poolw_ref, pools_ref, gm_ref, gd_ref,
                   ya_ref, qm_ref, km_ref, vm_ref, qd_ref, kd_ref, vd_ref, yd_ref, stats_ref,
                   hbuf, zbuf, ubuf, dbuf, *, tm):
    s = pl.program_id(1)

    def max_sq_norm(v, group_ref):
        vf = v.astype(F32)
        return jnp.max(_dot((vf * vf).astype(BF16), group_ref[...]), axis=0, keepdims=True)

    @pl.when(s == 0)
    def _():
        hbuf[0:CONV_HALO, :] = jnp.zeros((CONV_HALO, 256), F32)
        ubuf[0:POOL_HALO, :] = jnp.zeros((POOL_HALO, 256), F32)

    x = x_ref[0]
    h = x * lax.rsqrt(jnp.mean(x * x, axis=-1, keepdims=True) + 1e-6) * g_ref[...]
    hb = h.astype(BF16)

    pa = _dot(hb, win_ref[:, _A0:_A1])
    hbuf[CONV_HALO:CONV_HALO + tm, :] = pa[:, 0:256] * _sigmoid(pa[:, 256:512])
    for c in range(tm // ROW_CHUNK):
        r0 = c * ROW_CHUNK
        acc = jnp.broadcast_to(dwb_ref[...], (ROW_CHUNK, 256))
        for k in range(CONV_KERNEL):
            off = r0 + CONV_HALO - (CONV_KERNEL - 1) + k
            acc = acc + dw_ref[k:k + 1, :] * hbuf[off:off + ROW_CHUNK, :]
        mu = jnp.mean(acc, axis=-1, keepdims=True)
        cen = acc - mu
        var = jnp.mean(cen * cen, axis=-1, keepdims=True)
        yn = cen * lax.rsqrt(var + 1e-5) * lng_ref[...] + lnb_ref[...]
        zbuf[r0:r0 + ROW_CHUNK, :] = (yn * _sigmoid(yn)).astype(BF16)
    hbuf[0:CONV_HALO, :] = hbuf[tm:tm + CONV_HALO, :]
    ya_ref[0] = (_dot(zbuf[...], pw_ref[...]) + pwb_ref[...]).astype(BF16)

    pb = _dot(hb, win_ref[:, _B0:_B1])
    ct = ct_ref[...]
    st = st_ref[...]
    cq = pb[:, 0:256]
    cqn = cq * lax.rsqrt(jnp.sum(cq * cq, axis=-1, keepdims=True) * (1.0 / MLA_Q_RANK) + 1e-6)
    cqb = (cqn * qn_ref[...]).astype(BF16)
    q = _dot(cqb, wuq_ref[...])
    qs = _dot(cqb, wuqs_ref[...])
    ct4 = jnp.concatenate([ct] * MLA_HEADS, axis=1)
    st4 = jnp.concatenate([st] * MLA_HEADS, axis=1)
    q_scale = LOG2_E * (MLA_NOPE + MLA_ROPE) ** -0.5
    qm = ((q * ct4 + qs * st4) * q_scale).astype(BF16)
    qm_ref[0] = qm
    ckv = pb[:, 256:384]
    ckvn = ckv * lax.rsqrt(jnp.mean(ckv * ckv, axis=-1, keepdims=True) + 1e-6)
    kv = _dot((ckvn * kvn_ref[...]).astype(BF16), wukv_ref[...])
    kr = pb[:, 384:512] * ct + pb[:, 512:640] * st
    km = (kv[:, 0:512] + jnp.concatenate([kr] * MLA_HEADS, axis=1)).astype(BF16)
    km_ref[0] = km
    vm_ref[0] = kv[:, 512:768].astype(BF16)

    pc = _dot(hb, win_ref[:, _C0:_C1])
    qd = (pc[:, 0:256] * (LOG2_E * DIFF_HALF ** -0.5)).astype(BF16)
    kd = pc[:, 256:512].astype(BF16)
    qd_ref[0] = qd
    kd_ref[0] = kd
    stats_ref[0, 0] = jnp.concatenate(
        [max_sq_norm(qm, gm_ref), max_sq_norm(km, gm_ref),
         max_sq_norm(qd, gd_ref), max_sq_norm(kd, gd_ref),
         jnp.zeros((SUBLANES - 4, LANES), F32)], axis=0)
    vd_ref[0] = pc[:, 512:768].astype(BF16)

    ubuf[POOL_HALO:POOL_HALO + tm, :] = _dot(hb, win_ref[:, _D0:_D1])
    lane = lax.broadcasted_iota(jnp.int32, (ROW_CHUNK, 256), 1)
    grp = lane // POOL_CH
    win_len = jnp.where(grp == 0, POOL_WINDOWS[0],
                        jnp.where(grp == 1, POOL_WINDOWS[1],
                                  jnp.where(grp == 2, POOL_WINDOWS[2], POOL_WINDOWS[3])))
    for c in range(tm // ROW_CHUNK):
        r0 = c * ROW_CHUNK
        u = ubuf[POOL_HALO + r0:POOL_HALO + r0 + ROW_CHUNK, :]
        run = u
        sums = {}
        for j in range(1, POOL_WINDOWS[-1]):
            run = run + ubuf[POOL_HALO + r0 - j:POOL_HALO + r0 - j + ROW_CHUNK, :]
            if j + 1 in POOL_WINDOWS:
                sums[j + 1] = run
        win = jnp.where(grp == 0, sums[2],
                        jnp.where(grp == 1, sums[4], jnp.where(grp == 2, sums[8], sums[16])))
        t_glob = s * tm + r0 + lax.broadcasted_iota(jnp.int32, (ROW_CHUNK, 256), 0)
        cnt = jnp.minimum(t_glob + 1, win_len).astype(F32)
        dbuf[r0:r0 + ROW_CHUNK, :] = (win / cnt - u).astype(BF16)
    ubuf[0:POOL_HALO, :] = ubuf[tm:tm + POOL_HALO, :]
    yd_ref[0] = (_dot(dbuf[...], poolw_ref[...]) * pools_ref[...]).astype(BF16)


def _mix_in(x, g, w_in, dw, dwb, lng, lnb, pw, pwb, qn, wuq, wuqs, kvn, wukv, ctab, stab,
            poolw, pools, group_mla, group_diff):
    b, s, d = x.shape
    tm = min(TOKEN_TILE, s)
    tok = lambda w: pl.BlockSpec((1, tm, w), lambda bi, si: (bi, si, 0))
    consts = [g, w_in, dw, dwb, lng, lnb, pw, pwb, qn, wuq, wuqs, kvn, wukv]
    tail = [poolw, pools, group_mla, group_diff]
    in_specs = ([tok(d)] + [_const_spec(a.shape) for a in consts]
                + [pl.BlockSpec((tm, LANES), lambda bi, si: (si, 0))] * 2
                + [_const_spec(a.shape) for a in tail])
    widths = (256, 512, 512, 256, 256, 256, 256, 256)
    stats_spec = pl.BlockSpec((1, 1, SUBLANES, LANES), lambda bi, si: (bi, si, 0, 0))
    return pl.pallas_call(
        functools.partial(_mix_in_kernel, tm=tm),
        grid=(b, s // tm),
        in_specs=in_specs,
        out_specs=[tok(w) for w in widths] + [stats_spec],
        out_shape=([jax.ShapeDtypeStruct((b, s, w), BF16) for w in widths]
                   + [jax.ShapeDtypeStruct((b, s // tm, SUBLANES, LANES), F32)]),
        scratch_shapes=[pltpu.VMEM((CONV_HALO + tm, 256), F32),
                        pltpu.VMEM((tm, 256), BF16),
                        pltpu.VMEM((POOL_HALO + tm, 256), F32),
                        pltpu.VMEM((tm, 256), BF16)],
        compiler_params=pltpu.CompilerParams(
            dimension_semantics=("arbitrary", "arbitrary"),
            vmem_limit_bytes=VMEM_LIMIT_BYTES),
        name="mix_in",
    )(x, *consts, ctab, stab, *tail)


def _head_lanes(vals, rows):
    lane = lax.broadcasted_iota(jnp.int32, (rows, LANES), 1)
    return jnp.concatenate([jnp.where(lane < 64, vals[0], vals[1]),
                            jnp.where(lane < 64, vals[2], vals[3])], axis=1)


def _row_total(l_ref, idx):
    return jnp.sum(l_ref[idx], axis=-1, keepdims=True)


def _stack_masked_v(v_blk):
    n = v_blk.shape[0]
    lane = lax.broadcasted_iota(jnp.int32, (n, 256), 1)
    parts = []
    for h in range(4):
        keep = (lane >= 64 * h) & (lane < 64 * (h + 1))
        parts.append(jnp.where(keep, v_blk, jnp.zeros_like(v_blk)))
    return jnp.concatenate(parts, axis=0)


def _softmax_step(sc, m_ref, l_ref, idx, online):
    if online:
        m_old = m_ref[idx]
        m_new = jnp.maximum(m_old, jnp.max(sc, axis=-1, keepdims=True))
        alpha = jnp.exp2(m_old - m_new)
        m_ref[idx] = m_new
    else:
        m_new = m_ref[idx]
        alpha = None
    reps = sc.shape[1] // LANES
    p = jnp.exp2(sc - jnp.concatenate([m_new] * reps, axis=1))
    psum = p[:, 0:LANES]
    for r in range(1, reps):
        psum = psum + p[:, r * LANES:(r + 1) * LANES]
    l_ref[idx] = (alpha * l_ref[idx] if online else l_ref[idx]) + psum
    return p, alpha


def _logit_bounds(q, group_ref, kmax_sq):
    qf = q.astype(F32)
    qn_sq = _dot((qf * qf).astype(BF16), group_ref[...])
    return NORM_MARGIN * jnp.sqrt(qn_sq * kmax_sq)


def _mla_attn_kernel(safe_ref, q_ref, k_ref, v_ref, stats_ref, group_ref, o_ref,
                     m_ref, l_ref, acc_ref, *, t, far):
    qi = pl.program_id(1)
    l_ref[...] = jnp.zeros(l_ref.shape, F32)
    acc_ref[...] = jnp.zeros(acc_ref.shape, F32)

    def run(online):
        if online:
            m_ref[...] = jnp.full(m_ref.shape, NEG_INF, F32)
        else:
            bounds = _logit_bounds(q_ref[0], group_ref, stats_ref[0, 1:2, :])
            for h in range(MLA_HEADS):
                m_ref[h] = jnp.broadcast_to(bounds[:, h:h + 1], (t, LANES))

        def step(k0, n, diagonal):
            vst = _stack_masked_v(v_ref[0, pl.ds(k0, n), :])
            probs, alphas = [], []
            for h in range(MLA_HEADS):
                cols = slice(LANES * h, LANES * (h + 1))
                sc = _dot_nt(q_ref[0, :, cols], k_ref[0, pl.ds(k0, n), cols])
                if diagonal:
                    row = lax.broadcasted_iota(jnp.int32, (t, n), 0)
                    col = lax.broadcasted_iota(jnp.int32, (t, n), 1)
                    sc = jnp.where(row >= col, sc, NEG_INF)
                p, alpha = _softmax_step(sc, m_ref, l_ref, h, online)
                probs.append(p.astype(BF16))
                alphas.append(alpha)
            pv = _dot(jnp.concatenate(probs, axis=1), vst)
            acc_ref[...] = (acc_ref[...] * _head_lanes(alphas, t) if online else acc_ref[...]) + pv

        def far_body(j, carry):
            step(pl.multiple_of(j * far, far), far, False)
            return carry

        def near_body(j, carry):
            step(pl.multiple_of(j * t, t), t, False)
            return carry

        per_far = far // t
        n_far = qi // per_far
        lax.fori_loop(0, n_far, far_body, 0)
        lax.fori_loop(n_far * per_far, qi, near_body, 0)
        step(pl.multiple_of(qi * t, t), t, True)

    safe = safe_ref[0] != 0
    pl.when(safe)(functools.partial(run, False))
    pl.when(safe_ref[0] == 0)(functools.partial(run, True))
    inv_l = _head_lanes([1.0 / _row_total(l_ref, h) for h in range(MLA_HEADS)], t)
    o_ref[0] = (acc_ref[...] * inv_l).astype(BF16)


def _mla_attn(safe, q, k, v, stats, group):
    b, s, _ = q.shape
    t = min(ATTN_TILE, s)
    far = min(MLA_FAR_KEYS, s)
    return pl.pallas_call(
        functools.partial(_mla_attn_kernel, t=t, far=far),
        grid=(b, s // t),
        in_specs=[pl.BlockSpec(memory_space=pltpu.SMEM),
                  pl.BlockSpec((1, t, 512), lambda bi, qi: (bi, qi, 0)),
                  pl.BlockSpec((1, s, 512), lambda bi, qi: (bi, 0, 0)),
                  pl.BlockSpec((1, s, 256), lambda bi, qi: (bi, 0, 0)),
                  pl.BlockSpec((1, SUBLANES, LANES), lambda bi, qi: (bi, 0, 0)),
                  _const_spec(group.shape)],
        out_specs=pl.BlockSpec((1, t, 256), lambda bi, qi: (bi, qi, 0)),
        out_shape=jax.ShapeDtypeStruct((b, s, 256), BF16),
        scratch_shapes=[pltpu.VMEM((MLA_HEADS, t, LANES), F32),
                        pltpu.VMEM((MLA_HEADS, t, LANES), F32),
                        pltpu.VMEM((t, 256), F32)],
        compiler_params=pltpu.CompilerParams(
            dimension_semantics=("arbitrary", "arbitrary"),
            vmem_limit_bytes=VMEM_LIMIT_BYTES),
        name="mla_attn",
    )(safe, q, k, v, stats, group)


def _bias_tiles_kernel(rb_ref, out_ref, *, t):
    row = lax.broadcasted_iota(jnp.int32, (t, t), 0)
    col = lax.broadcasted_iota(jnp.int32, (t, t), 1)
    max_exact = REL_BUCKETS // 2
    for j in range(2):
        rel = (1 - j) * t + row - col
        n = jnp.maximum(rel, 0)
        nf = jnp.maximum(n, 1).astype(F32)
        large = max_exact + (jnp.log(nf / max_exact) / math.log(REL_MAX_DIST / max_exact)
                             * (REL_BUCKETS - max_exact)).astype(jnp.int32)
        large = jnp.minimum(large, REL_BUCKETS - 1)
        bucket = jnp.where(n < max_exact, n, large)
        for h in range(DIFF_HEADS):
            far = rb_ref[REL_BUCKETS - 1, h]
            val = jnp.zeros((t, t), F32)
            for bkt in range(REL_BUCKETS - 1):
                val = jnp.where(bucket == bkt, (rb_ref[bkt, h] - far) * LOG2_E, val)
            out_ref[h, j] = jnp.where(rel >= 0, val, NEG_INF)


def _bias_tiles(rel_bias, t):
    return pl.pallas_call(
        functools.partial(_bias_tiles_kernel, t=t),
        in_specs=[pl.BlockSpec(memory_space=pltpu.SMEM)],
        out_specs=pl.BlockSpec(memory_space=pltpu.VMEM),
        out_shape=jax.ShapeDtypeStruct((DIFF_HEADS, 2, t, t), F32),
        name="bias_tiles",
    )(rel_bias)


def _diff_attn_kernel(safe_ref, q_ref, k_ref, v_ref, stats_ref, group_ref, bias_hi_ref, bias_ref,
                      lq1_ref, lk1_ref, lq2_ref, lk2_ref, subln_ref, o_ref,
                      qst_ref, m_ref, l_ref, acc_ref, s0_ref, s1_ref, *, t, lambda_init):
    qi = pl.program_id(1)
    nmaps = 2 * DIFF_HEADS
    l_ref[...] = jnp.zeros(l_ref.shape, F32)
    acc_ref[...] = jnp.zeros(acc_ref.shape, F32)
    q = q_ref[0]
    lane = lax.broadcasted_iota(jnp.int32, (t, 256), 1)
    for idx in range(nmaps):
        keep = (lane >= DIFF_HALF * idx) & (lane < DIFF_HALF * (idx + 1))
        qst_ref[idx * t:(idx + 1) * t, :] = jnp.where(keep, q, jnp.zeros_like(q))

    def scores(kb):
        return _dot_nt(qst_ref[...], k_ref[0, pl.ds(pl.multiple_of(kb * t, t), t), :])

    def consume(sc_of, kb, bias_tile, online):
        vst = _stack_masked_v(v_ref[0, pl.ds(pl.multiple_of(kb * t, t), t), :])
        probs = [[], []]
        alphas = [[], []]
        for h in range(DIFF_HEADS):
            for which in range(2):
                idx = 2 * h + which
                sc = sc_of(idx)
                if bias_tile is not None:
                    sc = sc + bias_ref[h, bias_tile]
                p, alpha = _softmax_step(sc, m_ref, l_ref, idx, online)
                probs[which].append(p.astype(BF16))
                alphas[which].append(alpha)
        p_all = jnp.concatenate([jnp.concatenate(probs[0], axis=1),
                                 jnp.concatenate(probs[1], axis=1)], axis=0)
        pv = _dot(p_all, vst)
        for which in range(2):
            rows = slice(which * t, (which + 1) * t)
            old = acc_ref[rows, :] * _head_lanes(alphas[which], t) if online else acc_ref[rows, :]
            acc_ref[rows, :] = old + pv[rows, :]

    def from_ref(ref):
        return lambda idx: ref[idx * t:(idx + 1) * t, :]

    n_plain = jnp.maximum(qi - 1, 0)

    def run_online():
        m_ref[...] = jnp.full(m_ref.shape, NEG_INF, F32)

        def step(kb, bias_tile):
            sc_all = scores(kb)
            consume(lambda idx: sc_all[idx * t:(idx + 1) * t, :], kb, bias_tile, True)

        def body(kb, carry):
            step(kb, None)
            return carry

        lax.fori_loop(0, n_plain, body, 0)
        pl.when(qi > 0)(lambda: step(qi - 1, 0))
        step(qi, 1)

    def run_fixed():
        bounds = _logit_bounds(q, group_ref, stats_ref[0, 3:4, :]) + bias_hi_ref[...]
        for idx in range(nmaps):
            m_ref[idx] = jnp.broadcast_to(bounds[:, idx:idx + 1], (t, LANES))

        def pair_body(i, carry):
            s1_ref[...] = scores(2 * i + 1)
            consume(from_ref(s0_ref), 2 * i, None, False)
            s0_ref[...] = scores(2 * i + 2)
            consume(from_ref(s1_ref), 2 * i + 1, None, False)
            return carry

        s0_ref[...] = scores(0)
        n_pairs = n_plain // 2
        lax.fori_loop(0, n_pairs, pair_body, 0)
        b0 = 2 * n_pairs
        left = qi - b0

        @pl.when(left == 0)
        def _():
            consume(from_ref(s0_ref), 0, 1, False)

        @pl.when(left == 1)
        def _():
            s1_ref[...] = scores(b0 + 1)
            consume(from_ref(s0_ref), b0, 0, False)
            consume(from_ref(s1_ref), b0 + 1, 1, False)

        @pl.when(left == 2)
        def _():
            s1_ref[...] = scores(b0 + 1)
            consume(from_ref(s0_ref), b0, None, False)
            sc_last = scores(b0 + 2)
            consume(from_ref(s1_ref), b0 + 1, 0, False)
            consume(lambda idx: sc_last[idx * t:(idx + 1) * t, :], b0 + 2, 1, False)

    safe = safe_ref[0] != 0
    pl.when(safe)(run_fixed)
    pl.when(safe_ref[0] == 0)(run_online)

    lam = (jnp.exp(jnp.sum(lq1_ref[...] * lk1_ref[...], axis=-1, keepdims=True))
           - jnp.exp(jnp.sum(lq2_ref[...] * lk2_ref[...], axis=-1, keepdims=True))
           + lambda_init)
    o1 = acc_ref[0:t, :] * _head_lanes([1.0 / _row_total(l_ref, 2 * h) for h in range(DIFF_HEADS)], t)
    o2 = acc_ref[t:2 * t, :] * _head_lanes(
        [1.0 / _row_total(l_ref, 2 * h + 1) for h in range(DIFF_HEADS)], t)
    o = o1 - lam * o2
    osq = o * o
    ms = []
    for h in range(DIFF_HEADS):
        keep = (lane >= DIFF_V * h) & (lane < DIFF_V * (h + 1))
        ms.append(jnp.sum(jnp.where(keep, osq, 0.0), axis=-1, keepdims=True) * (1.0 / DIFF_V))
    on = o * lax.rsqrt(_head_lanes(ms, t) + 1e-5) * subln_ref[...]
    o_ref[0] = (on * (1.0 - lambda_init)).astype(BF16)


def _diff_attn(safe, q, k, v, stats, group, bias_hi, bias_tiles, lq1, lk1, lq2, lk2, subln,
               lambda_init):
    b, s, _ = q.shape
    t = min(ATTN_TILE, s)
    return pl.pallas_call(
        functools.partial(_diff_attn_kernel, t=t, lambda_init=lambda_init),
        grid=(b, s // t),
        in_specs=[pl.BlockSpec(memory_space=pltpu.SMEM),
                  pl.BlockSpec((1, t, 256), lambda bi, qi: (bi, qi, 0)),
                  pl.BlockSpec((1, s, 256), lambda bi, qi: (bi, 0, 0)),
                  pl.BlockSpec((1, s, 256), lambda bi, qi: (bi, 0, 0)),
                  pl.BlockSpec((1, SUBLANES, LANES), lambda bi, qi: (bi, 0, 0)),
                  _const_spec(group.shape), _const_spec(bias_hi.shape),
                  _const_spec(bias_tiles.shape),
                  _const_spec(lq1.shape), _const_spec(lk1.shape),
                  _const_spec(lq2.shape), _const_spec(lk2.shape),
                  _const_spec(subln.shape)],
        out_specs=pl.BlockSpec((1, t, 256), lambda bi, qi: (bi, qi, 0)),
        out_shape=jax.ShapeDtypeStruct((b, s, 256), BF16),
        scratch_shapes=[pltpu.VMEM((8 * t, 256), BF16),
                        pltpu.VMEM((2 * DIFF_HEADS, t, LANES), F32),
                        pltpu.VMEM((2 * DIFF_HEADS, t, LANES), F32),
                        pltpu.VMEM((2 * t, 256), F32),
                        pltpu.VMEM((8 * t, t), F32),
                        pltpu.VMEM((8 * t, t), F32)],
        compiler_params=pltpu.CompilerParams(
            dimension_semantics=("arbitrary", "arbitrary"),
            vmem_limit_bytes=VMEM_LIMIT_BYTES),
        name="diff_attn",
    )(safe, q, k, v, stats, group, bias_hi, bias_tiles, lq1, lk1, lq2, lk2, subln)


def _mem_kv_kernel(mem_ref, g_ref, wk_ref, wv_ref, k_ref, v_ref):
    m = mem_ref[0]
    mn = m * lax.rsqrt(jnp.mean(m * m, axis=-1, keepdims=True) + 1e-6) * g_ref[0]
    mb = mn.astype(BF16)
    k_ref[0, 0] = _dot(mb, wk_ref[0]).astype(BF16)
    v_ref[0, 0] = _dot(mb, wv_ref[0]).astype(BF16)


def _mem_kv(mem, g, wk, wv):
    b, m, d = mem.shape
    nl = wk.shape[0]
    out = jax.ShapeDtypeStruct((nl, b, m, d), BF16)
    return pl.pallas_call(
        _mem_kv_kernel,
        grid=(nl, b),
        in_specs=[pl.BlockSpec((1, m, d), lambda li, bi: (bi, 0, 0)),
                  pl.BlockSpec((1, 1, d), lambda li, bi: (li, 0, 0)),
                  pl.BlockSpec((1, d, d), lambda li, bi: (li, 0, 0)),
                  pl.BlockSpec((1, d, d), lambda li, bi: (li, 0, 0))],
        out_specs=[pl.BlockSpec((1, 1, m, d), lambda li, bi: (li, bi, 0, 0))] * 2,
        out_shape=[out, out],
        compiler_params=pltpu.CompilerParams(
            dimension_semantics=("arbitrary", "arbitrary"),
            vmem_limit_bytes=VMEM_LIMIT_BYTES),
        name="mem_kv",
    )(mem, g, wk, wv)


def _xa_kernel(x_ref, ya_ref, yb_ref, yc_ref, yd_ref, wout_ref, g_ref, wq_ref, km_ref, vm_ref,
               wo_ref, o_ref, obuf):
    x1 = x_ref[0]
    for i, y_ref in enumerate((ya_ref, yb_ref, yc_ref, yd_ref)):
        x1 = x1 + _dot(y_ref[0], wout_ref[256 * i:256 * (i + 1), :])
    hx = x1 * lax.rsqrt(jnp.mean(x1 * x1, axis=-1, keepdims=True) + 1e-6) * g_ref[...]
    hd = wq_ref.shape[1] // XA_HEADS
    q = (_dot(hx.astype(BF16), wq_ref[...]) * (hd ** -0.5)).astype(BF16)
    for h in range(XA_HEADS):
        cols = slice(hd * h, hd * (h + 1))
        sc = _dot_nt(q[:, cols], km_ref[0, :, cols])
        p = jnp.exp(sc - jnp.max(sc, axis=-1, keepdims=True))
        oh = _dot(p.astype(BF16), vm_ref[0, :, cols]) / jnp.sum(p, axis=-1, keepdims=True)
        obuf[:, cols] = oh.astype(BF16)
    o_ref[0] = x1 + _dot(obuf[...], wo_ref[...])


def _xa(x, ya, yb, yc, yd, wout, g, wq, kmem, vmem, wo):
    b, s, d = x.shape
    tm = min(TOKEN_TILE, s)
    m = kmem.shape[1]
    tok = lambda w: pl.BlockSpec((1, tm, w), lambda bi, si: (bi, si, 0))
    mem_spec = pl.BlockSpec((1, m, d), lambda bi, si: (bi, 0, 0))
    return pl.pallas_call(
        _xa_kernel,
        grid=(b, s // tm),
        in_specs=[tok(d), tok(256), tok(256), tok(256), tok(256),
                  _const_spec(wout.shape), _const_spec(g.shape), _const_spec(wq.shape),
                  mem_spec, mem_spec, _const_spec(wo.shape)],
        out_specs=tok(d),
        out_shape=jax.ShapeDtypeStruct((b, s, d), F32),
        scratch_shapes=[pltpu.VMEM((tm, d), BF16)],
        compiler_params=pltpu.CompilerParams(
            dimension_semantics=("arbitrary", "arbitrary"),
            vmem_limit_bytes=VMEM_LIMIT_BYTES),
        name="xa",
    )(x, ya, yb, yc, yd, wout, g, wq, kmem, vmem, wo)


def _ffn_kernel(x_ref, g_ref, wup_ref, dw_ref, dwb_ref, wdown_ref, gfin_ref, o_ref,
                cbuf, carry, acc_ref, *, tm, dff, final_norm):
    s = pl.program_id(1)

    @pl.when(s == 0)
    def _():
        carry[...] = jnp.zeros(carry.shape, F32)

    x = x_ref[0]
    h = x * lax.rsqrt(jnp.mean(x * x, axis=-1, keepdims=True) + 1e-6) * g_ref[...]
    hb = h.astype(BF16)
    nchunk = dff // FFN_CHUNK

    def conv(u, slot, col0):
        cbuf[slot, 0:FFN_HALO, :] = carry[slot, :, col0:col0 + FFN_CHUNK]
        cbuf[slot, FFN_HALO:FFN_HALO + tm, :] = u
        carry[slot, :, col0:col0 + FFN_CHUNK] = u[tm - FFN_HALO:tm, :]
        w = dw_ref[:, slot * dff + col0:slot * dff + col0 + FFN_CHUNK]
        out = dwb_ref[:, slot * dff + col0:slot * dff + col0 + FFN_CHUNK] + w[2:3, :] * u
        out = out + w[1:2, :] * cbuf[slot, FFN_HALO - 1:FFN_HALO - 1 + tm, :]
        return out + w[0:1, :] * cbuf[slot, FFN_HALO - 2:FFN_HALO - 2 + tm, :]

    for j in range(nchunk):
        col0 = j * FFN_CHUNK
        a = conv(_dot(hb, wup_ref[:, col0:col0 + FFN_CHUNK]), 0, col0)
        gt = conv(_dot(hb, wup_ref[:, dff + col0:dff + col0 + FFN_CHUNK]), 1, col0)
        act = (gt * _sigmoid(gt) * a).astype(BF16)
        part = _dot(act, wdown_ref[col0:col0 + FFN_CHUNK, :])
        if j == 0:
            acc_ref[...] = x + part
        else:
            acc_ref[...] += part
    y = acc_ref[...]
    if final_norm:
        y = y * lax.rsqrt(jnp.mean(y * y, axis=-1, keepdims=True) + 1e-6) * gfin_ref[...]
    o_ref[0] = y


def _ffn(x, g, wup, dw, dwb, wdown, gfin, final_norm):
    b, s, d = x.shape
    tm = min(TOKEN_TILE, s)
    dff = wdown.shape[0]
    tok = pl.BlockSpec((1, tm, d), lambda bi, si: (bi, si, 0))
    return pl.pallas_call(
        functools.partial(_ffn_kernel, tm=tm, dff=dff, final_norm=final_norm),
        grid=(b, s // tm),
        in_specs=[tok, _const_spec(g.shape), _const_spec(wup.shape), _const_spec(dw.shape),
                  _const_spec(dwb.shape), _const_spec(wdown.shape), _const_spec(gfin.shape)],
        out_specs=tok,
        out_shape=jax.ShapeDtypeStruct((b, s, d), F32),
        scratch_shapes=[pltpu.VMEM((2, FFN_HALO + tm, FFN_CHUNK), F32),
                        pltpu.VMEM((2, FFN_HALO, dff), F32),
                        pltpu.VMEM((tm, d), F32)],
        compiler_params=pltpu.CompilerParams(
            dimension_semantics=("arbitrary", "arbitrary"),
            vmem_limit_bytes=VMEM_LIMIT_BYTES),
        name="ffn",
    )(x, g, wup, dw, dwb, wdown, gfin)


def _pack_w_in(w):
    d = w.shape[0]
    ua, cq, ckv, kr, qc, kc, vc, ud = jnp.split(
        w, [512, 704, 832, 864, 1120, 1376, 1632], axis=1)
    z = lambda n: jnp.zeros((d, n), w.dtype)
    half = MLA_ROPE // 2
    kr_sw = jnp.concatenate([kr[:, half:], kr[:, :half]], axis=1)
    kr_blk = jnp.concatenate([z(MLA_NOPE), kr, z(LANES - MLA_NOPE - MLA_ROPE)], axis=1)
    kr_sw_blk = jnp.concatenate([z(MLA_NOPE), kr_sw, z(LANES - MLA_NOPE - MLA_ROPE)], axis=1)
    packed = jnp.concatenate(
        [ua, cq, z(256 - MLA_Q_RANK), ckv, kr_blk, kr_sw_blk, z(LANES), qc, kc, vc, ud], axis=1)
    return packed.astype(BF16)


def _pack_w_uq(w):
    w = w.reshape(MLA_Q_RANK, MLA_HEADS, MLA_NOPE + MLA_ROPE)
    nope, rp = w[..., :MLA_NOPE], w[..., MLA_NOPE:]
    half = MLA_ROPE // 2
    rp_sw = jnp.concatenate([rp[..., half:], rp[..., :half]], axis=-1)
    zn = jnp.zeros_like(nope)
    zp = jnp.zeros((MLA_Q_RANK, MLA_HEADS, LANES - MLA_NOPE - MLA_ROPE), w.dtype)

    def fin(a):
        a = a.reshape(MLA_Q_RANK, MLA_HEADS * LANES)
        return jnp.pad(a, ((0, 256 - MLA_Q_RANK), (0, 0))).astype(BF16)

    return (fin(jnp.concatenate([nope, rp, zp], axis=-1)),
            fin(jnp.concatenate([zn, rp_sw, zp], axis=-1)))


def _pack_w_ukv(w):
    w = w.reshape(MLA_KV_RANK, MLA_HEADS, MLA_NOPE + MLA_V)
    kn, v = w[..., :MLA_NOPE], w[..., MLA_NOPE:]
    kn = jnp.concatenate([kn, jnp.zeros_like(kn)], axis=-1).reshape(MLA_KV_RANK, MLA_HEADS * LANES)
    return jnp.concatenate([kn, v.reshape(MLA_KV_RANK, MLA_HEADS * MLA_V)], axis=1).astype(BF16)


def _rope_tables(positions):
    inv_freq = ROPE_BASE ** (-jnp.arange(0, MLA_ROPE, 2, dtype=F32) / MLA_ROPE)
    ang = positions.astype(F32)[:, None] * inv_freq[None, :]
    cos, sin = jnp.cos(ang), jnp.sin(ang)
    s = positions.shape[0]
    pad = jnp.zeros((s, LANES - MLA_NOPE - MLA_ROPE), F32)
    ctab = jnp.concatenate([jnp.ones((s, MLA_NOPE), F32), cos, cos, pad], axis=1)
    stab = jnp.concatenate([jnp.zeros((s, MLA_NOPE), F32), -sin, sin, pad], axis=1)
    return ctab, stab


def _lane_groups(n_lanes, width):
    lane = jnp.arange(n_lanes)[:, None] // width
    return (lane == jnp.arange(LANES)[None, :]).astype(BF16)


def _bias_range(rel_bias):
    shifted = (rel_bias - rel_bias[REL_BUCKETS - 1]) * LOG2_E
    hi = jnp.repeat(jnp.max(shifted, axis=0), 2)
    lo = jnp.repeat(jnp.min(shifted, axis=0), 2)
    pad = (0, LANES - 2 * DIFF_HEADS)
    return jnp.pad(hi, pad).reshape(1, LANES), jnp.pad(hi - lo, pad).reshape(1, LANES)


def _fixed_stabiliser_ok(qmax_sq, kmax_sq, extra_span):
    bound = NORM_MARGIN * jnp.sqrt(qmax_sq * kmax_sq)
    ok = jnp.all(2.0 * bound + extra_span <= SAFE_LOGIT_SPAN)
    return ok.astype(jnp.int32).reshape(1)


def _block_diag(w):
    g, c, _ = w.shape
    out = jnp.zeros((g * c, g * c), w.dtype)
    for i in range(g):
        out = out.at[i * c:(i + 1) * c, i * c:(i + 1) * c].set(w[i])
    return out


def kernel(x, mem, positions, rel_bias, norm_mix, w_in, w_out, conv_dw, conv_dw_b, conv_ln_g, conv_ln_b, conv_pw, conv_pw_b, mla_q_norm, mla_w_uq, mla_kv_norm, mla_w_ukv, diff_lq1, diff_lk1, diff_lq2, diff_lk2, diff_subln, pool_w, pool_scale, norm_xa, mem_norm, xa_wq, xa_wk, xa_wv, xa_wo, norm_ffn, ffn_up, ffn_dw, ffn_dw_b, ffn_down, norm_final):
    depth = w_in.shape[0]
    s = x.shape[1]
    row = lambda a: a.reshape(1, -1)
    ctab, stab = _rope_tables(positions)
    bias_tiles = _bias_tiles(rel_bias, min(ATTN_TILE, s))
    kmem, vmem = _mem_kv(mem, mem_norm[:, None, :], xa_wk.astype(BF16), xa_wv.astype(BF16))
    group_mla = _lane_groups(MLA_HEADS * LANES, LANES)
    group_diff = _lane_groups(2 * DIFF_HEADS * DIFF_HALF, DIFF_HALF)
    bias_hi, bias_span = _bias_range(rel_bias)
    for l in range(depth):
        wuq, wuqs = _pack_w_uq(mla_w_uq[l])
        qn = jnp.pad(mla_q_norm[l], (0, 256 - MLA_Q_RANK))
        ya, qm, km, vm, qd, kd, vd, yd, stats = _mix_in(
            x, row(norm_mix[l]), _pack_w_in(w_in[l]), conv_dw[l], row(conv_dw_b[l]),
            row(conv_ln_g[l]), row(conv_ln_b[l]), conv_pw[l].astype(BF16), row(conv_pw_b[l]),
            row(qn), wuq, wuqs, row(mla_kv_norm[l]), _pack_w_ukv(mla_w_ukv[l]), ctab, stab,
            _block_diag(pool_w[l]).astype(BF16), row(pool_scale[l]), group_mla, group_diff)
        stats = jnp.max(stats, axis=1)
        yb = _mla_attn(_fixed_stabiliser_ok(stats[:, 0], stats[:, 1], 0.0), qm, km, vm, stats,
                       group_mla)
        lambda_init = 0.8 - 0.6 * math.exp(-0.3 * l)
        yc = _diff_attn(_fixed_stabiliser_ok(stats[:, 2], stats[:, 3], bias_span), qd, kd, vd,
                        stats, group_diff, bias_hi, bias_tiles, row(diff_lq1[l]),
                        row(diff_lk1[l]), row(diff_lq2[l]), row(diff_lk2[l]),
                        row(jnp.tile(diff_subln[l], DIFF_HEADS)), lambda_init)
        x = _xa(x, ya, yb, yc, yd, w_out[l].astype(BF16), row(norm_xa[l]),
                xa_wq[l].astype(BF16), kmem[l], vmem[l], xa_wo[l].astype(BF16))
        x = _ffn(x, row(norm_ffn[l]), ffn_up[l].astype(BF16), ffn_dw[l], row(ffn_dw_b[l]),
                 ffn_down[l].astype(BF16), row(norm_final), l == depth - 1)
    return x
```

```python
import functools
import math

import jax
import jax.numpy as jnp
from jax import lax
from jax.experimental import pallas as pl
from jax.experimental.pallas import tpu as pltpu

N_GROUPS = 4
CONV_KERNEL = 31
MLA_HEADS = 4
MLA_NOPE = 64
MLA_ROPE = 32
MLA_V = 64
MLA_Q_RANK = 192
MLA_KV_RANK = 128
DIFF_HEADS = 4
DIFF_HALF = 32
DIFF_V = 64
POOL_WINDOWS = (2, 4, 8, 16)
POOL_CH = 64
REL_BUCKETS = 32
REL_MAX_DIST = 128
XA_HEADS = 4
FFN_CONV = 3
ROPE_BASE = 10000.0
NEG_INF = -1e30
LOG2_E = math.log2(math.e)
NORM_MARGIN = 1.02
SAFE_LOGIT_SPAN = 100.0

LANES = 128
SUBLANES = 8
VMEM_LIMIT_BYTES = 56 * 1024 * 1024

TOKEN_TILE = 512
ATTN_TILE = 256
MLA_FAR_KEYS = 1024
CONV_HALO = 32
POOL_HALO = 16
ROW_CHUNK = 64
FFN_CHUNK = 256

BF16 = jnp.bfloat16
F32 = jnp.float32


def _dot(a, b):
    return jnp.dot(a, b, preferred_element_type=F32)


def _dot_nt(a, b):
    return lax.dot_general(a, b, (((1,), (1,)), ((), ())), preferred_element_type=F32)


def _sigmoid(x):
    return 1.0 / (1.0 + jnp.exp(-x))


def _const_spec(shape):
    nd = len(shape)
    return pl.BlockSpec(shape, lambda *_: (0,) * nd)


def _layer_spec(stacked, layer):
    tail = stacked.shape[1:]
    return pl.BlockSpec((None,) + tail, lambda *_: (layer,) + (0,) * len(tail))


_A0, _A1 = 0, 512
_B0, _B1 = 512, 1280
_C0, _C1 = 1280, 2048
_D0, _D1 = 2048, 2304
W_IN_COLS = 2304


def _to_group_order(val, perm_ref, groups):
    ntile = val.shape[1] // LANES
    for c in range(ntile):
        perm_ref[c] = val[:, c * LANES:(c + 1) * LANES]
    return [jnp.concatenate([perm_ref[c, pl.ds(j, SUBLANES, stride=groups), :]
                             for c in range(ntile)], axis=1) for j in range(groups)]


def _to_row_order(val, perm_ref, groups):
    ntile = val.shape[1] // LANES
    for j in range(groups):
        for c in range(ntile):
            perm_ref[c, pl.ds(j, SUBLANES, stride=groups), :] = val[
                j * SUBLANES:(j + 1) * SUBLANES, c * LANES:(c + 1) * LANES]
    return jnp.concatenate([perm_ref[c] for c in range(ntile)], axis=1)


def _fill_window(grps, ext_ref, prev_ref, halo):
    groups = len(grps)
    first_sublane = lax.broadcasted_iota(jnp.int32, grps[0].shape, 0) == 0
    for m in range(groups - halo, groups):
        rows = slice((m - groups + halo) * SUBLANES, (m - groups + halo + 1) * SUBLANES)
        ext_ref[rows, :] = jnp.where(first_sublane, pltpu.roll(prev_ref[rows, :], 1, 0),
                                     pltpu.roll(grps[m], 1, 0))
        prev_ref[rows, :] = grps[m]
    for j in range(groups):
        ext_ref[(halo + j) * SUBLANES:(halo + j + 1) * SUBLANES, :] = grps[j]


def _mix_in_kernel(x_ref, g_ref, win_ref, dw_ref, dwb_ref, lng_ref, lnb_ref, pw_ref, pwb_ref,
                   qn_ref, wuq_ref, wuqs_ref, kvn_ref, wukv_ref, ct_ref, st_ref,
                   poolw_ref, pools_ref, gm_ref, gd_ref,
                   ya_ref, qm_ref, km_ref, vm_ref, qd_ref, kd_ref, vd_ref, yd_ref, stats_ref,
                   ext_a, prev_a, zbuf, ext_d, prev_d, dbuf, perm_a, perm_ya, perm_d, perm_yd, *, tm):
    s = pl.program_id(1)
    groups = tm // SUBLANES

    def max_sq_norm(v, group_ref):
        vf = v.astype(F32)
        return jnp.max(_dot((vf * vf).astype(BF16), group_ref[...]), axis=0, keepdims=True)

    @pl.when(s == 0)
    def _():
        prev_a[...] = jnp.zeros(prev_a.shape, F32)
        prev_d[...] = jnp.zeros(prev_d.shape, F32)

    x = x_ref[0]
    h = x * lax.rsqrt(jnp.mean(x * x, axis=-1, keepdims=True) + 1e-6) * g_ref[...]
    hb = h.astype(BF16)

    pa = _dot(hb, win_ref[:, _A0:_A1])
    pb = _dot(hb, win_ref[:, _B0:_B1])
    pc = _dot(hb, win_ref[:, _C0:_C1])
    ud = _dot(hb, win_ref[:, _D0:_D1])

    glu = pa[:, 0:256] * _sigmoid(pa[:, 256:512])
    _fill_window(_to_group_order(glu, perm_a, groups), ext_a, prev_a, CONV_HALO)
    for c in range(tm // ROW_CHUNK):
        r0 = c * ROW_CHUNK
        acc = jnp.broadcast_to(dwb_ref[...], (ROW_CHUNK, 256))
        for k in range(CONV_KERNEL):
            off = r0 + (CONV_HALO - (CONV_KERNEL - 1) + k) * SUBLANES
            acc = acc + dw_ref[k:k + 1, :] * ext_a[off:off + ROW_CHUNK, :]
        mu = jnp.mean(acc, axis=-1, keepdims=True)
        cen = acc - mu
        var = jnp.mean(cen * cen, axis=-1, keepdims=True)
        yn = cen * lax.rsqrt(var + 1e-5) * lng_ref[...] + lnb_ref[...]
        zbuf[r0:r0 + ROW_CHUNK, :] = (yn * _sigmoid(yn)).astype(BF16)
    ya = _to_row_order(_dot(zbuf[...], pw_ref[...]), perm_ya, groups)
    ya_ref[0] = (ya + pwb_ref[...]).astype(BF16)

    ct = ct_ref[...]
    st = st_ref[...]
    cq = pb[:, 0:256]
    cqn = cq * lax.rsqrt(jnp.sum(cq * cq, axis=-1, keepdims=True) * (1.0 / MLA_Q_RANK) + 1e-6)
    cqb = (cqn * qn_ref[...]).astype(BF16)
    q = _dot(cqb, wuq_ref[...])
    qs = _dot(cqb, wuqs_ref[...])
    ct4 = jnp.concatenate([ct] * MLA_HEADS, axis=1)
    st4 = jnp.concatenate([st] * MLA_HEADS, axis=1)
    q_scale = LOG2_E * (MLA_NOPE + MLA_ROPE) ** -0.5
    qm = ((q * ct4 + qs * st4) * q_scale).astype(BF16)
    qm_ref[0] = qm
    ckv = pb[:, 256:384]
    ckvn = ckv * lax.rsqrt(jnp.mean(ckv * ckv, axis=-1, keepdims=True) + 1e-6)
    kv = _dot((ckvn * kvn_ref[...]).astype(BF16), wukv_ref[...])
    kr = pb[:, 384:512] * ct + pb[:, 512:640] * st
    km = (kv[:, 0:512] + jnp.concatenate([kr] * MLA_HEADS, axis=1)).astype(BF16)
    km_ref[0] = km
    vm_ref[0] = kv[:, 512:768].astype(BF16)

    qd = (pc[:, 0:256] * (LOG2_E * DIFF_HALF ** -0.5)).astype(BF16)
    kd = pc[:, 256:512].astype(BF16)
    qd_ref[0] = qd
    kd_ref[0] = kd
    stats_ref[0, 0] = jnp.concatenate(
        [max_sq_norm(qm, gm_ref), max_sq_norm(km, gm_ref),
         max_sq_norm(qd, gd_ref), max_sq_norm(kd, gd_ref),
         jnp.zeros((SUBLANES - 4, LANES), F32)], axis=0)
    vd_ref[0] = pc[:, 512:768].astype(BF16)

    _fill_window(_to_group_order(ud, perm_d, groups), ext_d, prev_d, POOL_HALO)
    lane = lax.broadcasted_iota(jnp.int32, (ROW_CHUNK, 256), 1)
    grp = lane // POOL_CH
    win_len = jnp.where(grp == 0, POOL_WINDOWS[0],
                        jnp.where(grp == 1, POOL_WINDOWS[1],
                                  jnp.where(grp == 2, POOL_WINDOWS[2], POOL_WINDOWS[3])))
    for c in range(tm // ROW_CHUNK):
        r0 = c * ROW_CHUNK
        base = r0 + POOL_HALO * SUBLANES
        u = ext_d[base:base + ROW_CHUNK, :]
        run = u
        sums = {}
        for j in range(1, POOL_WINDOWS[-1]):
            run = run + ext_d[base - j * SUBLANES:base - j * SUBLANES + ROW_CHUNK, :]
            if j + 1 in POOL_WINDOWS:
                sums[j + 1] = run
        win = jnp.where(grp == 0, sums[2],
                        jnp.where(grp == 1, sums[4], jnp.where(grp == 2, sums[8], sums[16])))
        r = r0 + lax.broadcasted_iota(jnp.int32, (ROW_CHUNK, 256), 0)
        t_glob = s * tm + (r % SUBLANES) * groups + r // SUBLANES
        cnt = jnp.minimum(t_glob + 1, win_len).astype(F32)
        dbuf[r0:r0 + ROW_CHUNK, :] = (win / cnt - u).astype(BF16)
    yd = _to_row_order(_dot(dbuf[...], poolw_ref[...]), perm_yd, groups)
    yd_ref[0] = (yd * pools_ref[...]).astype(BF16)


def _mix_in(layer, x, per_layer, ctab, stab, group_mla, group_diff):
    b, s, d = x.shape
    tm = min(TOKEN_TILE, s)
    tok = lambda w: pl.BlockSpec((1, tm, w), lambda bi, si: (bi, si, 0))
    head, tail = per_layer[:13], per_layer[13:]
    in_specs = ([tok(d)] + [_layer_spec(a, layer) for a in head]
                + [pl.BlockSpec((tm, LANES), lambda bi, si: (si, 0))] * 2
                + [_layer_spec(a, layer) for a in tail]
                + [_const_spec(group_mla.shape), _const_spec(group_diff.shape)])
    widths = (256, 512, 512, 256, 256, 256, 256, 256)
    stats_spec = pl.BlockSpec((1, 1, SUBLANES, LANES), lambda bi, si: (bi, si, 0, 0))
    return pl.pallas_call(
        functools.partial(_mix_in_kernel, tm=tm),
        grid=(b, s // tm),
        in_specs=in_specs,
        out_specs=[tok(w) for w in widths] + [stats_spec],
        out_shape=([jax.ShapeDtypeStruct((b, s, w), BF16) for w in widths]
                   + [jax.ShapeDtypeStruct((b, s // tm, SUBLANES, LANES), F32)]),
        scratch_shapes=[pltpu.VMEM((CONV_HALO * SUBLANES + tm, 256), F32),
                        pltpu.VMEM((CONV_HALO * SUBLANES, 256), F32),
                        pltpu.VMEM((tm, 256), BF16),
                        pltpu.VMEM((POOL_HALO * SUBLANES + tm, 256), F32),
                        pltpu.VMEM((POOL_HALO * SUBLANES, 256), F32),
                        pltpu.VMEM((tm, 256), BF16)]
                       + [pltpu.VMEM((2, tm, LANES), F32)] * 4,
        compiler_params=pltpu.CompilerParams(
            dimension_semantics=("arbitrary", "arbitrary"),
            vmem_limit_bytes=VMEM_LIMIT_BYTES),
        name="mix_in",
    )(x, *head, ctab, stab, *tail, group_mla, group_diff)


def _head_lanes(vals, rows):
    lane = lax.broadcasted_iota(jnp.int32, (rows, LANES), 1)
    return jnp.concatenate([jnp.where(lane < 64, vals[0], vals[1]),
                            jnp.where(lane < 64, vals[2], vals[3])], axis=1)


def _row_total(l_ref, idx):
    return jnp.sum(l_ref[idx], axis=-1, keepdims=True)


def _stack_masked_v(v_blk):
    n = v_blk.shape[0]
    lane = lax.broadcasted_iota(jnp.int32, (n, 256), 1)
    parts = []
    for h in range(4):
        keep = (lane >= 64 * h) & (lane < 64 * (h + 1))
        parts.append(jnp.where(keep, v_blk, jnp.zeros_like(v_blk)))
    return jnp.concatenate(parts, axis=0)


def _softmax_step(sc, m_ref, l_ref, idx, online):
    if online:
        m_old = m_ref[idx]
        m_new = jnp.maximum(m_old, jnp.max(sc, axis=-1, keepdims=True))
        alpha = jnp.exp2(m_old - m_new)
        m_ref[idx] = m_new
    else:
        m_new = m_ref[idx]
        alpha = None
    reps = sc.shape[1] // LANES
    p = jnp.exp2(sc - jnp.concatenate([m_new] * reps, axis=1))
    psum = p[:, 0:LANES]
    for r in range(1, reps):
        psum = psum + p[:, r * LANES:(r + 1) * LANES]
    l_ref[idx] = (alpha * l_ref[idx] if online else l_ref[idx]) + psum
    return p, alpha


def _logit_bounds(q, group_ref, kmax_sq):
    qf = q.astype(F32)
    qn_sq = _dot((qf * qf).astype(BF16), group_ref[...])
    return NORM_MARGIN * jnp.sqrt(qn_sq * kmax_sq)


def _mla_attn_kernel(safe_ref, q_ref, k_ref, v_ref, stats_ref, group_ref, o_ref,
                     m_ref, l_ref, acc_ref, *, t, far):
    qi = pl.program_id(1)
    l_ref[...] = jnp.zeros(l_ref.shape, F32)
    acc_ref[...] = jnp.zeros(acc_ref.shape, F32)

    def run(online):
        if online:
            m_ref[...] = jnp.full(m_ref.shape, NEG_INF, F32)
        else:
            bounds = _logit_bounds(q_ref[0], group_ref, stats_ref[0, 1:2, :])
            for h in range(MLA_HEADS):
                m_ref[h] = jnp.broadcast_to(bounds[:, h:h + 1], (t, LANES))

        def step(k0, n, diagonal):
            vst = _stack_masked_v(v_ref[0, pl.ds(k0, n), :])
            probs, alphas = [], []
            for h in range(MLA_HEADS):
                cols = slice(LANES * h, LANES * (h + 1))
                sc = _dot_nt(q_ref[0, :, cols], k_ref[0, pl.ds(k0, n), cols])
                if diagonal:
                    row = lax.broadcasted_iota(jnp.int32, (t, n), 0)
                    col = lax.broadcasted_iota(jnp.int32, (t, n), 1)
                    sc = jnp.where(row >= col, sc, NEG_INF)
                p, alpha = _softmax_step(sc, m_ref, l_ref, h, online)
                probs.append(p.astype(BF16))
                alphas.append(alpha)
            pv = _dot(jnp.concatenate(probs, axis=1), vst)
            acc_ref[...] = (acc_ref[...] * _head_lanes(alphas, t) if online else acc_ref[...]) + pv

        def far_body(j, carry):
            step(pl.multiple_of(j * far, far), far, False)
            return carry

        def near_body(j, carry):
            step(pl.multiple_of(j * t, t), t, False)
            return carry

        per_far = far // t
        n_far = qi // per_far
        lax.fori_loop(0, n_far, far_body, 0)
        lax.fori_loop(n_far * per_far, qi, near_body, 0)
        step(pl.multiple_of(qi * t, t), t, True)

    safe = safe_ref[0] != 0
    pl.when(safe)(functools.partial(run, False))
    pl.when(safe_ref[0] == 0)(functools.partial(run, True))
    inv_l = _head_lanes([1.0 / _row_total(l_ref, h) for h in range(MLA_HEADS)], t)
    o_ref[0] = (acc_ref[...] * inv_l).astype(BF16)


def _mla_attn(safe, q, k, v, stats, group):
    b, s, _ = q.shape
    t = min(ATTN_TILE, s)
    far = min(MLA_FAR_KEYS, s)
    return pl.pallas_call(
        functools.partial(_mla_attn_kernel, t=t, far=far),
        grid=(b, s // t),
        in_specs=[pl.BlockSpec(memory_space=pltpu.SMEM),
                  pl.BlockSpec((1, t, 512), lambda bi, qi: (bi, qi, 0)),
                  pl.BlockSpec((1, s, 512), lambda bi, qi: (bi, 0, 0)),
                  pl.BlockSpec((1, s, 256), lambda bi, qi: (bi, 0, 0)),
                  pl.BlockSpec((1, SUBLANES, LANES), lambda bi, qi: (bi, 0, 0)),
                  _const_spec(group.shape)],
        out_specs=pl.BlockSpec((1, t, 256), lambda bi, qi: (bi, qi, 0)),
        out_shape=jax.ShapeDtypeStruct((b, s, 256), BF16),
        scratch_shapes=[pltpu.VMEM((MLA_HEADS, t, LANES), F32),
                        pltpu.VMEM((MLA_HEADS, t, LANES), F32),
                        pltpu.VMEM((t, 256), F32)],
        compiler_params=pltpu.CompilerParams(
            dimension_semantics=("arbitrary", "arbitrary"),
            vmem_limit_bytes=VMEM_LIMIT_BYTES),
        name="mla_attn",
    )(safe, q, k, v, stats, group)


def _bias_tiles_kernel(rb_ref, out_ref, *, t):
    row = lax.broadcasted_iota(jnp.int32, (t, t), 0)
    col = lax.broadcasted_iota(jnp.int32, (t, t), 1)
    max_exact = REL_BUCKETS // 2
    for j in range(2):
        rel = (1 - j) * t + row - col
        n = jnp.maximum(rel, 0)
        nf = jnp.maximum(n, 1).astype(F32)
        large = max_exact + (jnp.log(nf / max_exact) / math.log(REL_MAX_DIST / max_exact)
                             * (REL_BUCKETS - max_exact)).astype(jnp.int32)
        large = jnp.minimum(large, REL_BUCKETS - 1)
        bucket = jnp.where(n < max_exact, n, large)
        for h in range(DIFF_HEADS):
            far = rb_ref[REL_BUCKETS - 1, h]
            val = jnp.zeros((t, t), F32)
            for bkt in range(REL_BUCKETS - 1):
                val = jnp.where(bucket == bkt, (rb_ref[bkt, h] - far) * LOG2_E, val)
            out_ref[h, j] = jnp.where(rel >= 0, val, NEG_INF)


def _bias_tiles(rel_bias, t):
    return pl.pallas_call(
        functools.partial(_bias_tiles_kernel, t=t),
        in_specs=[pl.BlockSpec(memory_space=pltpu.SMEM)],
        out_specs=pl.BlockSpec(memory_space=pltpu.VMEM),
        out_shape=jax.ShapeDtypeStruct((DIFF_HEADS, 2, t, t), F32),
        name="bias_tiles",
    )(rel_bias)


def _diff_attn_kernel(safe_ref, q_ref, k_ref, v_ref, stats_ref, group_ref, bias_hi_ref, bias_ref,
                      lq1_ref, lk1_ref, lq2_ref, lk2_ref, subln_ref, o_ref,
                      qst_ref, m_ref, l_ref, acc_ref, s0_ref, s1_ref, *, t, lambda_init):
    qi = pl.program_id(1)
    nmaps = 2 * DIFF_HEADS
    l_ref[...] = jnp.zeros(l_ref.shape, F32)
    acc_ref[...] = jnp.zeros(acc_ref.shape, F32)
    q = q_ref[0]
    lane = lax.broadcasted_iota(jnp.int32, (t, 256), 1)
    for idx in range(nmaps):
        keep = (lane >= DIFF_HALF * idx) & (lane < DIFF_HALF * (idx + 1))
        qst_ref[idx * t:(idx + 1) * t, :] = jnp.where(keep, q, jnp.zeros_like(q))

    def scores(kb):
        return _dot_nt(qst_ref[...], k_ref[0, pl.ds(pl.multiple_of(kb * t, t), t), :])

    def consume(sc_of, kb, bias_tile, online):
        vst = _stack_masked_v(v_ref[0, pl.ds(pl.multiple_of(kb * t, t), t), :])
        probs = [[], []]
        alphas = [[], []]
        for h in range(DIFF_HEADS):
            for which in range(2):
                idx = 2 * h + which
                sc = sc_of(idx)
                if bias_tile is not None:
                    sc = sc + bias_ref[h, bias_tile]
                p, alpha = _softmax_step(sc, m_ref, l_ref, idx, online)
                probs[which].append(p.astype(BF16))
                alphas[which].append(alpha)
        p_all = jnp.concatenate([jnp.concatenate(probs[0], axis=1),
                                 jnp.concatenate(probs[1], axis=1)], axis=0)
        pv = _dot(p_all, vst)
        for which in range(2):
            rows = slice(which * t, (which + 1) * t)
            old = acc_ref[rows, :] * _head_lanes(alphas[which], t) if online else acc_ref[rows, :]
            acc_ref[rows, :] = old + pv[rows, :]

    def from_ref(ref):
        return lambda idx: ref[idx * t:(idx + 1) * t, :]

    n_plain = jnp.maximum(qi - 1, 0)

    def run_online():
        m_ref[...] = jnp.full(m_ref.shape, NEG_INF, F32)

        def step(kb, bias_tile):
            sc_all = scores(kb)
            consume(lambda idx: sc_all[idx * t:(idx + 1) * t, :], kb, bias_tile, True)

        def body(kb, carry):
            step(kb, None)
            return carry

        lax.fori_loop(0, n_plain, body, 0)
        pl.when(qi > 0)(lambda: step(qi - 1, 0))
        step(qi, 1)

    def run_fixed():
        bounds = _logit_bounds(q, group_ref, stats_ref[0, 3:4, :]) + bias_hi_ref[...]
        for idx in range(nmaps):
            m_ref[idx] = jnp.broadcast_to(bounds[:, idx:idx + 1], (t, LANES))

        def pair_body(i, carry):
            s1_ref[...] = scores(2 * i + 1)
            consume(from_ref(s0_ref), 2 * i, None, False)
            s0_ref[...] = scores(2 * i + 2)
            consume(from_ref(s1_ref), 2 * i + 1, None, False)
            return carry

        s0_ref[...] = scores(0)
        n_pairs = n_plain // 2
        lax.fori_loop(0, n_pairs, pair_body, 0)
        b0 = 2 * n_pairs
        left = qi - b0

        @pl.when(left == 0)
        def _():
            consume(from_ref(s0_ref), 0, 1, False)

        @pl.when(left == 1)
        def _():
            s1_ref[...] = scores(b0 + 1)
            consume(from_ref(s0_ref), b0, 0, False)
            consume(from_ref(s1_ref), b0 + 1, 1, False)

        @pl.when(left == 2)
        def _():
            s1_ref[...] = scores(b0 + 1)
            consume(from_ref(s0_ref), b0, None, False)
            sc_last = scores(b0 + 2)
            consume(from_ref(s1_ref), b0 + 1, 0, False)
            consume(lambda idx: sc_last[idx * t:(idx + 1) * t, :], b0 + 2, 1, False)

    safe = safe_ref[0] != 0
    pl.when(safe)(run_fixed)
    pl.when(safe_ref[0] == 0)(run_online)

    lam = (jnp.exp(jnp.sum(lq1_ref[...] * lk1_ref[...], axis=-1, keepdims=True))
           - jnp.exp(jnp.sum(lq2_ref[...] * lk2_ref[...], axis=-1, keepdims=True))
           + lambda_init)
    o1 = acc_ref[0:t, :] * _head_lanes([1.0 / _row_total(l_ref, 2 * h) for h in range(DIFF_HEADS)], t)
    o2 = acc_ref[t:2 * t, :] * _head_lanes(
        [1.0 / _row_total(l_ref, 2 * h + 1) for h in range(DIFF_HEADS)], t)
    o = o1 - lam * o2
    osq = o * o
    ms = []
    for h in range(DIFF_HEADS):
        keep = (lane >= DIFF_V * h) & (lane < DIFF_V * (h + 1))
        ms.append(jnp.sum(jnp.where(keep, osq, 0.0), axis=-1, keepdims=True) * (1.0 / DIFF_V))
    on = o * lax.rsqrt(_head_lanes(ms, t) + 1e-5) * subln_ref[...]
    o_ref[0] = (on * (1.0 - lambda_init)).astype(BF16)


def _diff_attn(layer, safe, q, k, v, stats, group, bias_hi, bias_tiles, lq1, lk1, lq2, lk2, subln,
               lambda_init):
    b, s, _ = q.shape
    t = min(ATTN_TILE, s)
    return pl.pallas_call(
        functools.partial(_diff_attn_kernel, t=t, lambda_init=lambda_init),
        grid=(b, s // t),
        in_specs=[pl.BlockSpec(memory_space=pltpu.SMEM),
                  pl.BlockSpec((1, t, 256), lambda bi, qi: (bi, qi, 0)),
                  pl.BlockSpec((1, s, 256), lambda bi, qi: (bi, 0, 0)),
                  pl.BlockSpec((1, s, 256), lambda bi, qi: (bi, 0, 0)),
                  pl.BlockSpec((1, SUBLANES, LANES), lambda bi, qi: (bi, 0, 0)),
                  _const_spec(group.shape), _const_spec(bias_hi.shape),
                  _const_spec(bias_tiles.shape),
                  _layer_spec(lq1, layer), _layer_spec(lk1, layer),
                  _layer_spec(lq2, layer), _layer_spec(lk2, layer),
                  _layer_spec(subln, layer)],
        out_specs=pl.BlockSpec((1, t, 256), lambda bi, qi: (bi, qi, 0)),
        out_shape=jax.ShapeDtypeStruct((b, s, 256), BF16),
        scratch_shapes=[pltpu.VMEM((8 * t, 256), BF16),
                        pltpu.VMEM((2 * DIFF_HEADS, t, LANES), F32),
                        pltpu.VMEM((2 * DIFF_HEADS, t, LANES), F32),
                        pltpu.VMEM((2 * t, 256), F32),
                        pltpu.VMEM((8 * t, t), F32),
                        pltpu.VMEM((8 * t, t), F32)],
        compiler_params=pltpu.CompilerParams(
            dimension_semantics=("arbitrary", "arbitrary"),
            vmem_limit_bytes=VMEM_LIMIT_BYTES),
        name="diff_attn",
    )(safe, q, k, v, stats, group, bias_hi, bias_tiles, lq1, lk1, lq2, lk2, subln)


def _mem_kv_kernel(mem_ref, g_ref, wk_ref, wv_ref, k_ref, v_ref):
    m = mem_ref[0]
    mn = m * lax.rsqrt(jnp.mean(m * m, axis=-1, keepdims=True) + 1e-6) * g_ref[0]
    mb = mn.astype(BF16)
    k_ref[0, 0] = _dot(mb, wk_ref[0]).astype(BF16)
    v_ref[0, 0] = _dot(mb, wv_ref[0]).astype(BF16)


def _mem_kv(mem, g, wk, wv):
    b, m, d = mem.shape
    nl = wk.shape[0]
    out = jax.ShapeDtypeStruct((nl, b, m, d), BF16)
    return pl.pallas_call(
        _mem_kv_kernel,
        grid=(nl, b),
        in_specs=[pl.BlockSpec((1, m, d), lambda li, bi: (bi, 0, 0)),
                  pl.BlockSpec((1, 1, d), lambda li, bi: (li, 0, 0)),
                  pl.BlockSpec((1, d, d), lambda li, bi: (li, 0, 0)),
                  pl.BlockSpec((1, d, d), lambda li, bi: (li, 0, 0))],
        out_specs=[pl.BlockSpec((1, 1, m, d), lambda li, bi: (li, bi, 0, 0))] * 2,
        out_shape=[out, out],
        compiler_params=pltpu.CompilerParams(
            dimension_semantics=("arbitrary", "arbitrary"),
            vmem_limit_bytes=VMEM_LIMIT_BYTES),
        name="mem_kv",
    )(mem, g, wk, wv)


def _xa_kernel(x_ref, ya_ref, yb_ref, yc_ref, yd_ref, wout_ref, g_ref, wq_ref, km_ref, vm_ref,
               wo_ref, o_ref, obuf):
    x1 = x_ref[0]
    for i, y_ref in enumerate((ya_ref, yb_ref, yc_ref, yd_ref)):
        x1 = x1 + _dot(y_ref[0], wout_ref[256 * i:256 * (i + 1), :])
    hx = x1 * lax.rsqrt(jnp.mean(x1 * x1, axis=-1, keepdims=True) + 1e-6) * g_ref[...]
    hd = wq_ref.shape[1] // XA_HEADS
    q = (_dot(hx.astype(BF16), wq_ref[...]) * (hd ** -0.5)).astype(BF16)
    for h in range(XA_HEADS):
        cols = slice(hd * h, hd * (h + 1))
        sc = _dot_nt(q[:, cols], km_ref[0, :, cols])
        p = jnp.exp(sc - jnp.max(sc, axis=-1, keepdims=True))
        oh = _dot(p.astype(BF16), vm_ref[0, :, cols]) / jnp.sum(p, axis=-1, keepdims=True)
        obuf[:, cols] = oh.astype(BF16)
    o_ref[0] = x1 + _dot(obuf[...], wo_ref[...])


def _xa(layer, x, ya, yb, yc, yd, wout, g, wq, kmem, vmem, wo):
    b, s, d = x.shape
    tm = min(TOKEN_TILE, s)
    m = kmem.shape[2]
    tok = lambda w: pl.BlockSpec((1, tm, w), lambda bi, si: (bi, si, 0))
    mem_spec = pl.BlockSpec((None, 1, m, d), lambda bi, si: (layer, bi, 0, 0))
    return pl.pallas_call(
        _xa_kernel,
        grid=(b, s // tm),
        in_specs=[tok(d), tok(256), tok(256), tok(256), tok(256),
                  _layer_spec(wout, layer), _layer_spec(g, layer), _layer_spec(wq, layer),
                  mem_spec, mem_spec, _layer_spec(wo, layer)],
        out_specs=tok(d),
        out_shape=jax.ShapeDtypeStruct((b, s, d), F32),
        scratch_shapes=[pltpu.VMEM((tm, d), BF16)],
        compiler_params=pltpu.CompilerParams(
            dimension_semantics=("arbitrary", "arbitrary"),
            vmem_limit_bytes=VMEM_LIMIT_BYTES),
        name="xa",
    )(x, ya, yb, yc, yd, wout, g, wq, kmem, vmem, wo)


def _ffn_kernel(x_ref, g_ref, wup_ref, dw_ref, dwb_ref, wdown_ref, gfin_ref, o_ref,
                carry, acc_ref, perm_in, perm_out, *, tm, dff, final_norm):
    s = pl.program_id(1)
    groups = tm // SUBLANES

    @pl.when(s == 0)
    def _():
        carry[...] = jnp.zeros(carry.shape, F32)

    x = x_ref[0]
    h = x * lax.rsqrt(jnp.mean(x * x, axis=-1, keepdims=True) + 1e-6) * g_ref[...]
    hb = jnp.concatenate(_to_group_order(h, perm_in, groups), axis=0).astype(BF16)
    nchunk = dff // FFN_CHUNK
    first_sublane = lax.broadcasted_iota(jnp.int32, (SUBLANES, FFN_CHUNK), 0) == 0

    def conv(u, slot, col0):
        cols = slice(col0, col0 + FFN_CHUNK)
        wcols = slice(slot * dff + col0, slot * dff + col0 + FFN_CHUNK)
        tail = u[tm - 2 * SUBLANES:tm, :]
        prev = carry[slot, :, cols]
        carry[slot, :, cols] = tail

        def wrap(k):
            lo, hi = k * SUBLANES, (k + 1) * SUBLANES
            return jnp.where(first_sublane, pltpu.roll(prev[lo:hi, :], 1, 0),
                             pltpu.roll(tail[lo:hi, :], 1, 0))

        back1 = wrap(1)
        u1 = jnp.concatenate([back1, u[0:tm - SUBLANES, :]], axis=0)
        u2 = jnp.concatenate([wrap(0), back1, u[0:tm - 2 * SUBLANES, :]], axis=0)
        w = dw_ref[:, wcols]
        return dwb_ref[:, wcols] + w[2:3, :] * u + w[1:2, :] * u1 + w[0:1, :] * u2

    def up(j):
        col0 = j * FFN_CHUNK
        return (_dot(hb, wup_ref[:, col0:col0 + FFN_CHUNK]),
                _dot(hb, wup_ref[:, dff + col0:dff + col0 + FFN_CHUNK]))

    ua, ug = up(0)
    for j in range(nchunk):
        col0 = j * FFN_CHUNK
        nxt = up(j + 1) if j + 1 < nchunk else None
        a = conv(ua, 0, col0)
        gt = conv(ug, 1, col0)
        act = (gt * _sigmoid(gt) * a).astype(BF16)
        part = _dot(act, wdown_ref[col0:col0 + FFN_CHUNK, :])
        if j == 0:
            acc_ref[...] = part
        else:
            acc_ref[...] += part
        if nxt is not None:
            ua, ug = nxt
    y = x + _to_row_order(acc_ref[...], perm_out, groups)
    if final_norm:
        y = y * lax.rsqrt(jnp.mean(y * y, axis=-1, keepdims=True) + 1e-6) * gfin_ref[...]
    o_ref[0] = y


def _ffn(layer, x, g, wup, dw, dwb, wdown, gfin, final_norm):
    b, s, d = x.shape
    tm = min(TOKEN_TILE, s)
    dff = wdown.shape[1]
    tok = pl.BlockSpec((1, tm, d), lambda bi, si: (bi, si, 0))
    return pl.pallas_call(
        functools.partial(_ffn_kernel, tm=tm, dff=dff, final_norm=final_norm),
        grid=(b, s // tm),
        in_specs=[tok, _layer_spec(g, layer), _layer_spec(wup, layer), _layer_spec(dw, layer),
                  _layer_spec(dwb, layer), _layer_spec(wdown, layer), _const_spec(gfin.shape)],
        out_specs=tok,
        out_shape=jax.ShapeDtypeStruct((b, s, d), F32),
        scratch_shapes=[pltpu.VMEM((2, 2 * SUBLANES, dff), F32),
                        pltpu.VMEM((tm, d), F32),
                        pltpu.VMEM((d // LANES, tm, LANES), F32),
                        pltpu.VMEM((d // LANES, tm, LANES), F32)],
        compiler_params=pltpu.CompilerParams(
            dimension_semantics=("arbitrary", "arbitrary"),
            vmem_limit_bytes=VMEM_LIMIT_BYTES),
        name="ffn",
    )(x, g, wup, dw, dwb, wdown, gfin)


def _pack_w_in(w):
    ua, cq, ckv, kr, qc, kc, vc, ud = jnp.split(
        w, [512, 704, 832, 864, 1120, 1376, 1632], axis=-1)
    z = lambda n: jnp.zeros(w.shape[:-1] + (n,), w.dtype)
    half = MLA_ROPE // 2
    kr_sw = jnp.concatenate([kr[..., half:], kr[..., :half]], axis=-1)
    kr_blk = jnp.concatenate([z(MLA_NOPE), kr, z(LANES - MLA_NOPE - MLA_ROPE)], axis=-1)
    kr_sw_blk = jnp.concatenate([z(MLA_NOPE), kr_sw, z(LANES - MLA_NOPE - MLA_ROPE)], axis=-1)
    packed = jnp.concatenate(
        [ua, cq, z(256 - MLA_Q_RANK), ckv, kr_blk, kr_sw_blk, z(LANES), qc, kc, vc, ud], axis=-1)
    return packed.astype(BF16)


def _pack_w_uq(w):
    nl = w.shape[0]
    w = w.reshape(nl, MLA_Q_RANK, MLA_HEADS, MLA_NOPE + MLA_ROPE)
    nope, rp = w[..., :MLA_NOPE], w[..., MLA_NOPE:]
    half = MLA_ROPE // 2
    rp_sw = jnp.concatenate([rp[..., half:], rp[..., :half]], axis=-1)
    zn = jnp.zeros_like(nope)
    zp = jnp.zeros((nl, MLA_Q_RANK, MLA_HEADS, LANES - MLA_NOPE - MLA_ROPE), w.dtype)

    def fin(a):
        a = a.reshape(nl, MLA_Q_RANK, MLA_HEADS * LANES)
        return jnp.pad(a, ((0, 0), (0, 256 - MLA_Q_RANK), (0, 0))).astype(BF16)

    return (fin(jnp.concatenate([nope, rp, zp], axis=-1)),
            fin(jnp.concatenate([zn, rp_sw, zp], axis=-1)))


def _pack_w_ukv(w):
    nl = w.shape[0]
    w = w.reshape(nl, MLA_KV_RANK, MLA_HEADS, MLA_NOPE + MLA_V)
    kn, v = w[..., :MLA_NOPE], w[..., MLA_NOPE:]
    kn = jnp.concatenate([kn, jnp.zeros_like(kn)], axis=-1).reshape(nl, MLA_KV_RANK, MLA_HEADS * LANES)
    v = v.reshape(nl, MLA_KV_RANK, MLA_HEADS * MLA_V)
    return jnp.concatenate([kn, v], axis=-1).astype(BF16)


def _rope_tables(positions):
    inv_freq = ROPE_BASE ** (-jnp.arange(0, MLA_ROPE, 2, dtype=F32) / MLA_ROPE)
    ang = positions.astype(F32)[:, None] * inv_freq[None, :]
    cos, sin = jnp.cos(ang), jnp.sin(ang)
    s = positions.shape[0]
    pad = jnp.zeros((s, LANES - MLA_NOPE - MLA_ROPE), F32)
    ctab = jnp.concatenate([jnp.ones((s, MLA_NOPE), F32), cos, cos, pad], axis=1)
    stab = jnp.concatenate([jnp.zeros((s, MLA_NOPE), F32), -sin, sin, pad], axis=1)
    return ctab, stab


def _lane_groups(n_lanes, width):
    lane = jnp.arange(n_lanes)[:, None] // width
    return (lane == jnp.arange(LANES)[None, :]).astype(BF16)


def _bias_range(rel_bias):
    shifted = (rel_bias - rel_bias[REL_BUCKETS - 1]) * LOG2_E
    hi = jnp.repeat(jnp.max(shifted, axis=0), 2)
    lo = jnp.repeat(jnp.min(shifted, axis=0), 2)
    pad = (0, LANES - 2 * DIFF_HEADS)
    return jnp.pad(hi, pad).reshape(1, LANES), jnp.pad(hi - lo, pad).reshape(1, LANES)


def _fixed_stabiliser_ok(qmax_sq, kmax_sq, extra_span):
    bound = NORM_MARGIN * jnp.sqrt(qmax_sq * kmax_sq)
    ok = jnp.all(2.0 * bound + extra_span <= SAFE_LOGIT_SPAN)
    return ok.astype(jnp.int32).reshape(1)


def _block_diag(w):
    nl, g, c, _ = w.shape
    out = jnp.zeros((nl, g * c, g * c), w.dtype)
    for i in range(g):
        out = out.at[:, i * c:(i + 1) * c, i * c:(i + 1) * c].set(w[:, i])
    return out


def kernel(x, mem, positions, rel_bias, norm_mix, w_in, w_out, conv_dw, conv_dw_b, conv_ln_g, conv_ln_b, conv_pw, conv_pw_b, mla_q_norm, mla_w_uq, mla_kv_norm, mla_w_ukv, diff_lq1, diff_lk1, diff_lq2, diff_lk2, diff_subln, pool_w, pool_scale, norm_xa, mem_norm, xa_wq, xa_wk, xa_wv, xa_wo, norm_ffn, ffn_up, ffn_dw, ffn_dw_b, ffn_down, norm_final):
    depth = w_in.shape[0]
    s = x.shape[1]
    rows = lambda a: a[:, None, :]
    bf = lambda a: a.astype(BF16)
    ctab, stab = _rope_tables(positions)
    bias_tiles = _bias_tiles(rel_bias, min(ATTN_TILE, s))
    kmem, vmem = _mem_kv(mem, rows(mem_norm), bf(xa_wk), bf(xa_wv))
    group_mla = _lane_groups(MLA_HEADS * LANES, LANES)
    group_diff = _lane_groups(2 * DIFF_HEADS * DIFF_HALF, DIFF_HALF)
    bias_hi, bias_span = _bias_range(rel_bias)
    wuq, wuqs = _pack_w_uq(mla_w_uq)
    mix_params = [rows(norm_mix), _pack_w_in(w_in), conv_dw, rows(conv_dw_b), rows(conv_ln_g),
                  rows(conv_ln_b), bf(conv_pw), rows(conv_pw_b),
                  rows(jnp.pad(mla_q_norm, ((0, 0), (0, 256 - MLA_Q_RANK)))), wuq, wuqs,
                  rows(mla_kv_norm), _pack_w_ukv(mla_w_ukv),
                  bf(_block_diag(pool_w)), rows(pool_scale)]
    lq1, lk1, lq2, lk2 = rows(diff_lq1), rows(diff_lk1), rows(diff_lq2), rows(diff_lk2)
    subln = rows(jnp.tile(diff_subln, (1, DIFF_HEADS)))
    w_out, xa_wq, xa_wo, ffn_up, ffn_down = bf(w_out), bf(xa_wq), bf(xa_wo), bf(ffn_up), bf(ffn_down)
    g_xa, g_ffn, ffn_dw_b = rows(norm_xa), rows(norm_ffn), rows(ffn_dw_b)
    for l in range(depth):
        ya, qm, km, vm, qd, kd, vd, yd, stats = _mix_in(l, x, mix_params, ctab, stab, group_mla,
                                                         group_diff)
        stats = jnp.max(stats, axis=1)
        yb = _mla_attn(_fixed_stabiliser_ok(stats[:, 0], stats[:, 1], 0.0), qm, km, vm, stats,
                       group_mla)
        lambda_init = 0.8 - 0.6 * math.exp(-0.3 * l)
        yc = _diff_attn(l, _fixed_stabiliser_ok(stats[:, 2], stats[:, 3], bias_span), qd, kd, vd,
                        stats, group_diff, bias_hi, bias_tiles, lq1, lk1, lq2, lk2, subln,
                        lambda_init)
        x = _xa(l, x, ya, yb, yc, yd, w_out, g_xa, xa_wq, kmem, vmem, xa_wo)
        x = _ffn(l, x, g_ffn, ffn_up, ffn_dw, ffn_dw_b, ffn_down, norm_final.reshape(1, -1),
                 l == depth - 1)
    return x
```

```python
import functools
import math

import jax
import jax.numpy as jnp
from jax import lax
from jax.experimental import pallas as pl
from jax.experimental.pallas import tpu as pltpu

N_GROUPS = 4
CONV_KERNEL = 31
MLA_HEADS = 4
MLA_NOPE = 64
MLA_ROPE = 32
MLA_V = 64
MLA_Q_RANK = 192
MLA_KV_RANK = 128
DIFF_HEADS = 4
DIFF_HALF = 32
DIFF_V = 64
POOL_WINDOWS = (2, 4, 8, 16)
POOL_CH = 64
REL_BUCKETS = 32
REL_MAX_DIST = 128
XA_HEADS = 4
FFN_CONV = 3
ROPE_BASE = 10000.0
NEG_INF = -1e30
LOG2_E = math.log2(math.e)
NORM_MARGIN = 1.02
SAFE_LOGIT_SPAN = 100.0

LANES = 128
SUBLANES = 8
VMEM_LIMIT_BYTES = 56 * 1024 * 1024

TOKEN_TILE = 512
ATTN_TILE = 256
MLA_FAR_KEYS = 1024
CONV_HALO = 32
POOL_HALO = 16
ROW_CHUNK = 64
FFN_CHUNK = 256

BF16 = jnp.bfloat16
F32 = jnp.float32


def _dot(a, b):
    return jnp.dot(a, b, preferred_element_type=F32)


def _dot_nt(a, b):
    return lax.dot_general(a, b, (((1,), (1,)), ((), ())), preferred_element_type=F32)


def _sigmoid(x):
    return 1.0 / (1.0 + jnp.exp(-x))


def _const_spec(shape):
    nd = len(shape)
    return pl.BlockSpec(shape, lambda *_: (0,) * nd)


def _layer_spec(stacked, layer):
    tail = stacked.shape[1:]
    return pl.BlockSpec((None,) + tail, lambda *_: (layer,) + (0,) * len(tail))


_A0, _A1 = 0, 512
_B0, _B1 = 512, 1280
_C0, _C1 = 1280, 2048
_D0, _D1 = 2048, 2304
W_IN_COLS = 2304


def _to_group_order(val, perm_ref, groups):
    ntile = val.shape[1] // LANES
    pitch = groups + SUBLANES // 2
    for c in range(ntile):
        for i in range(SUBLANES):
            perm_ref[c, i * pitch:i * pitch + groups, :] = val[i * groups:(i + 1) * groups,
                                                                 c * LANES:(c + 1) * LANES]
    return [jnp.concatenate([perm_ref[c, pl.ds(j, SUBLANES, stride=pitch), :]
                             for c in range(ntile)], axis=1) for j in range(groups)]


def _to_row_order(val, perm_ref, groups):
    ntile = val.shape[1] // LANES
    pitch = groups + SUBLANES // 2
    for j in range(groups):
        for c in range(ntile):
            perm_ref[c, pl.ds(j, SUBLANES, stride=pitch), :] = val[
                j * SUBLANES:(j + 1) * SUBLANES, c * LANES:(c + 1) * LANES]
    return jnp.concatenate(
        [jnp.concatenate([perm_ref[c, i * pitch:i * pitch + groups, :] for i in range(SUBLANES)],
                         axis=0) for c in range(ntile)], axis=1)


def _fill_window(grps, ext_ref, prev_ref, halo):
    groups = len(grps)
    first_sublane = lax.broadcasted_iota(jnp.int32, grps[0].shape, 0) == 0
    for m in range(groups - halo, groups):
        rows = slice((m - groups + halo) * SUBLANES, (m - groups + halo + 1) * SUBLANES)
        ext_ref[rows, :] = jnp.where(first_sublane, pltpu.roll(prev_ref[rows, :], 1, 0),
                                     pltpu.roll(grps[m], 1, 0))
        prev_ref[rows, :] = grps[m]
    for j in range(groups):
        ext_ref[(halo + j) * SUBLANES:(halo + j + 1) * SUBLANES, :] = grps[j]


def _mix_in_kernel(x_ref, g_ref, win_ref, dw_ref, dwb_ref, lng_ref, lnb_ref, pw_ref, pwb_ref,
                   qn_ref, wuq_ref, wuqs_ref, kvn_ref, wukv_ref, ct_ref, st_ref,
                   poolw_ref, pools_ref, gm_ref, gd_ref,
                   ya_ref, qm_ref, km_ref, vm_ref, qd_ref, kd_ref, vd_ref, yd_ref, stats_ref,
                   ext_a, prev_a, zbuf, ext_d, prev_d, dbuf, perm_a, perm_ya, perm_d, perm_yd, *, tm):
    s = pl.program_id(1)
    groups = tm // SUBLANES

    def max_sq_norm(v, group_ref):
        vf = v.astype(F32)
        return jnp.max(_dot((vf * vf).astype(BF16), group_ref[...]), axis=0, keepdims=True)

    @pl.when(s == 0)
    def _():
        prev_a[...] = jnp.zeros(prev_a.shape, F32)
        prev_d[...] = jnp.zeros(prev_d.shape, F32)

    x = x_ref[0]
    h = x * lax.rsqrt(jnp.mean(x * x, axis=-1, keepdims=True) + 1e-6) * g_ref[...]
    hb = h.astype(BF16)

    pa = _dot(hb, win_ref[:, _A0:_A1])
    pb = _dot(hb, win_ref[:, _B0:_B1])
    pc = _dot(hb, win_ref[:, _C0:_C1])
    ud = _dot(hb, win_ref[:, _D0:_D1])

    glu = pa[:, 0:256] * _sigmoid(pa[:, 256:512])
    _fill_window(_to_group_order(glu, perm_a, groups), ext_a, prev_a, CONV_HALO)
    for c in range(tm // ROW_CHUNK):
        r0 = c * ROW_CHUNK
        acc = jnp.broadcast_to(dwb_ref[...], (ROW_CHUNK, 256))
        for k in range(CONV_KERNEL):
            off = r0 + (CONV_HALO - (CONV_KERNEL - 1) + k) * SUBLANES
            acc = acc + dw_ref[k:k + 1, :] * ext_a[off:off + ROW_CHUNK, :]
        mu = jnp.mean(acc, axis=-1, keepdims=True)
        cen = acc - mu
        var = jnp.mean(cen * cen, axis=-1, keepdims=True)
        yn = cen * lax.rsqrt(var + 1e-5) * lng_ref[...] + lnb_ref[...]
        zbuf[r0:r0 + ROW_CHUNK, :] = (yn * _sigmoid(yn)).astype(BF16)
    ya = _to_row_order(_dot(zbuf[...], pw_ref[...]), perm_ya, groups)
    ya_ref[0] = (ya + pwb_ref[...]).astype(BF16)

    ct = ct_ref[...]
    st = st_ref[...]
    cq = pb[:, 0:256]
    cqn = cq * lax.rsqrt(jnp.sum(cq * cq, axis=-1, keepdims=True) * (1.0 / MLA_Q_RANK) + 1e-6)
    cqb = (cqn * qn_ref[...]).astype(BF16)
    q = _dot(cqb, wuq_ref[...])
    qs = _dot(cqb, wuqs_ref[...])
    ct4 = jnp.concatenate([ct] * MLA_HEADS, axis=1)
    st4 = jnp.concatenate([st] * MLA_HEADS, axis=1)
    q_scale = LOG2_E * (MLA_NOPE + MLA_ROPE) ** -0.5
    qm = ((q * ct4 + qs * st4) * q_scale).astype(BF16)
    qm_ref[0] = qm
    ckv = pb[:, 256:384]
    ckvn = ckv * lax.rsqrt(jnp.mean(ckv * ckv, axis=-1, keepdims=True) + 1e-6)
    kv = _dot((ckvn * kvn_ref[...]).astype(BF16), wukv_ref[...])
    kr = pb[:, 384:512] * ct + pb[:, 512:640] * st
    km = (kv[:, 0:512] + jnp.concatenate([kr] * MLA_HEADS, axis=1)).astype(BF16)
    km_ref[0] = km
    vm_ref[0] = kv[:, 512:768].astype(BF16)

    qd = (pc[:, 0:256] * (LOG2_E * DIFF_HALF ** -0.5)).astype(BF16)
    kd = pc[:, 256:512].astype(BF16)
    qd_ref[0] = qd
    kd_ref[0] = kd
    stats_ref[0, 0] = jnp.concatenate(
        [max_sq_norm(qm, gm_ref), max_sq_norm(km, gm_ref),
         max_sq_norm(qd, gd_ref), max_sq_norm(kd, gd_ref),
         jnp.zeros((SUBLANES - 4, LANES), F32)], axis=0)
    vd_ref[0] = pc[:, 512:768].astype(BF16)

    _fill_window(_to_group_order(ud, perm_d, groups), ext_d, prev_d, POOL_HALO)
    lane = lax.broadcasted_iota(jnp.int32, (ROW_CHUNK, 256), 1)
    grp = lane // POOL_CH
    win_len = jnp.where(grp == 0, POOL_WINDOWS[0],
                        jnp.where(grp == 1, POOL_WINDOWS[1],
                                  jnp.where(grp == 2, POOL_WINDOWS[2], POOL_WINDOWS[3])))
    for c in range(tm // ROW_CHUNK):
        r0 = c * ROW_CHUNK
        base = r0 + POOL_HALO * SUBLANES
        u = ext_d[base:base + ROW_CHUNK, :]
        run = u
        sums = {}
        for j in range(1, POOL_WINDOWS[-1]):
            run = run + ext_d[base - j * SUBLANES:base - j * SUBLANES + ROW_CHUNK, :]
            if j + 1 in POOL_WINDOWS:
                sums[j + 1] = run
        win = jnp.where(grp == 0, sums[2],
                        jnp.where(grp == 1, sums[4], jnp.where(grp == 2, sums[8], sums[16])))
        r = r0 + lax.broadcasted_iota(jnp.int32, (ROW_CHUNK, 256), 0)
        t_glob = s * tm + (r % SUBLANES) * groups + r // SUBLANES
        cnt = jnp.minimum(t_glob + 1, win_len).astype(F32)
        dbuf[r0:r0 + ROW_CHUNK, :] = (win / cnt - u).astype(BF16)
    yd = _to_row_order(_dot(dbuf[...], poolw_ref[...]), perm_yd, groups)
    yd_ref[0] = (yd * pools_ref[...]).astype(BF16)


def _mix_in(layer, x, per_layer, ctab, stab, group_mla, group_diff):
    b, s, d = x.shape
    tm = min(TOKEN_TILE, s)
    tok = lambda w: pl.BlockSpec((1, tm, w), lambda bi, si: (bi, si, 0))
    head, tail = per_layer[:13], per_layer[13:]
    in_specs = ([tok(d)] + [_layer_spec(a, layer) for a in head]
                + [pl.BlockSpec((tm, LANES), lambda bi, si: (si, 0))] * 2
                + [_layer_spec(a, layer) for a in tail]
                + [_const_spec(group_mla.shape), _const_spec(group_diff.shape)])
    widths = (256, 512, 512, 256, 256, 256, 256, 256)
    stats_spec = pl.BlockSpec((1, 1, SUBLANES, LANES), lambda bi, si: (bi, si, 0, 0))
    return pl.pallas_call(
        functools.partial(_mix_in_kernel, tm=tm),
        grid=(b, s // tm),
        in_specs=in_specs,
        out_specs=[tok(w) for w in widths] + [stats_spec],
        out_shape=([jax.ShapeDtypeStruct((b, s, w), BF16) for w in widths]
                   + [jax.ShapeDtypeStruct((b, s // tm, SUBLANES, LANES), F32)]),
        scratch_shapes=[pltpu.VMEM((CONV_HALO * SUBLANES + tm, 256), F32),
                        pltpu.VMEM((CONV_HALO * SUBLANES, 256), F32),
                        pltpu.VMEM((tm, 256), BF16),
                        pltpu.VMEM((POOL_HALO * SUBLANES + tm, 256), F32),
                        pltpu.VMEM((POOL_HALO * SUBLANES, 256), F32),
                        pltpu.VMEM((tm, 256), BF16)]
                       + [pltpu.VMEM((2, tm + SUBLANES * SUBLANES, LANES), F32)] * 4,
        compiler_params=pltpu.CompilerParams(
            dimension_semantics=("arbitrary", "arbitrary"),
            vmem_limit_bytes=VMEM_LIMIT_BYTES),
        name="mix_in",
    )(x, *head, ctab, stab, *tail, group_mla, group_diff)


def _head_lanes(vals, rows):
    lane = lax.broadcasted_iota(jnp.int32, (rows, LANES), 1)
    return jnp.concatenate([jnp.where(lane < 64, vals[0], vals[1]),
                            jnp.where(lane < 64, vals[2], vals[3])], axis=1)


def _row_total(l_ref, idx):
    return jnp.sum(l_ref[idx], axis=-1, keepdims=True)


def _stack_masked_v(v_blk):
    n = v_blk.shape[0]
    lane = lax.broadcasted_iota(jnp.int32, (n, 256), 1)
    parts = []
    for h in range(4):
        keep = (lane >= 64 * h) & (lane < 64 * (h + 1))
        parts.append(jnp.where(keep, v_blk, jnp.zeros_like(v_blk)))
    return jnp.concatenate(parts, axis=0)


def _softmax_step(sc, m_ref, l_ref, idx, online):
    if online:
        m_old = m_ref[idx]
        m_new = jnp.maximum(m_old, jnp.max(sc, axis=-1, keepdims=True))
        alpha = jnp.exp2(m_old - m_new)
        m_ref[idx] = m_new
    else:
        m_new = m_ref[idx]
        alpha = None
    reps = sc.shape[1] // LANES
    p = jnp.exp2(sc - jnp.concatenate([m_new] * reps, axis=1))
    psum = p[:, 0:LANES]
    for r in range(1, reps):
        psum = psum + p[:, r * LANES:(r + 1) * LANES]
    l_ref[idx] = (alpha * l_ref[idx] if online else l_ref[idx]) + psum
    return p, alpha


def _logit_bounds(q, group_ref, kmax_sq):
    qf = q.astype(F32)
    qn_sq = _dot((qf * qf).astype(BF16), group_ref[...])
    return NORM_MARGIN * jnp.sqrt(qn_sq * kmax_sq)


def _mla_attn_kernel(safe_ref, q_ref, k_ref, v_ref, stats_ref, group_ref, o_ref,
                     m_ref, l_ref, acc_ref, *, t, far):
    qi = pl.program_id(1)
    l_ref[...] = jnp.zeros(l_ref.shape, F32)
    acc_ref[...] = jnp.zeros(acc_ref.shape, F32)

    def run(online):
        if online:
            m_ref[...] = jnp.full(m_ref.shape, NEG_INF, F32)
        else:
            bounds = _logit_bounds(q_ref[0], group_ref, stats_ref[0, 1:2, :])
            for h in range(MLA_HEADS):
                m_ref[h] = jnp.broadcast_to(bounds[:, h:h + 1], (t, LANES))

        def step(k0, n, diagonal):
            vst = _stack_masked_v(v_ref[0, pl.ds(k0, n), :])
            probs, alphas = [], []
            for h in range(MLA_HEADS):
                cols = slice(LANES * h, LANES * (h + 1))
                sc = _dot_nt(q_ref[0, :, cols], k_ref[0, pl.ds(k0, n), cols])
                if diagonal:
                    row = lax.broadcasted_iota(jnp.int32, (t, n), 0)
                    col = lax.broadcasted_iota(jnp.int32, (t, n), 1)
                    sc = jnp.where(row >= col, sc, NEG_INF)
                p, alpha = _softmax_step(sc, m_ref, l_ref, h, online)
                probs.append(p.astype(BF16))
                alphas.append(alpha)
            pv = _dot(jnp.concatenate(probs, axis=1), vst)
            acc_ref[...] = (acc_ref[...] * _head_lanes(alphas, t) if online else acc_ref[...]) + pv

        def far_body(j, carry):
            step(pl.multiple_of(j * far, far), far, False)
            return carry

        def near_body(j, carry):
            step(pl.multiple_of(j * t, t), t, False)
            return carry

        per_far = far // t
        n_far = qi // per_far
        lax.fori_loop(0, n_far, far_body, 0)
        lax.fori_loop(n_far * per_far, qi, near_body, 0)
        step(pl.multiple_of(qi * t, t), t, True)

    safe = safe_ref[0] != 0
    pl.when(safe)(functools.partial(run, False))
    pl.when(safe_ref[0] == 0)(functools.partial(run, True))
    inv_l = _head_lanes([1.0 / _row_total(l_ref, h) for h in range(MLA_HEADS)], t)
    o_ref[0] = (acc_ref[...] * inv_l).astype(BF16)


def _mla_attn(safe, q, k, v, stats, group):
    b, s, _ = q.shape
    t = min(ATTN_TILE, s)
    far = min(MLA_FAR_KEYS, s)
    return pl.pallas_call(
        functools.partial(_mla_attn_kernel, t=t, far=far),
        grid=(b, s // t),
        in_specs=[pl.BlockSpec(memory_space=pltpu.SMEM),
                  pl.BlockSpec((1, t, 512), lambda bi, qi: (bi, qi, 0)),
                  pl.BlockSpec((1, s, 512), lambda bi, qi: (bi, 0, 0)),
                  pl.BlockSpec((1, s, 256), lambda bi, qi: (bi, 0, 0)),
                  pl.BlockSpec((1, SUBLANES, LANES), lambda bi, qi: (bi, 0, 0)),
                  _const_spec(group.shape)],
        out_specs=pl.BlockSpec((1, t, 256), lambda bi, qi: (bi, qi, 0)),
        out_shape=jax.ShapeDtypeStruct((b, s, 256), BF16),
        scratch_shapes=[pltpu.VMEM((MLA_HEADS, t, LANES), F32),
                        pltpu.VMEM((MLA_HEADS, t, LANES), F32),
                        pltpu.VMEM((t, 256), F32)],
        compiler_params=pltpu.CompilerParams(
            dimension_semantics=("arbitrary", "arbitrary"),
            vmem_limit_bytes=VMEM_LIMIT_BYTES),
        name="mla_attn",
    )(safe, q, k, v, stats, group)


def _bias_tiles_kernel(rb_ref, out_ref, *, t):
    row = lax.broadcasted_iota(jnp.int32, (t, t), 0)
    col = lax.broadcasted_iota(jnp.int32, (t, t), 1)
    max_exact = REL_BUCKETS // 2
    for j in range(2):
        rel = (1 - j) * t + row - col
        n = jnp.maximum(rel, 0)
        nf = jnp.maximum(n, 1).astype(F32)
        large = max_exact + (jnp.log(nf / max_exact) / math.log(REL_MAX_DIST / max_exact)
                             * (REL_BUCKETS - max_exact)).astype(jnp.int32)
        large = jnp.minimum(large, REL_BUCKETS - 1)
        bucket = jnp.where(n < max_exact, n, large)
        for h in range(DIFF_HEADS):
            far = rb_ref[REL_BUCKETS - 1, h]
            val = jnp.zeros((t, t), F32)
            for bkt in range(REL_BUCKETS - 1):
                val = jnp.where(bucket == bkt, (rb_ref[bkt, h] - far) * LOG2_E, val)
            out_ref[h, j] = jnp.where(rel >= 0, val, NEG_INF)


def _bias_tiles(rel_bias, t):
    return pl.pallas_call(
        functools.partial(_bias_tiles_kernel, t=t),
        in_specs=[pl.BlockSpec(memory_space=pltpu.SMEM)],
        out_specs=pl.BlockSpec(memory_space=pltpu.VMEM),
        out_shape=jax.ShapeDtypeStruct((DIFF_HEADS, 2, t, t), F32),
        name="bias_tiles",
    )(rel_bias)


def _diff_attn_kernel(safe_ref, q_ref, k_ref, v_ref, stats_ref, group_ref, bias_hi_ref, bias_ref,
                      lq1_ref, lk1_ref, lq2_ref, lk2_ref, subln_ref, o_ref,
                      qst_ref, m_ref, l_ref, acc_ref, s0_ref, s1_ref, *, t, lambda_init):
    qi = pl.program_id(1)
    nmaps = 2 * DIFF_HEADS
    l_ref[...] = jnp.zeros(l_ref.shape, F32)
    acc_ref[...] = jnp.zeros(acc_ref.shape, F32)
    q = q_ref[0]
    lane = lax.broadcasted_iota(jnp.int32, (t, 256), 1)
    for idx in range(nmaps):
        keep = (lane >= DIFF_HALF * idx) & (lane < DIFF_HALF * (idx + 1))
        qst_ref[idx * t:(idx + 1) * t, :] = jnp.where(keep, q, jnp.zeros_like(q))

    def scores(kb):
        return _dot_nt(qst_ref[...], k_ref[0, pl.ds(pl.multiple_of(kb * t, t), t), :])

    def consume(sc_of, kb, bias_tile, online):
        vst = _stack_masked_v(v_ref[0, pl.ds(pl.multiple_of(kb * t, t), t), :])
        probs = [[], []]
        alphas = [[], []]
        for h in range(DIFF_HEADS):
            for which in range(2):
                idx = 2 * h + which
                sc = sc_of(idx)
                if bias_tile is not None:
                    sc = sc + bias_ref[h, bias_tile]
                p, alpha = _softmax_step(sc, m_ref, l_ref, idx, online)
                probs[which].append(p.astype(BF16))
                alphas[which].append(alpha)
        p_all = jnp.concatenate([jnp.concatenate(probs[0], axis=1),
                                 jnp.concatenate(probs[1], axis=1)], axis=0)
        pv = _dot(p_all, vst)
        for which in range(2):
            rows = slice(which * t, (which + 1) * t)
            old = acc_ref[rows, :] * _head_lanes(alphas[which], t) if online else acc_ref[rows, :]
            acc_ref[rows, :] = old + pv[rows, :]

    def from_ref(ref):
        return lambda idx: ref[idx * t:(idx + 1) * t, :]

    n_plain = jnp.maximum(qi - 1, 0)

    def run_online():
        m_ref[...] = jnp.full(m_ref.shape, NEG_INF, F32)

        def step(kb, bias_tile):
            sc_all = scores(kb)
            consume(lambda idx: sc_all[idx * t:(idx + 1) * t, :], kb, bias_tile, True)

        def body(kb, carry):
            step(kb, None)
            return carry

        lax.fori_loop(0, n_plain, body, 0)
        pl.when(qi > 0)(lambda: step(qi - 1, 0))
        step(qi, 1)

    def run_fixed():
        bounds = _logit_bounds(q, group_ref, stats_ref[0, 3:4, :]) + bias_hi_ref[...]
        for idx in range(nmaps):
            m_ref[idx] = jnp.broadcast_to(bounds[:, idx:idx + 1], (t, LANES))

        def pair_body(i, carry):
            s1_ref[...] = scores(2 * i + 1)
            consume(from_ref(s0_ref), 2 * i, None, False)
            s0_ref[...] = scores(2 * i + 2)
            consume(from_ref(s1_ref), 2 * i + 1, None, False)
            return carry

        s0_ref[...] = scores(0)
        n_pairs = n_plain // 2
        lax.fori_loop(0, n_pairs, pair_body, 0)
        b0 = 2 * n_pairs
        left = qi - b0

        @pl.when(left == 0)
        def _():
            consume(from_ref(s0_ref), 0, 1, False)

        @pl.when(left == 1)
        def _():
            s1_ref[...] = scores(b0 + 1)
            consume(from_ref(s0_ref), b0, 0, False)
            consume(from_ref(s1_ref), b0 + 1, 1, False)

        @pl.when(left == 2)
        def _():
            s1_ref[...] = scores(b0 + 1)
            consume(from_ref(s0_ref), b0, None, False)
            sc_last = scores(b0 + 2)
            consume(from_ref(s1_ref), b0 + 1, 0, False)
            consume(lambda idx: sc_last[idx * t:(idx + 1) * t, :], b0 + 2, 1, False)

    safe = safe_ref[0] != 0
    pl.when(safe)(run_fixed)
    pl.when(safe_ref[0] == 0)(run_online)

    lam = (jnp.exp(jnp.sum(lq1_ref[...] * lk1_ref[...], axis=-1, keepdims=True))
           - jnp.exp(jnp.sum(lq2_ref[...] * lk2_ref[...], axis=-1, keepdims=True))
           + lambda_init)
    o1 = acc_ref[0:t, :] * _head_lanes([1.0 / _row_total(l_ref, 2 * h) for h in range(DIFF_HEADS)], t)
    o2 = acc_ref[t:2 * t, :] * _head_lanes(
        [1.0 / _row_total(l_ref, 2 * h + 1) for h in range(DIFF_HEADS)], t)
    o = o1 - lam * o2
    osq = o * o
    ms = []
    for h in range(DIFF_HEADS):
        keep = (lane >= DIFF_V * h) & (lane < DIFF_V * (h + 1))
        ms.append(jnp.sum(jnp.where(keep, osq, 0.0), axis=-1, keepdims=True) * (1.0 / DIFF_V))
    on = o * lax.rsqrt(_head_lanes(ms, t) + 1e-5) * subln_ref[...]
    o_ref[0] = (on * (1.0 - lambda_init)).astype(BF16)


def _diff_attn(layer, safe, q, k, v, stats, group, bias_hi, bias_tiles, lq1, lk1, lq2, lk2, subln,
               lambda_init):
    b, s, _ = q.shape
    t = min(ATTN_TILE, s)
    return pl.pallas_call(
        functools.partial(_diff_attn_kernel, t=t, lambda_init=lambda_init),
        grid=(b, s // t),
        in_specs=[pl.BlockSpec(memory_space=pltpu.SMEM),
                  pl.BlockSpec((1, t, 256), lambda bi, qi: (bi, qi, 0)),
                  pl.BlockSpec((1, s, 256), lambda bi, qi: (bi, 0, 0)),
                  pl.BlockSpec((1, s, 256), lambda bi, qi: (bi, 0, 0)),
                  pl.BlockSpec((1, SUBLANES, LANES), lambda bi, qi: (bi, 0, 0)),
                  _const_spec(group.shape), _const_spec(bias_hi.shape),
                  _const_spec(bias_tiles.shape),
                  _layer_spec(lq1, layer), _layer_spec(lk1, layer),
                  _layer_spec(lq2, layer), _layer_spec(lk2, layer),
                  _layer_spec(subln, layer)],
        out_specs=pl.BlockSpec((1, t, 256), lambda bi, qi: (bi, qi, 0)),
        out_shape=jax.ShapeDtypeStruct((b, s, 256), BF16),
        scratch_shapes=[pltpu.VMEM((8 * t, 256), BF16),
                        pltpu.VMEM((2 * DIFF_HEADS, t, LANES), F32),
                        pltpu.VMEM((2 * DIFF_HEADS, t, LANES), F32),
                        pltpu.VMEM((2 * t, 256), F32),
                        pltpu.VMEM((8 * t, t), F32),
                        pltpu.VMEM((8 * t, t), F32)],
        compiler_params=pltpu.CompilerParams(
            dimension_semantics=("arbitrary", "arbitrary"),
            vmem_limit_bytes=VMEM_LIMIT_BYTES),
        name="diff_attn",
    )(safe, q, k, v, stats, group, bias_hi, bias_tiles, lq1, lk1, lq2, lk2, subln)


def _mem_kv_kernel(mem_ref, g_ref, wk_ref, wv_ref, k_ref, v_ref):
    m = mem_ref[0]
    mn = m * lax.rsqrt(jnp.mean(m * m, axis=-1, keepdims=True) + 1e-6) * g_ref[0]
    mb = mn.astype(BF16)
    k_ref[0, 0] = _dot(mb, wk_ref[0]).astype(BF16)
    v_ref[0, 0] = _dot(mb, wv_ref[0]).astype(BF16)


def _mem_kv(mem, g, wk, wv):
    b, m, d = mem.shape
    nl = wk.shape[0]
    out = jax.ShapeDtypeStruct((nl, b, m, d), BF16)
    return pl.pallas_call(
        _mem_kv_kernel,
        grid=(nl, b),
        in_specs=[pl.BlockSpec((1, m, d), lambda li, bi: (bi, 0, 0)),
                  pl.BlockSpec((1, 1, d), lambda li, bi: (li, 0, 0)),
                  pl.BlockSpec((1, d, d), lambda li, bi: (li, 0, 0)),
                  pl.BlockSpec((1, d, d), lambda li, bi: (li, 0, 0))],
        out_specs=[pl.BlockSpec((1, 1, m, d), lambda li, bi: (li, bi, 0, 0))] * 2,
        out_shape=[out, out],
        compiler_params=pltpu.CompilerParams(
            dimension_semantics=("arbitrary", "arbitrary"),
            vmem_limit_bytes=VMEM_LIMIT_BYTES),
        name="mem_kv",
    )(mem, g, wk, wv)


def _xa_kernel(x_ref, ya_ref, yb_ref, yc_ref, yd_ref, wout_ref, g_ref, wq_ref, km_ref, vm_ref,
               wo_ref, o_ref, obuf):
    x1 = x_ref[0]
    for i, y_ref in enumerate((ya_ref, yb_ref, yc_ref, yd_ref)):
        x1 = x1 + _dot(y_ref[0], wout_ref[256 * i:256 * (i + 1), :])
    hx = x1 * lax.rsqrt(jnp.mean(x1 * x1, axis=-1, keepdims=True) + 1e-6) * g_ref[...]
    hd = wq_ref.shape[1] // XA_HEADS
    q = (_dot(hx.astype(BF16), wq_ref[...]) * (hd ** -0.5)).astype(BF16)
    for h in range(XA_HEADS):
        cols = slice(hd * h, hd * (h + 1))
        sc = _dot_nt(q[:, cols], km_ref[0, :, cols])
        p = jnp.exp(sc - jnp.max(sc, axis=-1, keepdims=True))
        oh = _dot(p.astype(BF16), vm_ref[0, :, cols]) / jnp.sum(p, axis=-1, keepdims=True)
        obuf[:, cols] = oh.astype(BF16)
    o_ref[0] = x1 + _dot(obuf[...], wo_ref[...])


def _xa(layer, x, ya, yb, yc, yd, wout, g, wq, kmem, vmem, wo):
    b, s, d = x.shape
    tm = min(TOKEN_TILE, s)
    m = kmem.shape[2]
    tok = lambda w: pl.BlockSpec((1, tm, w), lambda bi, si: (bi, si, 0))
    mem_spec = pl.BlockSpec((None, 1, m, d), lambda bi, si: (layer, bi, 0, 0))
    return pl.pallas_call(
        _xa_kernel,
        grid=(b, s // tm),
        in_specs=[tok(d), tok(256), tok(256), tok(256), tok(256),
                  _layer_spec(wout, layer), _layer_spec(g, layer), _layer_spec(wq, layer),
                  mem_spec, mem_spec, _layer_spec(wo, layer)],
        out_specs=tok(d),
        out_shape=jax.ShapeDtypeStruct((b, s, d), F32),
        scratch_shapes=[pltpu.VMEM((tm, d), BF16)],
        compiler_params=pltpu.CompilerParams(
            dimension_semantics=("arbitrary", "arbitrary"),
            vmem_limit_bytes=VMEM_LIMIT_BYTES),
        name="xa",
    )(x, ya, yb, yc, yd, wout, g, wq, kmem, vmem, wo)


def _ffn_kernel(x_ref, g_ref, wup_ref, dw_ref, dwb_ref, wdown_ref, gfin_ref, o_ref,
                carry, acc_ref, perm_in, perm_out, *, tm, dff, final_norm):
    s = pl.program_id(1)
    groups = tm // SUBLANES

    @pl.when(s == 0)
    def _():
        carry[...] = jnp.zeros(carry.shape, F32)

    x = x_ref[0]
    h = x * lax.rsqrt(jnp.mean(x * x, axis=-1, keepdims=True) + 1e-6) * g_ref[...]
    hb = jnp.concatenate(_to_group_order(h, perm_in, groups), axis=0).astype(BF16)
    nchunk = dff // FFN_CHUNK
    first_sublane = lax.broadcasted_iota(jnp.int32, (SUBLANES, FFN_CHUNK), 0) == 0

    def conv(u, slot, col0):
        cols = slice(col0, col0 + FFN_CHUNK)
        wcols = slice(slot * dff + col0, slot * dff + col0 + FFN_CHUNK)
        tail = u[tm - 2 * SUBLANES:tm, :]
        prev = carry[slot, :, cols]
        carry[slot, :, cols] = tail

        def wrap(k):
            lo, hi = k * SUBLANES, (k + 1) * SUBLANES
            return jnp.where(first_sublane, pltpu.roll(prev[lo:hi, :], 1, 0),
                             pltpu.roll(tail[lo:hi, :], 1, 0))

        back1 = wrap(1)
        u1 = jnp.concatenate([back1, u[0:tm - SUBLANES, :]], axis=0)
        u2 = jnp.concatenate([wrap(0), back1, u[0:tm - 2 * SUBLANES, :]], axis=0)
        w = dw_ref[:, wcols]
        return dwb_ref[:, wcols] + w[2:3, :] * u + w[1:2, :] * u1 + w[0:1, :] * u2

    def up(j):
        col0 = j * FFN_CHUNK
        return (_dot(hb, wup_ref[:, col0:col0 + FFN_CHUNK]),
                _dot(hb, wup_ref[:, dff + col0:dff + col0 + FFN_CHUNK]))

    ua, ug = up(0)
    for j in range(nchunk):
        col0 = j * FFN_CHUNK
        nxt = up(j + 1) if j + 1 < nchunk else None
        a = conv(ua, 0, col0)
        gt = conv(ug, 1, col0)
        act = (gt * _sigmoid(gt) * a).astype(BF16)
        part = _dot(act, wdown_ref[col0:col0 + FFN_CHUNK, :])
        if j == 0:
            acc_ref[...] = part
        else:
            acc_ref[...] += part
        if nxt is not None:
            ua, ug = nxt
    y = x + _to_row_order(acc_ref[...], perm_out, groups)
    if final_norm:
        y = y * lax.rsqrt(jnp.mean(y * y, axis=-1, keepdims=True) + 1e-6) * gfin_ref[...]
    o_ref[0] = y


def _ffn(layer, x, g, wup, dw, dwb, wdown, gfin, final_norm):
    b, s, d = x.shape
    tm = min(TOKEN_TILE, s)
    dff = wdown.shape[1]
    tok = pl.BlockSpec((1, tm, d), lambda bi, si: (bi, si, 0))
    return pl.pallas_call(
        functools.partial(_ffn_kernel, tm=tm, dff=dff, final_norm=final_norm),
        grid=(b, s // tm),
        in_specs=[tok, _layer_spec(g, layer), _layer_spec(wup, layer), _layer_spec(dw, layer),
                  _layer_spec(dwb, layer), _layer_spec(wdown, layer), _const_spec(gfin.shape)],
        out_specs=tok,
        out_shape=jax.ShapeDtypeStruct((b, s, d), F32),
        scratch_shapes=[pltpu.VMEM((2, 2 * SUBLANES, dff), F32),
                        pltpu.VMEM((tm, d), F32),
                        pltpu.VMEM((d // LANES, tm + SUBLANES * SUBLANES, LANES), F32),
                        pltpu.VMEM((d // LANES, tm + SUBLANES * SUBLANES, LANES), F32)],
        compiler_params=pltpu.CompilerParams(
            dimension_semantics=("arbitrary", "arbitrary"),
            vmem_limit_bytes=VMEM_LIMIT_BYTES),
        name="ffn",
    )(x, g, wup, dw, dwb, wdown, gfin)


def _pack_w_in(w):
    ua, cq, ckv, kr, qc, kc, vc, ud = jnp.split(
        w, [512, 704, 832, 864, 1120, 1376, 1632], axis=-1)
    z = lambda n: jnp.zeros(w.shape[:-1] + (n,), w.dtype)
    half = MLA_ROPE // 2
    kr_sw = jnp.concatenate([kr[..., half:], kr[..., :half]], axis=-1)
    kr_blk = jnp.concatenate([z(MLA_NOPE), kr, z(LANES - MLA_NOPE - MLA_ROPE)], axis=-1)
    kr_sw_blk = jnp.concatenate([z(MLA_NOPE), kr_sw, z(LANES - MLA_NOPE - MLA_ROPE)], axis=-1)
    packed = jnp.concatenate(
        [ua, cq, z(256 - MLA_Q_RANK), ckv, kr_blk, kr_sw_blk, z(LANES), qc, kc, vc, ud], axis=-1)
    return packed.astype(BF16)


def _pack_w_uq(w):
    nl = w.shape[0]
    w = w.reshape(nl, MLA_Q_RANK, MLA_HEADS, MLA_NOPE + MLA_ROPE)
    nope, rp = w[..., :MLA_NOPE], w[..., MLA_NOPE:]
    half = MLA_ROPE // 2
    rp_sw = jnp.concatenate([rp[..., half:], rp[..., :half]], axis=-1)
    zn = jnp.zeros_like(nope)
    zp = jnp.zeros((nl, MLA_Q_RANK, MLA_HEADS, LANES - MLA_NOPE - MLA_ROPE), w.dtype)

    def fin(a):
        a = a.reshape(nl, MLA_Q_RANK, MLA_HEADS * LANES)
        return jnp.pad(a, ((0, 0), (0, 256 - MLA_Q_RANK), (0, 0))).astype(BF16)

    return (fin(jnp.concatenate([nope, rp, zp], axis=-1)),
            fin(jnp.concatenate([zn, rp_sw, zp], axis=-1)))


def _pack_w_ukv(w):
    nl = w.shape[0]
    w = w.reshape(nl, MLA_KV_RANK, MLA_HEADS, MLA_NOPE + MLA_V)
    kn, v = w[..., :MLA_NOPE], w[..., MLA_NOPE:]
    kn = jnp.concatenate([kn, jnp.zeros_like(kn)], axis=-1).reshape(nl, MLA_KV_RANK, MLA_HEADS * LANES)
    v = v.reshape(nl, MLA_KV_RANK, MLA_HEADS * MLA_V)
    return jnp.concatenate([kn, v], axis=-1).astype(BF16)


def _rope_tables(positions):
    inv_freq = ROPE_BASE ** (-jnp.arange(0, MLA_ROPE, 2, dtype=F32) / MLA_ROPE)
    ang = positions.astype(F32)[:, None] * inv_freq[None, :]
    cos, sin = jnp.cos(ang), jnp.sin(ang)
    s = positions.shape[0]
    pad = jnp.zeros((s, LANES - MLA_NOPE - MLA_ROPE), F32)
    ctab = jnp.concatenate([jnp.ones((s, MLA_NOPE), F32), cos, cos, pad], axis=1)
    stab = jnp.concatenate([jnp.zeros((s, MLA_NOPE), F32), -sin, sin, pad], axis=1)
    return ctab, stab


def _lane_groups(n_lanes, width):
    lane = jnp.arange(n_lanes)[:, None] // width
    return (lane == jnp.arange(LANES)[None, :]).astype(BF16)


def _bias_range(rel_bias):
    shifted = (rel_bias - rel_bias[REL_BUCKETS - 1]) * LOG2_E
    hi = jnp.repeat(jnp.max(shifted, axis=0), 2)
    lo = jnp.repeat(jnp.min(shifted, axis=0), 2)
    pad = (0, LANES - 2 * DIFF_HEADS)
    return jnp.pad(hi, pad).reshape(1, LANES), jnp.pad(hi - lo, pad).reshape(1, LANES)


def _fixed_stabiliser_ok(qmax_sq, kmax_sq, extra_span):
    bound = NORM_MARGIN * jnp.sqrt(qmax_sq * kmax_sq)
    ok = jnp.all(2.0 * bound + extra_span <= SAFE_LOGIT_SPAN)
    return ok.astype(jnp.int32).reshape(1)


def _block_diag(w):
    nl, g, c, _ = w.shape
    out = jnp.zeros((nl, g * c, g * c), w.dtype)
    for i in range(g):
        out = out.at[:, i * c:(i + 1) * c, i * c:(i + 1) * c].set(w[:, i])
    return out


def kernel(x, mem, positions, rel_bias, norm_mix, w_in, w_out, conv_dw, conv_dw_b, conv_ln_g, conv_ln_b, conv_pw, conv_pw_b, mla_q_norm, mla_w_uq, mla_kv_norm, mla_w_ukv, diff_lq1, diff_lk1, diff_lq2, diff_lk2, diff_subln, pool_w, pool_scale, norm_xa, mem_norm, xa_wq, xa_wk, xa_wv, xa_wo, norm_ffn, ffn_up, ffn_dw, ffn_dw_b, ffn_down, norm_final):
    depth = w_in.shape[0]
    s = x.shape[1]
    rows = lambda a: a[:, None, :]
    bf = lambda a: a.astype(BF16)
    ctab, stab = _rope_tables(positions)
    bias_tiles = _bias_tiles(rel_bias, min(ATTN_TILE, s))
    kmem, vmem = _mem_kv(mem, rows(mem_norm), bf(xa_wk), bf(xa_wv))
    group_mla = _lane_groups(MLA_HEADS * LANES, LANES)
    group_diff = _lane_groups(2 * DIFF_HEADS * DIFF_HALF, DIFF_HALF)
    bias_hi, bias_span = _bias_range(rel_bias)
    wuq, wuqs = _pack_w_uq(mla_w_uq)
    mix_params = [rows(norm_mix), _pack_w_in(w_in), conv_dw, rows(conv_dw_b), rows(conv_ln_g),
                  rows(conv_ln_b), bf(conv_pw), rows(conv_pw_b),
                  rows(jnp.pad(mla_q_norm, ((0, 0), (0, 256 - MLA_Q_RANK)))), wuq, wuqs,
                  rows(mla_kv_norm), _pack_w_ukv(mla_w_ukv),
                  bf(_block_diag(pool_w)), rows(pool_scale)]
    lq1, lk1, lq2, lk2 = rows(diff_lq1), rows(diff_lk1), rows(diff_lq2), rows(diff_lk2)
    subln = rows(jnp.tile(diff_subln, (1, DIFF_HEADS)))
    w_out, xa_wq, xa_wo, ffn_up, ffn_down = bf(w_out), bf(xa_wq), bf(xa_wo), bf(ffn_up), bf(ffn_down)
    g_xa, g_ffn, ffn_dw_b = rows(norm_xa), rows(norm_ffn), rows(ffn_dw_b)
    for l in range(depth):
        ya, qm, km, vm, qd, kd, vd, yd, stats = _mix_in(l, x, mix_params, ctab, stab, group_mla,
                                                         group_diff)
        stats = jnp.max(stats, axis=1)
        yb = _mla_attn(_fixed_stabiliser_ok(stats[:, 0], stats[:, 1], 0.0), qm, km, vm, stats,
                       group_mla)
        lambda_init = 0.8 - 0.6 * math.exp(-0.3 * l)
        yc = _diff_attn(l, _fixed_stabiliser_ok(stats[:, 2], stats[:, 3], bias_span), qd, kd, vd,
                        stats, group_diff, bias_hi, bias_tiles, lq1, lk1, lq2, lk2, subln,
                        lambda_init)
        x = _xa(l, x, ya, yb, yc, yd, w_out, g_xa, xa_wq, kmem, vmem, xa_wo)
        x = _ffn(l, x, g_ffn, ffn_up, ffn_dw, ffn_dw_b, ffn_down, norm_final.reshape(1, -1),
                 l == depth - 1)
    return x
```

```python
import functools
import math

import jax
import jax.numpy as jnp
from jax import lax
from jax.experimental import pallas as pl
from jax.experimental.pallas import tpu as pltpu

N_GROUPS = 4
CONV_KERNEL = 31
MLA_HEADS = 4
MLA_NOPE = 64
MLA_ROPE = 32
MLA_V = 64
MLA_Q_RANK = 192
MLA_KV_RANK = 128
DIFF_HEADS = 4
DIFF_HALF = 32
DIFF_V = 64
POOL_WINDOWS = (2, 4, 8, 16)
POOL_CH = 64
REL_BUCKETS = 32
REL_MAX_DIST = 128
XA_HEADS = 4
FFN_CONV = 3
ROPE_BASE = 10000.0
NEG_INF = -1e30
LOG2_E = math.log2(math.e)
NORM_MARGIN = 1.02
SAFE_LOGIT_SPAN = 100.0

LANES = 128
SUBLANES = 8
VMEM_LIMIT_BYTES = 56 * 1024 * 1024

TOKEN_TILE = 512
DIFF_TILE = 512
MLA_TILE = 1024
MLA_FAR_KEYS = 1024
CONV_HALO = 32
POOL_HALO = 16
ROW_CHUNK = 64
FFN_CHUNK = 1024

BF16 = jnp.bfloat16
F32 = jnp.float32


def _dot(a, b):
    return jnp.dot(a, b, preferred_element_type=F32)


def _dot_nt(a, b):
    return lax.dot_general(a, b, (((1,), (1,)), ((), ())), preferred_element_type=F32)


def _sigmoid(x):
    return 1.0 / (1.0 + jnp.exp(-x))


def _const_spec(shape):
    nd = len(shape)
    return pl.BlockSpec(shape, lambda *_: (0,) * nd)


def _layer_spec(stacked, layer):
    tail = stacked.shape[1:]
    return pl.BlockSpec((None,) + tail, lambda *_: (layer,) + (0,) * len(tail))


_A0, _A1 = 0, 512
_B0, _B1 = 512, 1280
_C0, _C1 = 1280, 2048
_D0, _D1 = 2048, 2304
W_IN_COLS = 2304


def _to_group_order(val, perm_ref, groups):
    ntile = val.shape[1] // LANES
    pitch = groups + SUBLANES // 2
    for c in range(ntile):
        for i in range(SUBLANES):
            perm_ref[c, i * pitch:i * pitch + groups, :] = val[i * groups:(i + 1) * groups,
                                                                 c * LANES:(c + 1) * LANES]
    return [jnp.concatenate([perm_ref[c, pl.ds(j, SUBLANES, stride=pitch), :]
                             for c in range(ntile)], axis=1) for j in range(groups)]


def _to_row_order(val, perm_ref, groups):
    ntile = val.shape[1] // LANES
    pitch = groups + SUBLANES // 2
    for j in range(groups):
        for c in range(ntile):
            perm_ref[c, pl.ds(j, SUBLANES, stride=pitch), :] = val[
                j * SUBLANES:(j + 1) * SUBLANES, c * LANES:(c + 1) * LANES]
    return jnp.concatenate(
        [jnp.concatenate([perm_ref[c, i * pitch:i * pitch + groups, :] for i in range(SUBLANES)],
                         axis=0) for c in range(ntile)], axis=1)


def _fill_window(grps, ext_ref, prev_ref, halo):
    groups = len(grps)
    first_sublane = lax.broadcasted_iota(jnp.int32, grps[0].shape, 0) == 0
    for m in range(groups - halo, groups):
        rows = slice((m - groups + halo) * SUBLANES, (m - groups + halo + 1) * SUBLANES)
        ext_ref[rows, :] = jnp.where(first_sublane, pltpu.roll(prev_ref[rows, :], 1, 0),
                                     pltpu.roll(grps[m], 1, 0))
        prev_ref[rows, :] = grps[m]
    for j in range(groups):
        ext_ref[(halo + j) * SUBLANES:(halo + j + 1) * SUBLANES, :] = grps[j]


def _mix_in_kernel(x_ref, g_ref, win_ref, dw_ref, dwb_ref, lng_ref, lnb_ref, pw_ref, pwb_ref,
                   qn_ref, wuq_ref, wuqs_ref, kvn_ref, wukv_ref, ct_ref, st_ref,
                   poolw_ref, pools_ref, gm_ref, gd_ref,
                   ya_ref, qm_ref, km_ref, vm_ref, qd_ref, kd_ref, vd_ref, yd_ref, stats_ref,
                   ext_a, prev_a, zbuf, ext_d, prev_d, dbuf, perm_a, perm_ya, perm_d, perm_yd, *, tm):
    s = pl.program_id(1)
    groups = tm // SUBLANES

    def max_sq_norm(v, group_ref):
        vf = v.astype(F32)
        return jnp.max(_dot((vf * vf).astype(BF16), group_ref[...]), axis=0, keepdims=True)

    @pl.when(s == 0)
    def _():
        prev_a[...] = jnp.zeros(prev_a.shape, F32)
        prev_d[...] = jnp.zeros(prev_d.shape, F32)

    x = x_ref[0]
    h = x * lax.rsqrt(jnp.mean(x * x, axis=-1, keepdims=True) + 1e-6) * g_ref[...]
    hb = h.astype(BF16)

    pa = _dot(hb, win_ref[:, _A0:_A1])
    pb = _dot(hb, win_ref[:, _B0:_B1])
    pc = _dot(hb, win_ref[:, _C0:_C1])
    ud = _dot(hb, win_ref[:, _D0:_D1])

    glu = pa[:, 0:256] * _sigmoid(pa[:, 256:512])
    _fill_window(_to_group_order(glu, perm_a, groups), ext_a, prev_a, CONV_HALO)
    for c in range(tm // ROW_CHUNK):
        r0 = c * ROW_CHUNK
        acc = jnp.broadcast_to(dwb_ref[...], (ROW_CHUNK, 256))
        for k in range(CONV_KERNEL):
            off = r0 + (CONV_HALO - (CONV_KERNEL - 1) + k) * SUBLANES
            acc = acc + dw_ref[k:k + 1, :] * ext_a[off:off + ROW_CHUNK, :]
        mu = jnp.mean(acc, axis=-1, keepdims=True)
        cen = acc - mu
        var = jnp.mean(cen * cen, axis=-1, keepdims=True)
        yn = cen * lax.rsqrt(var + 1e-5) * lng_ref[...] + lnb_ref[...]
        zbuf[r0:r0 + ROW_CHUNK, :] = (yn * _sigmoid(yn)).astype(BF16)
    ya = _to_row_order(_dot(zbuf[...], pw_ref[...]), perm_ya, groups)
    ya_ref[0] = (ya + pwb_ref[...]).astype(BF16)

    ct = ct_ref[...]
    st = st_ref[...]
    cq = pb[:, 0:256]
    cqn = cq * lax.rsqrt(jnp.sum(cq * cq, axis=-1, keepdims=True) * (1.0 / MLA_Q_RANK) + 1e-6)
    cqb = (cqn * qn_ref[...]).astype(BF16)
    q = _dot(cqb, wuq_ref[...])
    qs = _dot(cqb, wuqs_ref[...])
    ct4 = jnp.concatenate([ct] * MLA_HEADS, axis=1)
    st4 = jnp.concatenate([st] * MLA_HEADS, axis=1)
    q_scale = LOG2_E * (MLA_NOPE + MLA_ROPE) ** -0.5
    qm = ((q * ct4 + qs * st4) * q_scale).astype(BF16)
    qm_ref[0] = qm
    ckv = pb[:, 256:384]
    ckvn = ckv * lax.rsqrt(jnp.mean(ckv * ckv, axis=-1, keepdims=True) + 1e-6)
    kv = _dot((ckvn * kvn_ref[...]).astype(BF16), wukv_ref[...])
    kr = pb[:, 384:512] * ct + pb[:, 512:640] * st
    km = (kv[:, 0:512] + jnp.concatenate([kr] * MLA_HEADS, axis=1)).astype(BF16)
    km_ref[0] = km
    vm_ref[0] = kv[:, 512:768].astype(BF16)

    qd = (pc[:, 0:256] * (LOG2_E * DIFF_HALF ** -0.5)).astype(BF16)
    kd = pc[:, 256:512].astype(BF16)
    qd_ref[0] = qd
    kd_ref[0] = kd
    stats_ref[0, 0] = jnp.concatenate(
        [max_sq_norm(qm, gm_ref), max_sq_norm(km, gm_ref),
         max_sq_norm(qd, gd_ref), max_sq_norm(kd, gd_ref),
         jnp.zeros((SUBLANES - 4, LANES), F32)], axis=0)
    vd_ref[0] = pc[:, 512:768].astype(BF16)

    _fill_window(_to_group_order(ud, perm_d, groups), ext_d, prev_d, POOL_HALO)
    lane = lax.broadcasted_iota(jnp.int32, (ROW_CHUNK, 256), 1)
    grp = lane // POOL_CH
    win_len = jnp.where(grp == 0, POOL_WINDOWS[0],
                        jnp.where(grp == 1, POOL_WINDOWS[1],
                                  jnp.where(grp == 2, POOL_WINDOWS[2], POOL_WINDOWS[3])))
    for c in range(tm // ROW_CHUNK):
        r0 = c * ROW_CHUNK
        base = r0 + POOL_HALO * SUBLANES
        u = ext_d[base:base + ROW_CHUNK, :]
        run = u
        sums = {}
        for j in range(1, POOL_WINDOWS[-1]):
            run = run + ext_d[base - j * SUBLANES:base - j * SUBLANES + ROW_CHUNK, :]
            if j + 1 in POOL_WINDOWS:
                sums[j + 1] = run
        win = jnp.where(grp == 0, sums[2],
                        jnp.where(grp == 1, sums[4], jnp.where(grp == 2, sums[8], sums[16])))
        r = r0 + lax.broadcasted_iota(jnp.int32, (ROW_CHUNK, 256), 0)
        t_glob = s * tm + (r % SUBLANES) * groups + r // SUBLANES
        cnt = jnp.minimum(t_glob + 1, win_len).astype(F32)
        dbuf[r0:r0 + ROW_CHUNK, :] = (win / cnt - u).astype(BF16)
    yd = _to_row_order(_dot(dbuf[...], poolw_ref[...]), perm_yd, groups)
    yd_ref[0] = (yd * pools_ref[...]).astype(BF16)


def _mix_in(layer, x, per_layer, ctab, stab, group_mla, group_diff):
    b, s, d = x.shape
    tm = min(TOKEN_TILE, s)
    tok = lambda w: pl.BlockSpec((1, tm, w), lambda bi, si: (bi, si, 0))
    head, tail = per_layer[:13], per_layer[13:]
    in_specs = ([tok(d)] + [_layer_spec(a, layer) for a in head]
                + [pl.BlockSpec((tm, LANES), lambda bi, si: (si, 0))] * 2
                + [_layer_spec(a, layer) for a in tail]
                + [_const_spec(group_mla.shape), _const_spec(group_diff.shape)])
    widths = (256, 512, 512, 256, 256, 256, 256, 256)
    stats_spec = pl.BlockSpec((1, 1, SUBLANES, LANES), lambda bi, si: (bi, si, 0, 0))
    return pl.pallas_call(
        functools.partial(_mix_in_kernel, tm=tm),
        grid=(b, s // tm),
        in_specs=in_specs,
        out_specs=[tok(w) for w in widths] + [stats_spec],
        out_shape=([jax.ShapeDtypeStruct((b, s, w), BF16) for w in widths]
                   + [jax.ShapeDtypeStruct((b, s // tm, SUBLANES, LANES), F32)]),
        scratch_shapes=[pltpu.VMEM((CONV_HALO * SUBLANES + tm, 256), F32),
                        pltpu.VMEM((CONV_HALO * SUBLANES, 256), F32),
                        pltpu.VMEM((tm, 256), BF16),
                        pltpu.VMEM((POOL_HALO * SUBLANES + tm, 256), F32),
                        pltpu.VMEM((POOL_HALO * SUBLANES, 256), F32),
                        pltpu.VMEM((tm, 256), BF16)]
                       + [pltpu.VMEM((2, tm + SUBLANES * SUBLANES, LANES), F32)] * 4,
        compiler_params=pltpu.CompilerParams(
            dimension_semantics=("arbitrary", "arbitrary"),
            vmem_limit_bytes=VMEM_LIMIT_BYTES),
        name="mix_in",
    )(x, *head, ctab, stab, *tail, group_mla, group_diff)


def _head_lanes(vals, rows):
    lane = lax.broadcasted_iota(jnp.int32, (rows, LANES), 1)
    return jnp.concatenate([jnp.where(lane < 64, vals[0], vals[1]),
                            jnp.where(lane < 64, vals[2], vals[3])], axis=1)


def _row_total(l_ref, idx):
    return jnp.sum(l_ref[idx], axis=-1, keepdims=True)


def _stack_masked_v(v_blk):
    n = v_blk.shape[0]
    lane = lax.broadcasted_iota(jnp.int32, (n, 256), 1)
    parts = []
    for h in range(4):
        keep = (lane >= 64 * h) & (lane < 64 * (h + 1))
        parts.append(jnp.where(keep, v_blk, jnp.zeros_like(v_blk)))
    return jnp.concatenate(parts, axis=0)


def _softmax_step(sc, m_ref, l_ref, idx, online):
    if online:
        m_old = m_ref[idx]
        m_new = jnp.maximum(m_old, jnp.max(sc, axis=-1, keepdims=True))
        alpha = jnp.exp2(m_old - m_new)
        m_ref[idx] = m_new
    else:
        m_new = m_ref[idx]
        alpha = None
    reps = sc.shape[1] // LANES
    p = jnp.exp2(sc - jnp.concatenate([m_new] * reps, axis=1))
    psum = p[:, 0:LANES]
    for r in range(1, reps):
        psum = psum + p[:, r * LANES:(r + 1) * LANES]
    l_ref[idx] = (alpha * l_ref[idx] if online else l_ref[idx]) + psum
    return p, alpha


def _logit_bounds(q, group_ref, kmax_sq):
    qf = q.astype(F32)
    qn_sq = _dot((qf * qf).astype(BF16), group_ref[...])
    return NORM_MARGIN * jnp.sqrt(qn_sq * kmax_sq)


def _mla_attn_kernel(safe_ref, q_ref, k_ref, v_ref, stats_ref, group_ref, o_ref,
                     m_ref, l_ref, acc_ref, *, t, far):
    qi = pl.program_id(1)
    l_ref[...] = jnp.zeros(l_ref.shape, F32)
    acc_ref[...] = jnp.zeros(acc_ref.shape, F32)

    def run(online):
        if online:
            m_ref[...] = jnp.full(m_ref.shape, NEG_INF, F32)
        else:
            bounds = _logit_bounds(q_ref[0], group_ref, stats_ref[0, 1:2, :])
            for h in range(MLA_HEADS):
                m_ref[h] = jnp.broadcast_to(bounds[:, h:h + 1], (t, LANES))

        def step(k0, n, diagonal):
            vst = _stack_masked_v(v_ref[0, pl.ds(k0, n), :])
            probs, alphas = [], []
            for h in range(MLA_HEADS):
                cols = slice(LANES * h, LANES * (h + 1))
                sc = _dot_nt(q_ref[0, :, cols], k_ref[0, pl.ds(k0, n), cols])
                if diagonal:
                    row = lax.broadcasted_iota(jnp.int32, (t, n), 0)
                    col = lax.broadcasted_iota(jnp.int32, (t, n), 1)
                    sc = jnp.where(row >= col, sc, NEG_INF)
                p, alpha = _softmax_step(sc, m_ref, l_ref, h, online)
                probs.append(p.astype(BF16))
                alphas.append(alpha)
            pv = _dot(jnp.concatenate(probs, axis=1), vst)
            acc_ref[...] = (acc_ref[...] * _head_lanes(alphas, t) if online else acc_ref[...]) + pv

        def far_body(j, carry):
            step(pl.multiple_of(j * far, far), far, False)
            return carry

        def near_body(j, carry):
            step(pl.multiple_of(j * t, t), t, False)
            return carry

        per_far = far // t
        n_far = qi // per_far
        lax.fori_loop(0, n_far, far_body, 0)
        lax.fori_loop(n_far * per_far, qi, near_body, 0)
        step(pl.multiple_of(qi * t, t), t, True)

    safe = safe_ref[0] != 0
    pl.when(safe)(functools.partial(run, False))
    pl.when(safe_ref[0] == 0)(functools.partial(run, True))
    inv_l = _head_lanes([1.0 / _row_total(l_ref, h) for h in range(MLA_HEADS)], t)
    o_ref[0] = (acc_ref[...] * inv_l).astype(BF16)


def _mla_attn(safe, q, k, v, stats, group):
    b, s, _ = q.shape
    t = min(MLA_TILE, s)
    far = min(MLA_FAR_KEYS, s)
    return pl.pallas_call(
        functools.partial(_mla_attn_kernel, t=t, far=far),
        grid=(b, s // t),
        in_specs=[pl.BlockSpec(memory_space=pltpu.SMEM),
                  pl.BlockSpec((1, t, 512), lambda bi, qi: (bi, qi, 0)),
                  pl.BlockSpec((1, s, 512), lambda bi, qi: (bi, 0, 0)),
                  pl.BlockSpec((1, s, 256), lambda bi, qi: (bi, 0, 0)),
                  pl.BlockSpec((1, SUBLANES, LANES), lambda bi, qi: (bi, 0, 0)),
                  _const_spec(group.shape)],
        out_specs=pl.BlockSpec((1, t, 256), lambda bi, qi: (bi, qi, 0)),
        out_shape=jax.ShapeDtypeStruct((b, s, 256), BF16),
        scratch_shapes=[pltpu.VMEM((MLA_HEADS, t, LANES), F32),
                        pltpu.VMEM((MLA_HEADS, t, LANES), F32),
                        pltpu.VMEM((t, 256), F32)],
        compiler_params=pltpu.CompilerParams(
            dimension_semantics=("arbitrary", "arbitrary"),
            vmem_limit_bytes=VMEM_LIMIT_BYTES),
        name="mla_attn",
    )(safe, q, k, v, stats, group)


def _bias_tiles_kernel(rb_ref, out_ref, *, t, tk):
    row = lax.broadcasted_iota(jnp.int32, (t, tk), 0)
    col = lax.broadcasted_iota(jnp.int32, (t, tk), 1)
    max_exact = REL_BUCKETS // 2
    for j in range(t // tk + 1):
        rel = (1 - j) * tk + row - col
        n = jnp.maximum(rel, 0)
        nf = jnp.maximum(n, 1).astype(F32)
        large = max_exact + (jnp.log(nf / max_exact) / math.log(REL_MAX_DIST / max_exact)
                             * (REL_BUCKETS - max_exact)).astype(jnp.int32)
        large = jnp.minimum(large, REL_BUCKETS - 1)
        bucket = jnp.where(n < max_exact, n, large)
        for h in range(DIFF_HEADS):
            far = rb_ref[REL_BUCKETS - 1, h]
            val = jnp.zeros((t, tk), F32)
            for bkt in range(REL_BUCKETS - 1):
                val = jnp.where(bucket == bkt, (rb_ref[bkt, h] - far) * LOG2_E, val)
            out_ref[h, j] = jnp.where(rel >= 0, val, NEG_INF)


def _bias_tiles(rel_bias, t, tk):
    return pl.pallas_call(
        functools.partial(_bias_tiles_kernel, t=t, tk=tk),
        in_specs=[pl.BlockSpec(memory_space=pltpu.SMEM)],
        out_specs=pl.BlockSpec(memory_space=pltpu.VMEM),
        out_shape=jax.ShapeDtypeStruct((DIFF_HEADS, t // tk + 1, t, tk), F32),
        name="bias_tiles",
    )(rel_bias)


def _diff_attn_kernel(safe_ref, q_ref, k_ref, v_ref, stats_ref, group_ref, bias_hi_ref, bias_ref,
                      lq1_ref, lk1_ref, lq2_ref, lk2_ref, subln_ref, o_ref,
                      qst_ref, m_ref, l_ref, acc_ref, s0_ref, s1_ref, *, t, tk, lambda_init):
    qi = pl.program_id(1)
    nmaps = 2 * DIFF_HEADS
    per_tile = t // tk
    l_ref[...] = jnp.zeros(l_ref.shape, F32)
    acc_ref[...] = jnp.zeros(acc_ref.shape, F32)
    q = q_ref[0]
    lane = lax.broadcasted_iota(jnp.int32, (t, 256), 1)
    for idx in range(nmaps):
        keep = (lane >= DIFF_HALF * idx) & (lane < DIFF_HALF * (idx + 1))
        qst_ref[idx * t:(idx + 1) * t, :] = jnp.where(keep, q, jnp.zeros_like(q))

    def scores(kb):
        return _dot_nt(qst_ref[...], k_ref[0, pl.ds(pl.multiple_of(kb * tk, tk), tk), :])

    def consume(sc_of, kb, bias_tile, online):
        vst = _stack_masked_v(v_ref[0, pl.ds(pl.multiple_of(kb * tk, tk), tk), :])
        probs = [[], []]
        alphas = [[], []]
        for h in range(DIFF_HEADS):
            for which in range(2):
                idx = 2 * h + which
                sc = sc_of(idx)
                if bias_tile is not None:
                    sc = sc + bias_ref[h, bias_tile]
                p, alpha = _softmax_step(sc, m_ref, l_ref, idx, online)
                probs[which].append(p.astype(BF16))
                alphas[which].append(alpha)
        p_all = jnp.concatenate([jnp.concatenate(probs[0], axis=1),
                                 jnp.concatenate(probs[1], axis=1)], axis=0)
        pv = _dot(p_all, vst)
        for which in range(2):
            rows = slice(which * t, (which + 1) * t)
            old = acc_ref[rows, :] * _head_lanes(alphas[which], t) if online else acc_ref[rows, :]
            acc_ref[rows, :] = old + pv[rows, :]

    def from_ref(ref):
        return lambda idx: ref[idx * t:(idx + 1) * t, :]

    def from_val(sc_all):
        return lambda idx: sc_all[idx * t:(idx + 1) * t, :]

    first_near = per_tile * qi - 1
    n_plain = jnp.maximum(first_near, 0)

    def run_online():
        m_ref[...] = jnp.full(m_ref.shape, NEG_INF, F32)

        def step(kb, bias_tile):
            consume(from_val(scores(kb)), kb, bias_tile, True)

        def body(kb, carry):
            step(kb, None)
            return carry

        lax.fori_loop(0, n_plain, body, 0)
        pl.when(qi > 0)(lambda: step(first_near, 0))
        for j in range(per_tile):
            step(first_near + 1 + j, 1 + j)

    def run_fixed():
        bounds = _logit_bounds(q, group_ref, stats_ref[0, 3:4, :]) + bias_hi_ref[...]
        for idx in range(nmaps):
            m_ref[idx] = jnp.broadcast_to(bounds[:, idx:idx + 1], (t, LANES))

        def pair_body(i, carry):
            s1_ref[...] = scores(2 * i + 1)
            consume(from_ref(s0_ref), 2 * i, None, False)
            s0_ref[...] = scores(2 * i + 2)
            consume(from_ref(s1_ref), 2 * i + 1, None, False)
            return carry

        s0_ref[...] = scores(0)
        n_pairs = n_plain // 2
        lax.fori_loop(0, n_pairs, pair_body, 0)

        @pl.when(qi == 0)
        def _():
            sc1 = scores(1)
            consume(from_ref(s0_ref), 0, 1, False)
            consume(from_val(sc1), 1, 2, False)

        @pl.when(qi > 0)
        def _():
            b0 = 2 * n_pairs
            s1_ref[...] = scores(b0 + 1)
            consume(from_ref(s0_ref), b0, None, False)
            sc2 = scores(b0 + 2)
            consume(from_ref(s1_ref), b0 + 1, 0, False)
            sc3 = scores(b0 + 3)
            consume(from_val(sc2), b0 + 2, 1, False)
            consume(from_val(sc3), b0 + 3, 2, False)

    pl.when(safe_ref[0] != 0)(run_fixed)
    pl.when(safe_ref[0] == 0)(run_online)

    lam = (jnp.exp(jnp.sum(lq1_ref[...] * lk1_ref[...], axis=-1, keepdims=True))
           - jnp.exp(jnp.sum(lq2_ref[...] * lk2_ref[...], axis=-1, keepdims=True))
           + lambda_init)
    o1 = acc_ref[0:t, :] * _head_lanes([1.0 / _row_total(l_ref, 2 * h) for h in range(DIFF_HEADS)], t)
    o2 = acc_ref[t:2 * t, :] * _head_lanes(
        [1.0 / _row_total(l_ref, 2 * h + 1) for h in range(DIFF_HEADS)], t)
    o = o1 - lam * o2
    osq = o * o
    ms = []
    for h in range(DIFF_HEADS):
        keep = (lane >= DIFF_V * h) & (lane < DIFF_V * (h + 1))
        ms.append(jnp.sum(jnp.where(keep, osq, 0.0), axis=-1, keepdims=True) * (1.0 / DIFF_V))
    on = o * lax.rsqrt(_head_lanes(ms, t) + 1e-5) * subln_ref[...]
    o_ref[0] = (on * (1.0 - lambda_init)).astype(BF16)


def _diff_attn(layer, safe, q, k, v, stats, group, bias_hi, bias_tiles, lq1, lk1, lq2, lk2, subln,
               lambda_init):
    b, s, _ = q.shape
    t = min(DIFF_TILE, s)
    tk = t // 2
    return pl.pallas_call(
        functools.partial(_diff_attn_kernel, t=t, tk=tk, lambda_init=lambda_init),
        grid=(b, s // t),
        in_specs=[pl.BlockSpec(memory_space=pltpu.SMEM),
                  pl.BlockSpec((1, t, 256), lambda bi, qi: (bi, qi, 0)),
                  pl.BlockSpec((1, s, 256), lambda bi, qi: (bi, 0, 0)),
                  pl.BlockSpec((1, s, 256), lambda bi, qi: (bi, 0, 0)),
                  pl.BlockSpec((1, SUBLANES, LANES), lambda bi, qi: (bi, 0, 0)),
                  _const_spec(group.shape), _const_spec(bias_hi.shape),
                  _const_spec(bias_tiles.shape),
                  _layer_spec(lq1, layer), _layer_spec(lk1, layer),
                  _layer_spec(lq2, layer), _layer_spec(lk2, layer),
                  _layer_spec(subln, layer)],
        out_specs=pl.BlockSpec((1, t, 256), lambda bi, qi: (bi, qi, 0)),
        out_shape=jax.ShapeDtypeStruct((b, s, 256), BF16),
        scratch_shapes=[pltpu.VMEM((8 * t, 256), BF16),
                        pltpu.VMEM((2 * DIFF_HEADS, t, LANES), F32),
                        pltpu.VMEM((2 * DIFF_HEADS, t, LANES), F32),
                        pltpu.VMEM((2 * t, 256), F32),
                        pltpu.VMEM((8 * t, tk), F32),
                        pltpu.VMEM((8 * t, tk), F32)],
        compiler_params=pltpu.CompilerParams(
            dimension_semantics=("arbitrary", "arbitrary"),
            vmem_limit_bytes=VMEM_LIMIT_BYTES),
        name="diff_attn",
    )(safe, q, k, v, stats, group, bias_hi, bias_tiles, lq1, lk1, lq2, lk2, subln)


def _mem_kv_kernel(mem_ref, g_ref, wk_ref, wv_ref, k_ref, v_ref):
    m = mem_ref[0]
    mn = m * lax.rsqrt(jnp.mean(m * m, axis=-1, keepdims=True) + 1e-6) * g_ref[0]
    mb = mn.astype(BF16)
    k_ref[0, 0] = _dot(mb, wk_ref[0]).astype(BF16)
    v_ref[0, 0] = _dot(mb, wv_ref[0]).astype(BF16)


def _mem_kv(mem, g, wk, wv):
    b, m, d = mem.shape
    nl = wk.shape[0]
    out = jax.ShapeDtypeStruct((nl, b, m, d), BF16)
    return pl.pallas_call(
        _mem_kv_kernel,
        grid=(nl, b),
        in_specs=[pl.BlockSpec((1, m, d), lambda li, bi: (bi, 0, 0)),
                  pl.BlockSpec((1, 1, d), lambda li, bi: (li, 0, 0)),
                  pl.BlockSpec((1, d, d), lambda li, bi: (li, 0, 0)),
                  pl.BlockSpec((1, d, d), lambda li, bi: (li, 0, 0))],
        out_specs=[pl.BlockSpec((1, 1, m, d), lambda li, bi: (li, bi, 0, 0))] * 2,
        out_shape=[out, out],
        compiler_params=pltpu.CompilerParams(
            dimension_semantics=("arbitrary", "arbitrary"),
            vmem_limit_bytes=VMEM_LIMIT_BYTES),
        name="mem_kv",
    )(mem, g, wk, wv)


def _xa_kernel(x_ref, ya_ref, yb_ref, yc_ref, yd_ref, wout_ref, g_ref, wq_ref, km_ref, vm_ref,
               wo_ref, o_ref, obuf):
    x1 = x_ref[0]
    for i, y_ref in enumerate((ya_ref, yb_ref, yc_ref, yd_ref)):
        x1 = x1 + _dot(y_ref[0], wout_ref[256 * i:256 * (i + 1), :])
    hx = x1 * lax.rsqrt(jnp.mean(x1 * x1, axis=-1, keepdims=True) + 1e-6) * g_ref[...]
    hd = wq_ref.shape[1] // XA_HEADS
    q = (_dot(hx.astype(BF16), wq_ref[...]) * (hd ** -0.5)).astype(BF16)
    for h in range(XA_HEADS):
        cols = slice(hd * h, hd * (h + 1))
        sc = _dot_nt(q[:, cols], km_ref[0, :, cols])
        p = jnp.exp(sc - jnp.max(sc, axis=-1, keepdims=True))
        oh = _dot(p.astype(BF16), vm_ref[0, :, cols]) / jnp.sum(p, axis=-1, keepdims=True)
        obuf[:, cols] = oh.astype(BF16)
    o_ref[0] = x1 + _dot(obuf[...], wo_ref[...])


def _xa(layer, x, ya, yb, yc, yd, wout, g, wq, kmem, vmem, wo):
    b, s, d = x.shape
    tm = min(TOKEN_TILE, s)
    m = kmem.shape[2]
    tok = lambda w: pl.BlockSpec((1, tm, w), lambda bi, si: (bi, si, 0))
    mem_spec = pl.BlockSpec((None, 1, m, d), lambda bi, si: (layer, bi, 0, 0))
    return pl.pallas_call(
        _xa_kernel,
        grid=(b, s // tm),
        in_specs=[tok(d), tok(256), tok(256), tok(256), tok(256),
                  _layer_spec(wout, layer), _layer_spec(g, layer), _layer_spec(wq, layer),
                  mem_spec, mem_spec, _layer_spec(wo, layer)],
        out_specs=tok(d),
        out_shape=jax.ShapeDtypeStruct((b, s, d), F32),
        scratch_shapes=[pltpu.VMEM((tm, d), BF16)],
        compiler_params=pltpu.CompilerParams(
            dimension_semantics=("arbitrary", "arbitrary"),
            vmem_limit_bytes=VMEM_LIMIT_BYTES),
        name="xa",
    )(x, ya, yb, yc, yd, wout, g, wq, kmem, vmem, wo)


def _ffn_kernel(x_ref, g_ref, wup_ref, dw_ref, dwb_ref, wdown_ref, gfin_ref, o_ref,
                carry, acc_ref, perm_in, perm_out, *, tm, dff, final_norm):
    s = pl.program_id(1)
    groups = tm // SUBLANES

    @pl.when(s == 0)
    def _():
        carry[...] = jnp.zeros(carry.shape, F32)

    x = x_ref[0]
    h = x * lax.rsqrt(jnp.mean(x * x, axis=-1, keepdims=True) + 1e-6) * g_ref[...]
    hb = jnp.concatenate(_to_group_order(h, perm_in, groups), axis=0).astype(BF16)
    bounds = list(range(0, dff, FFN_CHUNK)) + [dff]
    nchunk = len(bounds) - 1

    def conv(u, slot, col0):
        width = u.shape[1]
        first_sublane = lax.broadcasted_iota(jnp.int32, (SUBLANES, width), 0) == 0
        cols = slice(col0, col0 + width)
        wcols = slice(slot * dff + col0, slot * dff + col0 + width)
        tail = u[tm - 2 * SUBLANES:tm, :]
        prev = carry[slot, :, cols]
        carry[slot, :, cols] = tail

        def wrap(k):
            lo, hi = k * SUBLANES, (k + 1) * SUBLANES
            return jnp.where(first_sublane, pltpu.roll(prev[lo:hi, :], 1, 0),
                             pltpu.roll(tail[lo:hi, :], 1, 0))

        back1 = wrap(1)
        u1 = jnp.concatenate([back1, u[0:tm - SUBLANES, :]], axis=0)
        u2 = jnp.concatenate([wrap(0), back1, u[0:tm - 2 * SUBLANES, :]], axis=0)
        w = dw_ref[:, wcols]
        return dwb_ref[:, wcols] + w[2:3, :] * u + w[1:2, :] * u1 + w[0:1, :] * u2

    def up(j):
        c0, c1 = bounds[j], bounds[j + 1]
        return (_dot(hb, wup_ref[:, c0:c1]), _dot(hb, wup_ref[:, dff + c0:dff + c1]))

    ua, ug = up(0)
    for j in range(nchunk):
        col0 = bounds[j]
        nxt = up(j + 1) if j + 1 < nchunk else None
        a = conv(ua, 0, col0)
        gt = conv(ug, 1, col0)
        act = (gt * _sigmoid(gt) * a).astype(BF16)
        part = _dot(act, wdown_ref[bounds[j]:bounds[j + 1], :])
        if j == 0:
            acc_ref[...] = part
        else:
            acc_ref[...] += part
        if nxt is not None:
            ua, ug = nxt
    y = x + _to_row_order(acc_ref[...], perm_out, groups)
    if final_norm:
        y = y * lax.rsqrt(jnp.mean(y * y, axis=-1, keepdims=True) + 1e-6) * gfin_ref[...]
    o_ref[0] = y


def _ffn(layer, x, g, wup, dw, dwb, wdown, gfin, final_norm):
    b, s, d = x.shape
    tm = min(TOKEN_TILE, s)
    dff = wdown.shape[1]
    tok = pl.BlockSpec((1, tm, d), lambda bi, si: (bi, si, 0))
    return pl.pallas_call(
        functools.partial(_ffn_kernel, tm=tm, dff=dff, final_norm=final_norm),
        grid=(b, s // tm),
        in_specs=[tok, _layer_spec(g, layer), _layer_spec(wup, layer), _layer_spec(dw, layer),
                  _layer_spec(dwb, layer), _layer_spec(wdown, layer), _const_spec(gfin.shape)],
        out_specs=tok,
        out_shape=jax.ShapeDtypeStruct((b, s, d), F32),
        scratch_shapes=[pltpu.VMEM((2, 2 * SUBLANES, dff), F32),
                        pltpu.VMEM((tm, d), F32),
                        pltpu.VMEM((d // LANES, tm + SUBLANES * SUBLANES, LANES), F32),
                        pltpu.VMEM((d // LANES, tm + SUBLANES * SUBLANES, LANES), F32)],
        compiler_params=pltpu.CompilerParams(
            dimension_semantics=("arbitrary", "arbitrary"),
            vmem_limit_bytes=VMEM_LIMIT_BYTES),
        name="ffn",
    )(x, g, wup, dw, dwb, wdown, gfin)


def _pack_w_in(w):
    ua, cq, ckv, kr, qc, kc, vc, ud = jnp.split(
        w, [512, 704, 832, 864, 1120, 1376, 1632], axis=-1)
    z = lambda n: jnp.zeros(w.shape[:-1] + (n,), w.dtype)
    half = MLA_ROPE // 2
    kr_sw = jnp.concatenate([kr[..., half:], kr[..., :half]], axis=-1)
    kr_blk = jnp.concatenate([z(MLA_NOPE), kr, z(LANES - MLA_NOPE - MLA_ROPE)], axis=-1)
    kr_sw_blk = jnp.concatenate([z(MLA_NOPE), kr_sw, z(LANES - MLA_NOPE - MLA_ROPE)], axis=-1)
    packed = jnp.concatenate(
        [ua, cq, z(256 - MLA_Q_RANK), ckv, kr_blk, kr_sw_blk, z(LANES), qc, kc, vc, ud], axis=-1)
    return packed.astype(BF16)


def _pack_w_uq(w):
    nl = w.shape[0]
    w = w.reshape(nl, MLA_Q_RANK, MLA_HEADS, MLA_NOPE + MLA_ROPE)
    nope, rp = w[..., :MLA_NOPE], w[..., MLA_NOPE:]
    half = MLA_ROPE // 2
    rp_sw = jnp.concatenate([rp[..., half:], rp[..., :half]], axis=-1)
    zn = jnp.zeros_like(nope)
    zp = jnp.zeros((nl, MLA_Q_RANK, MLA_HEADS, LANES - MLA_NOPE - MLA_ROPE), w.dtype)

    def fin(a):
        a = a.reshape(nl, MLA_Q_RANK, MLA_HEADS * LANES)
        return jnp.pad(a, ((0, 0), (0, 256 - MLA_Q_RANK), (0, 0))).astype(BF16)

    return (fin(jnp.concatenate([nope, rp, zp], axis=-1)),
            fin(jnp.concatenate([zn, rp_sw, zp], axis=-1)))


def _pack_w_ukv(w):
    nl = w.shape[0]
    w = w.reshape(nl, MLA_KV_RANK, MLA_HEADS, MLA_NOPE + MLA_V)
    kn, v = w[..., :MLA_NOPE], w[..., MLA_NOPE:]
    kn = jnp.concatenate([kn, jnp.zeros_like(kn)], axis=-1).reshape(nl, MLA_KV_RANK, MLA_HEADS * LANES)
    v = v.reshape(nl, MLA_KV_RANK, MLA_HEADS * MLA_V)
    return jnp.concatenate([kn, v], axis=-1).astype(BF16)


def _rope_tables(positions):
    inv_freq = ROPE_BASE ** (-jnp.arange(0, MLA_ROPE, 2, dtype=F32) / MLA_ROPE)
    ang = positions.astype(F32)[:, None] * inv_freq[None, :]
    cos, sin = jnp.cos(ang), jnp.sin(ang)
    s = positions.shape[0]
    pad = jnp.zeros((s, LANES - MLA_NOPE - MLA_ROPE), F32)
    ctab = jnp.concatenate([jnp.ones((s, MLA_NOPE), F32), cos, cos, pad], axis=1)
    stab = jnp.concatenate([jnp.zeros((s, MLA_NOPE), F32), -sin, sin, pad], axis=1)
    return ctab, stab


def _lane_groups(n_lanes, width):
    lane = jnp.arange(n_lanes)[:, None] // width
    return (lane == jnp.arange(LANES)[None, :]).astype(BF16)


def _bias_range(rel_bias):
    shifted = (rel_bias - rel_bias[REL_BUCKETS - 1]) * LOG2_E
    hi = jnp.repeat(jnp.max(shifted, axis=0), 2)
    lo = jnp.repeat(jnp.min(shifted, axis=0), 2)
    pad = (0, LANES - 2 * DIFF_HEADS)
    return jnp.pad(hi, pad).reshape(1, LANES), jnp.pad(hi - lo, pad).reshape(1, LANES)


def _fixed_stabiliser_ok(qmax_sq, kmax_sq, extra_span):
    bound = NORM_MARGIN * jnp.sqrt(qmax_sq * kmax_sq)
    ok = jnp.all(2.0 * bound + extra_span <= SAFE_LOGIT_SPAN)
    return ok.astype(jnp.int32).reshape(1)


def _block_diag(w):
    nl, g, c, _ = w.shape
    out = jnp.zeros((nl, g * c, g * c), w.dtype)
    for i in range(g):
        out = out.at[:, i * c:(i + 1) * c, i * c:(i + 1) * c].set(w[:, i])
    return out


def kernel(x, mem, positions, rel_bias, norm_mix, w_in, w_out, conv_dw, conv_dw_b, conv_ln_g, conv_ln_b, conv_pw, conv_pw_b, mla_q_norm, mla_w_uq, mla_kv_norm, mla_w_ukv, diff_lq1, diff_lk1, diff_lq2, diff_lk2, diff_subln, pool_w, pool_scale, norm_xa, mem_norm, xa_wq, xa_wk, xa_wv, xa_wo, norm_ffn, ffn_up, ffn_dw, ffn_dw_b, ffn_down, norm_final):
    depth = w_in.shape[0]
    s = x.shape[1]
    rows = lambda a: a[:, None, :]
    bf = lambda a: a.astype(BF16)
    ctab, stab = _rope_tables(positions)
    bias_tiles = _bias_tiles(rel_bias, min(DIFF_TILE, s), min(DIFF_TILE, s) // 2)
    kmem, vmem = _mem_kv(mem, rows(mem_norm), bf(xa_wk), bf(xa_wv))
    group_mla = _lane_groups(MLA_HEADS * LANES, LANES)
    group_diff = _lane_groups(2 * DIFF_HEADS * DIFF_HALF, DIFF_HALF)
    bias_hi, bias_span = _bias_range(rel_bias)
    wuq, wuqs = _pack_w_uq(mla_w_uq)
    mix_params = [rows(norm_mix), _pack_w_in(w_in), conv_dw, rows(conv_dw_b), rows(conv_ln_g),
                  rows(conv_ln_b), bf(conv_pw), rows(conv_pw_b),
                  rows(jnp.pad(mla_q_norm, ((0, 0), (0, 256 - MLA_Q_RANK)))), wuq, wuqs,
                  rows(mla_kv_norm), _pack_w_ukv(mla_w_ukv),
                  bf(_block_diag(pool_w)), rows(pool_scale)]
    lq1, lk1, lq2, lk2 = rows(diff_lq1), rows(diff_lk1), rows(diff_lq2), rows(diff_lk2)
    subln = rows(jnp.tile(diff_subln, (1, DIFF_HEADS)))
    w_out, xa_wq, xa_wo, ffn_up, ffn_down = bf(w_out), bf(xa_wq), bf(xa_wo), bf(ffn_up), bf(ffn_down)
    g_xa, g_ffn, ffn_dw_b = rows(norm_xa), rows(norm_ffn), rows(ffn_dw_b)
    for l in range(depth):
        ya, qm, km, vm, qd, kd, vd, yd, stats = _mix_in(l, x, mix_params, ctab, stab, group_mla,
                                                         group_diff)
        stats = jnp.max(stats, axis=1)
        yb = _mla_attn(_fixed_stabiliser_ok(stats[:, 0], stats[:, 1], 0.0), qm, km, vm, stats,
                       group_mla)
        lambda_init = 0.8 - 0.6 * math.exp(-0.3 * l)
        yc = _diff_attn(l, _fixed_stabiliser_ok(stats[:, 2], stats[:, 3], bias_span), qd, kd, vd,
                        stats, group_diff, bias_hi, bias_tiles, lq1, lk1, lq2, lk2, subln,
                        lambda_init)
        x = _xa(l, x, ya, yb, yc, yd, w_out, g_xa, xa_wq, kmem, vmem, xa_wo)
        x = _ffn(l, x, g_ffn, ffn_up, ffn_dw, ffn_dw_b, ffn_down, norm_final.reshape(1, -1),
                 l == depth - 1)
    return x
```

```python
import functools
import math

import jax
import jax.numpy as jnp
from jax import lax
from jax.experimental import pallas as pl
from jax.experimental.pallas import tpu as pltpu

N_GROUPS = 4
CONV_KERNEL = 31
MLA_HEADS = 4
MLA_NOPE = 64
MLA_ROPE = 32
MLA_V = 64
MLA_Q_RANK = 192
MLA_KV_RANK = 128
DIFF_HEADS = 4
DIFF_HALF = 32
DIFF_V = 64
POOL_WINDOWS = (2, 4, 8, 16)
POOL_CH = 64
REL_BUCKETS = 32
REL_MAX_DIST = 128
XA_HEADS = 4
FFN_CONV = 3
ROPE_BASE = 10000.0
NEG_INF = -1e30
LOG2_E = math.log2(math.e)
NORM_MARGIN = 1.02
SAFE_LOGIT_SPAN = 100.0

LANES = 128
SUBLANES = 8
VMEM_LIMIT_BYTES = 56 * 1024 * 1024

TOKEN_TILE = 512
DIFF_TILE = 512
MLA_TILE = 1024
MLA_FAR_KEYS = 1024
MLA_DIAG_SPLIT = 2
CONV_HALO = 32
POOL_HALO = 16
ROW_CHUNK = 64
FFN_CHUNK = 1024

BF16 = jnp.bfloat16
F32 = jnp.float32


def _dot(a, b):
    return jnp.dot(a, b, preferred_element_type=F32)


def _dot_nt(a, b):
    return lax.dot_general(a, b, (((1,), (1,)), ((), ())), preferred_element_type=F32)


def _sigmoid(x):
    return 1.0 / (1.0 + jnp.exp(-x))


def _const_spec(shape):
    nd = len(shape)
    return pl.BlockSpec(shape, lambda *_: (0,) * nd)


def _layer_spec(stacked, layer):
    tail = stacked.shape[1:]
    return pl.BlockSpec((None,) + tail, lambda *_: (layer,) + (0,) * len(tail))


_A0, _A1 = 0, 512
_B0, _B1 = 512, 1280
_C0, _C1 = 1280, 2048
_D0, _D1 = 2048, 2304
W_IN_COLS = 2304


def _to_group_order(val, perm_ref, groups):
    ntile = val.shape[1] // LANES
    pitch = groups + SUBLANES // 2
    for c in range(ntile):
        for i in range(SUBLANES):
            perm_ref[c, i * pitch:i * pitch + groups, :] = val[i * groups:(i + 1) * groups,
                                                                 c * LANES:(c + 1) * LANES]
    return [jnp.concatenate([perm_ref[c, pl.ds(j, SUBLANES, stride=pitch), :]
                             for c in range(ntile)], axis=1) for j in range(groups)]


def _to_row_order(val, perm_ref, groups):
    ntile = val.shape[1] // LANES
    pitch = groups + SUBLANES // 2
    for j in range(groups):
        for c in range(ntile):
            perm_ref[c, pl.ds(j, SUBLANES, stride=pitch), :] = val[
                j * SUBLANES:(j + 1) * SUBLANES, c * LANES:(c + 1) * LANES]
    return jnp.concatenate(
        [jnp.concatenate([perm_ref[c, i * pitch:i * pitch + groups, :] for i in range(SUBLANES)],
                         axis=0) for c in range(ntile)], axis=1)


def _fill_window(grps, ext_ref, prev_ref, halo):
    groups = len(grps)
    first_sublane = lax.broadcasted_iota(jnp.int32, grps[0].shape, 0) == 0
    for m in range(groups - halo, groups):
        rows = slice((m - groups + halo) * SUBLANES, (m - groups + halo + 1) * SUBLANES)
        ext_ref[rows, :] = jnp.where(first_sublane, pltpu.roll(prev_ref[rows, :], 1, 0),
                                     pltpu.roll(grps[m], 1, 0))
        prev_ref[rows, :] = grps[m]
    for j in range(groups):
        ext_ref[(halo + j) * SUBLANES:(halo + j + 1) * SUBLANES, :] = grps[j]


def _mix_in_kernel(x_ref, g_ref, win_ref, dw_ref, dwb_ref, lng_ref, lnb_ref, pw_ref, pwb_ref,
                   qn_ref, wuq_ref, wuqs_ref, kvn_ref, wukv_ref, ct_ref, st_ref,
                   poolw_ref, pools_ref, gm_ref, gd_ref,
                   ya_ref, qm_ref, km_ref, vm_ref, qd_ref, kd_ref, vd_ref, yd_ref, stats_ref,
                   ext_a, prev_a, zbuf, ext_d, prev_d, dbuf, perm_a, perm_ya, perm_d, perm_yd, *, tm):
    s = pl.program_id(1)
    groups = tm // SUBLANES

    def max_sq_norm(v, group_ref):
        vf = v.astype(F32)
        return jnp.max(_dot((vf * vf).astype(BF16), group_ref[...]), axis=0, keepdims=True)

    @pl.when(s == 0)
    def _():
        prev_a[...] = jnp.zeros(prev_a.shape, F32)
        prev_d[...] = jnp.zeros(prev_d.shape, F32)

    x = x_ref[0]
    h = x * lax.rsqrt(jnp.mean(x * x, axis=-1, keepdims=True) + 1e-6) * g_ref[...]
    hb = h.astype(BF16)

    pa = _dot(hb, win_ref[:, _A0:_A1])
    pb = _dot(hb, win_ref[:, _B0:_B1])
    pc = _dot(hb, win_ref[:, _C0:_C1])
    ud = _dot(hb, win_ref[:, _D0:_D1])

    glu = pa[:, 0:256] * _sigmoid(pa[:, 256:512])
    _fill_window(_to_group_order(glu, perm_a, groups), ext_a, prev_a, CONV_HALO)
    for c in range(tm // ROW_CHUNK):
        r0 = c * ROW_CHUNK
        acc = jnp.broadcast_to(dwb_ref[...], (ROW_CHUNK, 256))
        for k in range(CONV_KERNEL):
            off = r0 + (CONV_HALO - (CONV_KERNEL - 1) + k) * SUBLANES
            acc = acc + dw_ref[k:k + 1, :] * ext_a[off:off + ROW_CHUNK, :]
        mu = jnp.mean(acc, axis=-1, keepdims=True)
        cen = acc - mu
        var = jnp.mean(cen * cen, axis=-1, keepdims=True)
        yn = cen * lax.rsqrt(var + 1e-5) * lng_ref[...] + lnb_ref[...]
        zbuf[r0:r0 + ROW_CHUNK, :] = (yn * _sigmoid(yn)).astype(BF16)
    ya = _to_row_order(_dot(zbuf[...], pw_ref[...]), perm_ya, groups)
    ya_ref[0] = (ya + pwb_ref[...]).astype(BF16)

    ct = ct_ref[...]
    st = st_ref[...]
    cq = pb[:, 0:256]
    cqn = cq * lax.rsqrt(jnp.sum(cq * cq, axis=-1, keepdims=True) * (1.0 / MLA_Q_RANK) + 1e-6)
    cqb = (cqn * qn_ref[...]).astype(BF16)
    q = _dot(cqb, wuq_ref[...])
    qs = _dot(cqb, wuqs_ref[...])
    ct4 = jnp.concatenate([ct] * MLA_HEADS, axis=1)
    st4 = jnp.concatenate([st] * MLA_HEADS, axis=1)
    q_scale = LOG2_E * (MLA_NOPE + MLA_ROPE) ** -0.5
    qm = ((q * ct4 + qs * st4) * q_scale).astype(BF16)
    qm_ref[0] = qm
    ckv = pb[:, 256:384]
    ckvn = ckv * lax.rsqrt(jnp.mean(ckv * ckv, axis=-1, keepdims=True) + 1e-6)
    kv = _dot((ckvn * kvn_ref[...]).astype(BF16), wukv_ref[...])
    kr = pb[:, 384:512] * ct + pb[:, 512:640] * st
    km = (kv[:, 0:512] + jnp.concatenate([kr] * MLA_HEADS, axis=1)).astype(BF16)
    km_ref[0] = km
    vm_ref[0] = kv[:, 512:768].astype(BF16)

    qd = (pc[:, 0:256] * (LOG2_E * DIFF_HALF ** -0.5)).astype(BF16)
    kd = pc[:, 256:512].astype(BF16)
    qd_ref[0] = qd
    kd_ref[0] = kd
    stats_ref[0, 0] = jnp.concatenate(
        [max_sq_norm(qm, gm_ref), max_sq_norm(km, gm_ref),
         max_sq_norm(qd, gd_ref), max_sq_norm(kd, gd_ref),
         jnp.zeros((SUBLANES - 4, LANES), F32)], axis=0)
    vd_ref[0] = pc[:, 512:768].astype(BF16)

    _fill_window(_to_group_order(ud, perm_d, groups), ext_d, prev_d, POOL_HALO)
    lane = lax.broadcasted_iota(jnp.int32, (ROW_CHUNK, 256), 1)
    grp = lane // POOL_CH
    win_len = jnp.where(grp == 0, POOL_WINDOWS[0],
                        jnp.where(grp == 1, POOL_WINDOWS[1],
                                  jnp.where(grp == 2, POOL_WINDOWS[2], POOL_WINDOWS[3])))
    for c in range(tm // ROW_CHUNK):
        r0 = c * ROW_CHUNK
        base = r0 + POOL_HALO * SUBLANES
        u = ext_d[base:base + ROW_CHUNK, :]
        run = u
        sums = {}
        for j in range(1, POOL_WINDOWS[-1]):
            run = run + ext_d[base - j * SUBLANES:base - j * SUBLANES + ROW_CHUNK, :]
            if j + 1 in POOL_WINDOWS:
                sums[j + 1] = run
        win = jnp.where(grp == 0, sums[2],
                        jnp.where(grp == 1, sums[4], jnp.where(grp == 2, sums[8], sums[16])))
        r = r0 + lax.broadcasted_iota(jnp.int32, (ROW_CHUNK, 256), 0)
        t_glob = s * tm + (r % SUBLANES) * groups + r // SUBLANES
        cnt = jnp.minimum(t_glob + 1, win_len).astype(F32)
        dbuf[r0:r0 + ROW_CHUNK, :] = (win / cnt - u).astype(BF16)
    yd = _to_row_order(_dot(dbuf[...], poolw_ref[...]), perm_yd, groups)
    yd_ref[0] = (yd * pools_ref[...]).astype(BF16)


def _mix_in(layer, x, per_layer, ctab, stab, group_mla, group_diff):
    b, s, d = x.shape
    tm = min(TOKEN_TILE, s)
    tok = lambda w: pl.BlockSpec((1, tm, w), lambda bi, si: (bi, si, 0))
    head, tail = per_layer[:13], per_layer[13:]
    in_specs = ([tok(d)] + [_layer_spec(a, layer) for a in head]
                + [pl.BlockSpec((tm, LANES), lambda bi, si: (si, 0))] * 2
                + [_layer_spec(a, layer) for a in tail]
                + [_const_spec(group_mla.shape), _const_spec(group_diff.shape)])
    widths = (256, 512, 512, 256, 256, 256, 256, 256)
    stats_spec = pl.BlockSpec((1, 1, SUBLANES, LANES), lambda bi, si: (bi, si, 0, 0))
    return pl.pallas_call(
        functools.partial(_mix_in_kernel, tm=tm),
        grid=(b, s // tm),
        in_specs=in_specs,
        out_specs=[tok(w) for w in widths] + [stats_spec],
        out_shape=([jax.ShapeDtypeStruct((b, s, w), BF16) for w in widths]
                   + [jax.ShapeDtypeStruct((b, s // tm, SUBLANES, LANES), F32)]),
        scratch_shapes=[pltpu.VMEM((CONV_HALO * SUBLANES + tm, 256), F32),
                        pltpu.VMEM((CONV_HALO * SUBLANES, 256), F32),
                        pltpu.VMEM((tm, 256), BF16),
                        pltpu.VMEM((POOL_HALO * SUBLANES + tm, 256), F32),
                        pltpu.VMEM((POOL_HALO * SUBLANES, 256), F32),
                        pltpu.VMEM((tm, 256), BF16)]
                       + [pltpu.VMEM((2, tm + SUBLANES * SUBLANES, LANES), F32)] * 4,
        compiler_params=pltpu.CompilerParams(
            dimension_semantics=("arbitrary", "arbitrary"),
            vmem_limit_bytes=VMEM_LIMIT_BYTES),
        name="mix_in",
    )(x, *head, ctab, stab, *tail, group_mla, group_diff)


def _head_lanes(vals, rows):
    lane = lax.broadcasted_iota(jnp.int32, (rows, LANES), 1)
    return jnp.concatenate([jnp.where(lane < 64, vals[0], vals[1]),
                            jnp.where(lane < 64, vals[2], vals[3])], axis=1)


def _row_total(l_ref, idx):
    return jnp.sum(l_ref[idx], axis=-1, keepdims=True)


def _stack_masked_v(v_blk):
    n = v_blk.shape[0]
    lane = lax.broadcasted_iota(jnp.int32, (n, 256), 1)
    parts = []
    for h in range(4):
        keep = (lane >= 64 * h) & (lane < 64 * (h + 1))
        parts.append(jnp.where(keep, v_blk, jnp.zeros_like(v_blk)))
    return jnp.concatenate(parts, axis=0)


def _softmax_step(sc, m_ref, l_ref, idx, online, rows=slice(None)):
    if online:
        m_old = m_ref[idx, rows]
        m_new = jnp.maximum(m_old, jnp.max(sc, axis=-1, keepdims=True))
        alpha = jnp.exp2(m_old - m_new)
        m_ref[idx, rows] = m_new
    else:
        m_new = m_ref[idx, rows]
        alpha = None
    reps = sc.shape[1] // LANES
    p = jnp.exp2(sc - jnp.concatenate([m_new] * reps, axis=1))
    psum = p[:, 0:LANES]
    for r in range(1, reps):
        psum = psum + p[:, r * LANES:(r + 1) * LANES]
    l_ref[idx, rows] = (alpha * l_ref[idx, rows] if online else l_ref[idx, rows]) + psum
    return p, alpha


def _logit_bounds(q, group_ref, kmax_sq):
    qf = q.astype(F32)
    qn_sq = _dot((qf * qf).astype(BF16), group_ref[...])
    return NORM_MARGIN * jnp.sqrt(qn_sq * kmax_sq)


def _mla_attn_kernel(safe_ref, q_ref, k_ref, v_ref, stats_ref, group_ref, o_ref,
                     m_ref, l_ref, acc_ref, *, t, far):
    qi = pl.program_id(1)
    l_ref[...] = jnp.zeros(l_ref.shape, F32)
    acc_ref[...] = jnp.zeros(acc_ref.shape, F32)

    def run(online):
        if online:
            m_ref[...] = jnp.full(m_ref.shape, NEG_INF, F32)
        else:
            bounds = _logit_bounds(q_ref[0], group_ref, stats_ref[0, 1:2, :])
            for h in range(MLA_HEADS):
                m_ref[h] = jnp.broadcast_to(bounds[:, h:h + 1], (t, LANES))

        def step(k0, n, r0=0, diagonal=False):
            nr = t - r0
            rows = slice(r0, t)
            vst = _stack_masked_v(v_ref[0, pl.ds(k0, n), :])
            probs, alphas = [], []
            for h in range(MLA_HEADS):
                cols = slice(LANES * h, LANES * (h + 1))
                sc = _dot_nt(q_ref[0, rows, cols], k_ref[0, pl.ds(k0, n), cols])
                if diagonal:
                    row = lax.broadcasted_iota(jnp.int32, (nr, n), 0)
                    col = lax.broadcasted_iota(jnp.int32, (nr, n), 1)
                    sc = jnp.where(row >= col, sc, NEG_INF)
                p, alpha = _softmax_step(sc, m_ref, l_ref, h, online, rows)
                probs.append(p.astype(BF16))
                alphas.append(alpha)
            pv = _dot(jnp.concatenate(probs, axis=1), vst)
            old_acc = acc_ref[rows, :] * _head_lanes(alphas, nr) if online else acc_ref[rows, :]
            acc_ref[rows, :] = old_acc + pv

        def far_body(j, carry):
            step(pl.multiple_of(j * far, far), far)
            return carry

        def near_body(j, carry):
            step(pl.multiple_of(j * t, t), t)
            return carry

        per_far = far // t
        n_far = qi // per_far
        lax.fori_loop(0, n_far, far_body, 0)
        lax.fori_loop(n_far * per_far, qi, near_body, 0)
        strip = t // MLA_DIAG_SPLIT
        for j in range(MLA_DIAG_SPLIT):
            step(pl.multiple_of(qi * t + j * strip, strip), strip, j * strip, True)

    safe = safe_ref[0] != 0
    pl.when(safe)(functools.partial(run, False))
    pl.when(safe_ref[0] == 0)(functools.partial(run, True))
    inv_l = _head_lanes([1.0 / _row_total(l_ref, h) for h in range(MLA_HEADS)], t)
    o_ref[0] = (acc_ref[...] * inv_l).astype(BF16)


def _mla_attn(safe, q, k, v, stats, group):
    b, s, _ = q.shape
    t = min(MLA_TILE, s)
    far = min(MLA_FAR_KEYS, s)
    return pl.pallas_call(
        functools.partial(_mla_attn_kernel, t=t, far=far),
        grid=(b, s // t),
        in_specs=[pl.BlockSpec(memory_space=pltpu.SMEM),
                  pl.BlockSpec((1, t, 512), lambda bi, qi: (bi, qi, 0)),
                  pl.BlockSpec((1, s, 512), lambda bi, qi: (bi, 0, 0)),
                  pl.BlockSpec((1, s, 256), lambda bi, qi: (bi, 0, 0)),
                  pl.BlockSpec((1, SUBLANES, LANES), lambda bi, qi: (bi, 0, 0)),
                  _const_spec(group.shape)],
        out_specs=pl.BlockSpec((1, t, 256), lambda bi, qi: (bi, qi, 0)),
        out_shape=jax.ShapeDtypeStruct((b, s, 256), BF16),
        scratch_shapes=[pltpu.VMEM((MLA_HEADS, t, LANES), F32),
                        pltpu.VMEM((MLA_HEADS, t, LANES), F32),
                        pltpu.VMEM((t, 256), F32)],
        compiler_params=pltpu.CompilerParams(
            dimension_semantics=("arbitrary", "arbitrary"),
            vmem_limit_bytes=VMEM_LIMIT_BYTES),
        name="mla_attn",
    )(safe, q, k, v, stats, group)


def _bias_tiles_kernel(rb_ref, out_ref, *, t, tk):
    row = lax.broadcasted_iota(jnp.int32, (t, tk), 0)
    col = lax.broadcasted_iota(jnp.int32, (t, tk), 1)
    max_exact = REL_BUCKETS // 2
    for j in range(t // tk + 1):
        rel = (1 - j) * tk + row - col
        n = jnp.maximum(rel, 0)
        nf = jnp.maximum(n, 1).astype(F32)
        large = max_exact + (jnp.log(nf / max_exact) / math.log(REL_MAX_DIST / max_exact)
                             * (REL_BUCKETS - max_exact)).astype(jnp.int32)
        large = jnp.minimum(large, REL_BUCKETS - 1)
        bucket = jnp.where(n < max_exact, n, large)
        for h in range(DIFF_HEADS):
            far = rb_ref[REL_BUCKETS - 1, h]
            val = jnp.zeros((t, tk), F32)
            for bkt in range(REL_BUCKETS - 1):
                val = jnp.where(bucket == bkt, (rb_ref[bkt, h] - far) * LOG2_E, val)
            out_ref[h, j] = jnp.where(rel >= 0, val, NEG_INF)


def _bias_tiles(rel_bias, t, tk):
    return pl.pallas_call(
        functools.partial(_bias_tiles_kernel, t=t, tk=tk),
        in_specs=[pl.BlockSpec(memory_space=pltpu.SMEM)],
        out_specs=pl.BlockSpec(memory_space=pltpu.VMEM),
        out_shape=jax.ShapeDtypeStruct((DIFF_HEADS, t // tk + 1, t, tk), F32),
        name="bias_tiles",
    )(rel_bias)


def _diff_attn_kernel(safe_ref, q_ref, k_ref, v_ref, stats_ref, group_ref, bias_hi_ref, bias_ref,
                      lq1_ref, lk1_ref, lq2_ref, lk2_ref, subln_ref, o_ref,
                      qst_ref, m_ref, l_ref, acc_ref, s0_ref, s1_ref, *, t, tk, lambda_init):
    qi = pl.program_id(1)
    nmaps = 2 * DIFF_HEADS
    per_tile = t // tk
    l_ref[...] = jnp.zeros(l_ref.shape, F32)
    acc_ref[...] = jnp.zeros(acc_ref.shape, F32)
    q = q_ref[0]
    lane = lax.broadcasted_iota(jnp.int32, (t, 256), 1)
    for idx in range(nmaps):
        keep = (lane >= DIFF_HALF * idx) & (lane < DIFF_HALF * (idx + 1))
        qst_ref[idx * t:(idx + 1) * t, :] = jnp.where(keep, q, jnp.zeros_like(q))

    def scores(kb, r0=0):
        lhs = qst_ref[...] if r0 == 0 else jnp.concatenate(
            [qst_ref[idx * t + r0:(idx + 1) * t, :] for idx in range(nmaps)], axis=0)
        return _dot_nt(lhs, k_ref[0, pl.ds(pl.multiple_of(kb * tk, tk), tk), :])

    def consume(sc_of, kb, bias_tile, online, r0=0):
        nr = t - r0
        rows = slice(r0, t)
        vst =_stack_masked_v(v_ref[0, pl.ds(pl.multiple_of(kb * tk, tk), tk), :])
        probs = [[], []]
        alphas = [[], []]
        for h in range(DIFF_HEADS):
            for which in range(2):
                idx = 2 * h + which
                sc = sc_of(idx)
                if bias_tile is not None:
                    sc = sc + bias_ref[h, bias_tile, rows, :]
                p, alpha = _softmax_step(sc, m_ref, l_ref, idx, online, rows)
                probs[which].append(p.astype(BF16))
                alphas[which].append(alpha)
        p_all = jnp.concatenate([jnp.concatenate(probs[0], axis=1),
                                 jnp.concatenate(probs[1], axis=1)], axis=0)
        pv = _dot(p_all, vst)
        for which in range(2):
            arows = slice(which * t + r0, (which + 1) * t)
            old = acc_ref[arows, :] * _head_lanes(alphas[which], nr) if online else acc_ref[arows, :]
            acc_ref[arows, :] = old + pv[which * nr:(which + 1) * nr, :]

    def from_ref(ref):
        return lambda idx: ref[idx * t:(idx + 1) * t, :]

    def from_val(sc_all):
        nr = sc_all.shape[0] // nmaps
        return lambda idx: sc_all[idx * nr:(idx + 1) * nr, :]

    hidden = lambda j: j * tk

    first_near = per_tile * qi - 1
    n_plain = jnp.maximum(first_near, 0)

    def run_online():
        m_ref[...] = jnp.full(m_ref.shape, NEG_INF, F32)

        def step(kb, bias_tile, r0=0):
            consume(from_val(scores(kb, r0)), kb, bias_tile, True, r0)

        def body(kb, carry):
            step(kb, None)
            return carry

        lax.fori_loop(0, n_plain, body, 0)
        pl.when(qi > 0)(lambda: step(first_near, 0))
        for j in range(per_tile):
            step(first_near + 1 + j, 1 + j, hidden(j))

    def run_fixed():
        bounds = _logit_bounds(q, group_ref, stats_ref[0, 3:4, :]) + bias_hi_ref[...]
        for idx in range(nmaps):
            m_ref[idx] = jnp.broadcast_to(bounds[:, idx:idx + 1], (t, LANES))

        def pair_body(i, carry):
            s1_ref[...] = scores(2 * i + 1)
            consume(from_ref(s0_ref), 2 * i, None, False)
            s0_ref[...] = scores(2 * i + 2)
            consume(from_ref(s1_ref), 2 * i + 1, None, False)
            return carry

        s0_ref[...] = scores(0)
        n_pairs = n_plain // 2
        lax.fori_loop(0, n_pairs, pair_body, 0)

        @pl.when(qi == 0)
        def _():
            sc1 = scores(1, hidden(1))
            consume(from_ref(s0_ref), 0, 1, False)
            consume(from_val(sc1), 1, 2, False, hidden(1))

        @pl.when(qi > 0)
        def _():
            b0 = 2 * n_pairs
            s1_ref[...] = scores(b0 + 1)
            consume(from_ref(s0_ref), b0, None, False)
            sc2 = scores(b0 + 2)
            consume(from_ref(s1_ref), b0 + 1, 0, False)
            sc3 = scores(b0 + 3, hidden(1))
            consume(from_val(sc2), b0 + 2, 1, False)
            consume(from_val(sc3), b0 + 3, 2, False, hidden(1))

    pl.when(safe_ref[0] != 0)(run_fixed)
    pl.when(safe_ref[0] == 0)(run_online)

    lam = (jnp.exp(jnp.sum(lq1_ref[...] * lk1_ref[...], axis=-1, keepdims=True))
           - jnp.exp(jnp.sum(lq2_ref[...] * lk2_ref[...], axis=-1, keepdims=True))
           + lambda_init)
    o1 = acc_ref[0:t, :] * _head_lanes([1.0 / _row_total(l_ref, 2 * h) for h in range(DIFF_HEADS)], t)
    o2 = acc_ref[t:2 * t, :] * _head_lanes(
        [1.0 / _row_total(l_ref, 2 * h + 1) for h in range(DIFF_HEADS)], t)
    o = o1 - lam * o2
    osq = o * o
    ms = []
    for h in range(DIFF_HEADS):
        keep = (lane >= DIFF_V * h) & (lane < DIFF_V * (h + 1))
        ms.append(jnp.sum(jnp.where(keep, osq, 0.0), axis=-1, keepdims=True) * (1.0 / DIFF_V))
    on = o * lax.rsqrt(_head_lanes(ms, t) + 1e-5) * subln_ref[...]
    o_ref[0] = (on * (1.0 - lambda_init)).astype(BF16)


def _diff_attn(layer, safe, q, k, v, stats, group, bias_hi, bias_tiles, lq1, lk1, lq2, lk2, subln,
               lambda_init):
    b, s, _ = q.shape
    t = min(DIFF_TILE, s)
    tk = t // 2
    return pl.pallas_call(
        functools.partial(_diff_attn_kernel, t=t, tk=tk, lambda_init=lambda_init),
        grid=(b, s // t),
        in_specs=[pl.BlockSpec(memory_space=pltpu.SMEM),
                  pl.BlockSpec((1, t, 256), lambda bi, qi: (bi, qi, 0)),
                  pl.BlockSpec((1, s, 256), lambda bi, qi: (bi, 0, 0)),
                  pl.BlockSpec((1, s, 256), lambda bi, qi: (bi, 0, 0)),
                  pl.BlockSpec((1, SUBLANES, LANES), lambda bi, qi: (bi, 0, 0)),
                  _const_spec(group.shape), _const_spec(bias_hi.shape),
                  _const_spec(bias_tiles.shape),
                  _layer_spec(lq1, layer), _layer_spec(lk1, layer),
                  _layer_spec(lq2, layer), _layer_spec(lk2, layer),
                  _layer_spec(subln, layer)],
        out_specs=pl.BlockSpec((1, t, 256), lambda bi, qi: (bi, qi, 0)),
        out_shape=jax.ShapeDtypeStruct((b, s, 256), BF16),
        scratch_shapes=[pltpu.VMEM((8 * t, 256), BF16),
                        pltpu.VMEM((2 * DIFF_HEADS, t, LANES), F32),
                        pltpu.VMEM((2 * DIFF_HEADS, t, LANES), F32),
                        pltpu.VMEM((2 * t, 256), F32),
                        pltpu.VMEM((8 * t, tk), F32),
                        pltpu.VMEM((8 * t, tk), F32)],
        compiler_params=pltpu.CompilerParams(
            dimension_semantics=("arbitrary", "arbitrary"),
            vmem_limit_bytes=VMEM_LIMIT_BYTES),
        name="diff_attn",
    )(safe, q, k, v, stats, group, bias_hi, bias_tiles, lq1, lk1, lq2, lk2, subln)


def _mem_kv_kernel(mem_ref, g_ref, wk_ref, wv_ref, k_ref, v_ref):
    m = mem_ref[0]
    mn = m * lax.rsqrt(jnp.mean(m * m, axis=-1, keepdims=True) + 1e-6) * g_ref[0]
    mb = mn.astype(BF16)
    k_ref[0, 0] = _dot(mb, wk_ref[0]).astype(BF16)
    v_ref[0, 0] = _dot(mb, wv_ref[0]).astype(BF16)


def _mem_kv(mem, g, wk, wv):
    b, m, d = mem.shape
    nl = wk.shape[0]
    out = jax.ShapeDtypeStruct((nl, b, m, d), BF16)
    return pl.pallas_call(
        _mem_kv_kernel,
        grid=(nl, b),
        in_specs=[pl.BlockSpec((1, m, d), lambda li, bi: (bi, 0, 0)),
                  pl.BlockSpec((1, 1, d), lambda li, bi: (li, 0, 0)),
                  pl.BlockSpec((1, d, d), lambda li, bi: (li, 0, 0)),
                  pl.BlockSpec((1, d, d), lambda li, bi: (li, 0, 0))],
        out_specs=[pl.BlockSpec((1, 1, m, d), lambda li, bi: (li, bi, 0, 0))] * 2,
        out_shape=[out, out],
        compiler_params=pltpu.CompilerParams(
            dimension_semantics=("arbitrary", "arbitrary"),
            vmem_limit_bytes=VMEM_LIMIT_BYTES),
        name="mem_kv",
    )(mem, g, wk, wv)


def _xa_kernel(x_ref, ya_ref, yb_ref, yc_ref, yd_ref, wout_ref, g_ref, wq_ref, km_ref, vm_ref,
               wo_ref, o_ref, obuf):
    x1 = x_ref[0]
    for i, y_ref in enumerate((ya_ref, yb_ref, yc_ref, yd_ref)):
        x1 = x1 + _dot(y_ref[0], wout_ref[256 * i:256 * (i + 1), :])
    hx = x1 * lax.rsqrt(jnp.mean(x1 * x1, axis=-1, keepdims=True) + 1e-6) * g_ref[...]
    hd = wq_ref.shape[1] // XA_HEADS
    q = (_dot(hx.astype(BF16), wq_ref[...]) * (hd ** -0.5)).astype(BF16)
    for h in range(XA_HEADS):
        cols = slice(hd * h, hd * (h + 1))
        sc = _dot_nt(q[:, cols], km_ref[0, :, cols])
        p = jnp.exp(sc - jnp.max(sc, axis=-1, keepdims=True))
        oh = _dot(p.astype(BF16), vm_ref[0, :, cols]) / jnp.sum(p, axis=-1, keepdims=True)
        obuf[:, cols] = oh.astype(BF16)
    o_ref[0] = x1 + _dot(obuf[...], wo_ref[...])


def _xa(layer, x, ya, yb, yc, yd, wout, g, wq, kmem, vmem, wo):
    b, s, d = x.shape
    tm = min(TOKEN_TILE, s)
    m = kmem.shape[2]
    tok = lambda w: pl.BlockSpec((1, tm, w), lambda bi, si: (bi, si, 0))
    mem_spec = pl.BlockSpec((None, 1, m, d), lambda bi, si: (layer, bi, 0, 0))
    return pl.pallas_call(
        _xa_kernel,
        grid=(b, s // tm),
        in_specs=[tok(d), tok(256), tok(256), tok(256), tok(256),
                  _layer_spec(wout, layer), _layer_spec(g, layer), _layer_spec(wq, layer),
                  mem_spec, mem_spec, _layer_spec(wo, layer)],
        out_specs=tok(d),
        out_shape=jax.ShapeDtypeStruct((b, s, d), F32),
        scratch_shapes=[pltpu.VMEM((tm, d), BF16)],
        compiler_params=pltpu.CompilerParams(
            dimension_semantics=("arbitrary", "arbitrary"),
            vmem_limit_bytes=VMEM_LIMIT_BYTES),
        name="xa",
    )(x, ya, yb, yc, yd, wout, g, wq, kmem, vmem, wo)


def _ffn_kernel(x_ref, g_ref, wup_ref, dw_ref, dwb_ref, wdown_ref, gfin_ref, o_ref,
                carry, acc_ref, perm_in, perm_out, *, tm, dff, final_norm):
    s = pl.program_id(1)
    groups = tm // SUBLANES

    @pl.when(s == 0)
    def _():
        carry[...] = jnp.zeros(carry.shape, F32)

    x = x_ref[0]
    h = x * lax.rsqrt(jnp.mean(x * x, axis=-1, keepdims=True) + 1e-6) * g_ref[...]
    hb = jnp.concatenate(_to_group_order(h, perm_in, groups), axis=0).astype(BF16)
    bounds = list(range(0, dff, FFN_CHUNK)) + [dff]
    nchunk = len(bounds) - 1

    def conv(u, slot, col0):
        width = u.shape[1]
        first_sublane = lax.broadcasted_iota(jnp.int32, (SUBLANES, width), 0) == 0
        cols = slice(col0, col0 + width)
        wcols = slice(slot * dff + col0, slot * dff + col0 + width)
        tail = u[tm - 2 * SUBLANES:tm, :]
        prev = carry[slot, :, cols]
        carry[slot, :, cols] = tail

        def wrap(k):
            lo, hi = k * SUBLANES, (k + 1) * SUBLANES
            return jnp.where(first_sublane, pltpu.roll(prev[lo:hi, :], 1, 0),
                             pltpu.roll(tail[lo:hi, :], 1, 0))

        back1 = wrap(1)
        u1 = jnp.concatenate([back1, u[0:tm - SUBLANES, :]], axis=0)
        u2 = jnp.concatenate([wrap(0), back1, u[0:tm - 2 * SUBLANES, :]], axis=0)
        w = dw_ref[:, wcols]
        return dwb_ref[:, wcols] + w[2:3, :] * u + w[1:2, :] * u1 + w[0:1, :] * u2

    def up(j):
        c0, c1 = bounds[j], bounds[j + 1]
        return (_dot(hb, wup_ref[:, c0:c1]), _dot(hb, wup_ref[:, dff + c0:dff + c1]))

    ua, ug = up(0)
    for j in range(nchunk):
        col0 = bounds[j]
        nxt = up(j + 1) if j + 1 < nchunk else None
        a = conv(ua, 0, col0)
        gt = conv(ug, 1, col0)
        act = (gt * _sigmoid(gt) * a).astype(BF16)
        part = _dot(act, wdown_ref[bounds[j]:bounds[j + 1], :])
        if j == 0:
            acc_ref[...] = part
        else:
            acc_ref[...] += part
        if nxt is not None:
            ua, ug = nxt
    y = x + _to_row_order(acc_ref[...], perm_out, groups)
    if final_norm:
        y = y * lax.rsqrt(jnp.mean(y * y, axis=-1, keepdims=True) + 1e-6) * gfin_ref[...]
    o_ref[0] = y


def _ffn(layer, x, g, wup, dw, dwb, wdown, gfin, final_norm):
    b, s, d = x.shape
    tm = min(TOKEN_TILE, s)
    dff = wdown.shape[1]
    tok = pl.BlockSpec((1, tm, d), lambda bi, si: (bi, si, 0))
    return pl.pallas_call(
        functools.partial(_ffn_kernel, tm=tm, dff=dff, final_norm=final_norm),
        grid=(b, s // tm),
        in_specs=[tok, _layer_spec(g, layer), _layer_spec(wup, layer), _layer_spec(dw, layer),
                  _layer_spec(dwb, layer), _layer_spec(wdown, layer), _const_spec(gfin.shape)],
        out_specs=tok,
        out_shape=jax.ShapeDtypeStruct((b, s, d), F32),
        scratch_shapes=[pltpu.VMEM((2, 2 * SUBLANES, dff), F32),
                        pltpu.VMEM((tm, d), F32),
                        pltpu.VMEM((d // LANES, tm + SUBLANES * SUBLANES, LANES), F32),
                        pltpu.VMEM((d // LANES, tm + SUBLANES * SUBLANES, LANES), F32)],
        compiler_params=pltpu.CompilerParams(
            dimension_semantics=("arbitrary", "arbitrary"),
            vmem_limit_bytes=VMEM_LIMIT_BYTES),
        name="ffn",
    )(x, g, wup, dw, dwb, wdown, gfin)


def _pack_w_in(w):
    ua, cq, ckv, kr, qc, kc, vc, ud = jnp.split(
        w, [512, 704, 832, 864, 1120, 1376, 1632], axis=-1)
    z = lambda n: jnp.zeros(w.shape[:-1] + (n,), w.dtype)
    half = MLA_ROPE // 2
    kr_sw = jnp.concatenate([kr[..., half:], kr[..., :half]], axis=-1)
    kr_blk = jnp.concatenate([z(MLA_NOPE), kr, z(LANES - MLA_NOPE - MLA_ROPE)], axis=-1)
    kr_sw_blk = jnp.concatenate([z(MLA_NOPE), kr_sw, z(LANES - MLA_NOPE - MLA_ROPE)], axis=-1)
    packed = jnp.concatenate(
        [ua, cq, z(256 - MLA_Q_RANK), ckv, kr_blk, kr_sw_blk, z(LANES), qc, kc, vc, ud], axis=-1)
    return packed


def _pack_w_uq(w):
    nl = w.shape[0]
    w = w.reshape(nl, MLA_Q_RANK, MLA_HEADS, MLA_NOPE + MLA_ROPE)
    nope, rp = w[..., :MLA_NOPE], w[..., MLA_NOPE:]
    half = MLA_ROPE // 2
    rp_sw = jnp.concatenate([rp[..., half:], rp[..., :half]], axis=-1)
    zn = jnp.zeros_like(nope)
    zp = jnp.zeros((nl, MLA_Q_RANK, MLA_HEADS, LANES - MLA_NOPE - MLA_ROPE), w.dtype)

    def fin(a):
        a = a.reshape(nl, MLA_Q_RANK, MLA_HEADS * LANES)
        return jnp.pad(a, ((0, 0), (0, 256 - MLA_Q_RANK), (0, 0))).astype(BF16)

    return (fin(jnp.concatenate([nope, rp, zp], axis=-1)),
            fin(jnp.concatenate([zn, rp_sw, zp], axis=-1)))


def _pack_w_ukv(w):
    nl = w.shape[0]
    w = w.reshape(nl, MLA_KV_RANK, MLA_HEADS, MLA_NOPE + MLA_V)
    kn, v = w[..., :MLA_NOPE], w[..., MLA_NOPE:]
    kn = jnp.concatenate([kn, jnp.zeros_like(kn)], axis=-1).reshape(nl, MLA_KV_RANK, MLA_HEADS * LANES)
    v = v.reshape(nl, MLA_KV_RANK, MLA_HEADS * MLA_V)
    return jnp.concatenate([kn, v], axis=-1).astype(BF16)


def _rope_tables(positions):
    inv_freq = ROPE_BASE ** (-jnp.arange(0, MLA_ROPE, 2, dtype=F32) / MLA_ROPE)
    ang = positions.astype(F32)[:, None] * inv_freq[None, :]
    cos, sin = jnp.cos(ang), jnp.sin(ang)
    s = positions.shape[0]
    pad = jnp.zeros((s, LANES - MLA_NOPE - MLA_ROPE), F32)
    ctab = jnp.concatenate([jnp.ones((s, MLA_NOPE), F32), cos, cos, pad], axis=1)
    stab = jnp.concatenate([jnp.zeros((s, MLA_NOPE), F32), -sin, sin, pad], axis=1)
    return ctab, stab


def _lane_groups(n_lanes, width):
    lane = jnp.arange(n_lanes)[:, None] // width
    return (lane == jnp.arange(LANES)[None, :]).astype(BF16)


def _bias_range(rel_bias):
    shifted = (rel_bias - rel_bias[REL_BUCKETS - 1]) * LOG2_E
    hi = jnp.repeat(jnp.max(shifted, axis=0), 2)
    lo = jnp.repeat(jnp.min(shifted, axis=0), 2)
    pad = (0, LANES - 2 * DIFF_HEADS)
    return jnp.pad(hi, pad).reshape(1, LANES), jnp.pad(hi - lo, pad).reshape(1, LANES)


def _fixed_stabiliser_ok(qmax_sq, kmax_sq, extra_span):
    bound = NORM_MARGIN * jnp.sqrt(qmax_sq * kmax_sq)
    ok = jnp.all(2.0 * bound + extra_span <= SAFE_LOGIT_SPAN)
    return ok.astype(jnp.int32).reshape(1)


def _block_diag(w):
    nl, g, c, _ = w.shape
    out = jnp.zeros((nl, g * c, g * c), w.dtype)
    for i in range(g):
        out = out.at[:, i * c:(i + 1) * c, i * c:(i + 1) * c].set(w[:, i])
    return out


def kernel(x, mem, positions, rel_bias, norm_mix, w_in, w_out, conv_dw, conv_dw_b, conv_ln_g, conv_ln_b, conv_pw, conv_pw_b, mla_q_norm, mla_w_uq, mla_kv_norm, mla_w_ukv, diff_lq1, diff_lk1, diff_lq2, diff_lk2, diff_subln, pool_w, pool_scale, norm_xa, mem_norm, xa_wq, xa_wk, xa_wv, xa_wo, norm_ffn, ffn_up, ffn_dw, ffn_dw_b, ffn_down, norm_final):
    depth = w_in.shape[0]
    s = x.shape[1]
    rows = lambda a: a[:, None, :]
    bf = lambda a: a.astype(BF16)
    ctab, stab = _rope_tables(positions)
    bias_tiles = _bias_tiles(rel_bias, min(DIFF_TILE, s), min(DIFF_TILE, s) // 2)
    kmem, vmem = _mem_kv(mem, rows(mem_norm), bf(xa_wk), bf(xa_wv))
    group_mla = _lane_groups(MLA_HEADS * LANES, LANES)
    group_diff = _lane_groups(2 * DIFF_HEADS * DIFF_HALF, DIFF_HALF)
    bias_hi, bias_span = _bias_range(rel_bias)
    wuq, wuqs = _pack_w_uq(mla_w_uq)
    mix_params = [rows(norm_mix), _pack_w_in(bf(w_in)), conv_dw, rows(conv_dw_b), rows(conv_ln_g),
                  rows(conv_ln_b), bf(conv_pw), rows(conv_pw_b),
                  rows(jnp.pad(mla_q_norm, ((0, 0), (0, 256 - MLA_Q_RANK)))), wuq, wuqs,
                  rows(mla_kv_norm), _pack_w_ukv(mla_w_ukv),
                  bf(_block_diag(pool_w)), rows(pool_scale)]
    lq1, lk1, lq2, lk2 = rows(diff_lq1), rows(diff_lk1), rows(diff_lq2), rows(diff_lk2)
    subln = rows(jnp.tile(diff_subln, (1, DIFF_HEADS)))
    w_out, xa_wq, xa_wo, ffn_up, ffn_down = bf(w_out), bf(xa_wq), bf(xa_wo), bf(ffn_up), bf(ffn_down)
    g_xa, g_ffn, ffn_dw_b = rows(norm_xa), rows(norm_ffn), rows(ffn_dw_b)
    for l in range(depth):
        ya, qm, km, vm, qd, kd, vd, yd, stats = _mix_in(l, x, mix_params, ctab, stab, group_mla,
                                                         group_diff)
        stats = jnp.max(stats, axis=1)
        yb = _mla_attn(_fixed_stabiliser_ok(stats[:, 0], stats[:, 1], 0.0), qm, km, vm, stats,
                       group_mla)
        lambda_init = 0.8 - 0.6 * math.exp(-0.3 * l)
        yc = _diff_attn(l, _fixed_stabiliser_ok(stats[:, 2], stats[:, 3], bias_span), qd, kd, vd,
                        stats, group_diff, bias_hi, bias_tiles, lq1, lk1, lq2, lk2, subln,
                        lambda_init)
        x = _xa(l, x, ya, yb, yc, yd, w_out, g_xa, xa_wq, kmem, vmem, xa_wo)
        x = _ffn(l, x, g_ffn, ffn_up, ffn_dw, ffn_dw_b, ffn_down, norm_final.reshape(1, -1),
                 l == depth - 1)
    return x
```

```python
import functools
import math

import jax
import jax.numpy as jnp
from jax import lax
from jax.experimental import pallas as pl
from jax.experimental.pallas import tpu as pltpu

N_GROUPS = 4
CONV_KERNEL = 31
MLA_HEADS = 4
MLA_NOPE = 64
MLA_ROPE = 32
MLA_V = 64
MLA_Q_RANK = 192
MLA_KV_RANK = 128
DIFF_HEADS = 4
DIFF_HALF = 32
DIFF_V = 64
POOL_WINDOWS = (2, 4, 8, 16)
POOL_CH = 64
REL_BUCKETS = 32
REL_MAX_DIST = 128
XA_HEADS = 4
FFN_CONV = 3
ROPE_BASE = 10000.0
NEG_INF = -1e30
LOG2_E = math.log2(math.e)
NORM_MARGIN = 1.02
SAFE_LOGIT_SPAN = 100.0

LANES = 128
SUBLANES = 8
VMEM_LIMIT_BYTES = 56 * 1024 * 1024

TOKEN_TILE = 512
MIX_TILE = 1024
DIFF_TILE = 512
MLA_TILE = 1024
MLA_FAR_KEYS = 1024
MLA_DIAG_SPLIT = 2
CONV_HALO = 32
POOL_HALO = 16
ROW_CHUNK = 64
FFN_CHUNK = 1024

BF16 = jnp.bfloat16
F32 = jnp.float32


def _dot(a, b):
    return jnp.dot(a, b, preferred_element_type=F32)


def _dot_nt(a, b):
    return lax.dot_general(a, b, (((1,), (1,)), ((), ())), preferred_element_type=F32)


def _sigmoid(x):
    return 1.0 / (1.0 + jnp.exp(-x))


def _const_spec(shape):
    nd = len(shape)
    return pl.BlockSpec(shape, lambda *_: (0,) * nd)


def _layer_spec(stacked, layer):
    tail = stacked.shape[1:]
    return pl.BlockSpec((None,) + tail, lambda *_: (layer,) + (0,) * len(tail))


_A0, _A1 = 0, 512
_B0, _B1 = 512, 1280
_C0, _C1 = 1280, 2048
_D0, _D1 = 2048, 2304
W_IN_COLS = 2304


def _to_group_order(val, perm_ref, groups):
    ntile = val.shape[1] // LANES
    pitch = groups + SUBLANES // 2
    for c in range(ntile):
        for i in range(SUBLANES):
            perm_ref[c, i * pitch:i * pitch + groups, :] = val[i * groups:(i + 1) * groups,
                                                                 c * LANES:(c + 1) * LANES]
    return [jnp.concatenate([perm_ref[c, pl.ds(j, SUBLANES, stride=pitch), :]
                             for c in range(ntile)], axis=1) for j in range(groups)]


def _to_row_order(val, perm_ref, groups):
    ntile = val.shape[1] // LANES
    pitch = groups + SUBLANES // 2
    for j in range(groups):
        for c in range(ntile):
            perm_ref[c, pl.ds(j, SUBLANES, stride=pitch), :] = val[
                j * SUBLANES:(j + 1) * SUBLANES, c * LANES:(c + 1) * LANES]
    return jnp.concatenate(
        [jnp.concatenate([perm_ref[c, i * pitch:i * pitch + groups, :] for i in range(SUBLANES)],
                         axis=0) for c in range(ntile)], axis=1)


def _fill_window(grps, ext_ref, prev_ref, halo):
    groups = len(grps)
    first_sublane = lax.broadcasted_iota(jnp.int32, grps[0].shape, 0) == 0
    for m in range(groups - halo, groups):
        rows = slice((m - groups + halo) * SUBLANES, (m - groups + halo + 1) * SUBLANES)
        ext_ref[rows, :] = jnp.where(first_sublane, pltpu.roll(prev_ref[rows, :], 1, 0),
                                     pltpu.roll(grps[m], 1, 0))
        prev_ref[rows, :] = grps[m]
    for j in range(groups):
        ext_ref[(halo + j) * SUBLANES:(halo + j + 1) * SUBLANES, :] = grps[j]


def _mix_in_kernel(x_ref, g_ref, win_ref, dw_ref, dwb_ref, lng_ref, lnb_ref, pw_ref, pwb_ref,
                   qn_ref, wuq_ref, wuqs_ref, kvn_ref, wukv_ref, ct_ref, st_ref,
                   poolw_ref, pools_ref, gm_ref, gd_ref,
                   ya_ref, qm_ref, km_ref, vm_ref, qd_ref, kd_ref, vd_ref, yd_ref, stats_ref,
                   ext_a, prev_a, zbuf, ext_d, prev_d, dbuf, perm_a, perm_ya, perm_d, perm_yd, *, tm):
    s = pl.program_id(1)
    groups = tm // SUBLANES

    def max_sq_norm(v, group_ref):
        vf = v.astype(F32)
        return jnp.max(_dot((vf * vf).astype(BF16), group_ref[...]), axis=0, keepdims=True)

    @pl.when(s == 0)
    def _():
        prev_a[...] = jnp.zeros(prev_a.shape, F32)
        prev_d[...] = jnp.zeros(prev_d.shape, F32)

    x = x_ref[0]
    h = x * lax.rsqrt(jnp.mean(x * x, axis=-1, keepdims=True) + 1e-6) * g_ref[...]
    hb = h.astype(BF16)

    pa = _dot(hb, win_ref[:, _A0:_A1])
    pb = _dot(hb, win_ref[:, _B0:_B1])
    pc = _dot(hb, win_ref[:, _C0:_C1])
    ud = _dot(hb, win_ref[:, _D0:_D1])

    glu = pa[:, 0:256] * _sigmoid(pa[:, 256:512])
    _fill_window(_to_group_order(glu, perm_a, groups), ext_a, prev_a, CONV_HALO)
    for c in range(tm // ROW_CHUNK):
        r0 = c * ROW_CHUNK
        acc = jnp.broadcast_to(dwb_ref[...], (ROW_CHUNK, 256))
        for k in range(CONV_KERNEL):
            off = r0 + (CONV_HALO - (CONV_KERNEL - 1) + k) * SUBLANES
            acc = acc + dw_ref[k:k + 1, :] * ext_a[off:off + ROW_CHUNK, :]
        mu = jnp.mean(acc, axis=-1, keepdims=True)
        cen = acc - mu
        var = jnp.mean(cen * cen, axis=-1, keepdims=True)
        yn = cen * lax.rsqrt(var + 1e-5) * lng_ref[...] + lnb_ref[...]
        zbuf[r0:r0 + ROW_CHUNK, :] = (yn * _sigmoid(yn)).astype(BF16)
    ya = _to_row_order(_dot(zbuf[...], pw_ref[...]), perm_ya, groups)
    ya_ref[0] = (ya + pwb_ref[...]).astype(BF16)

    ct = ct_ref[...]
    st = st_ref[...]
    cq = pb[:, 0:256]
    cqn = cq * lax.rsqrt(jnp.sum(cq * cq, axis=-1, keepdims=True) * (1.0 / MLA_Q_RANK) + 1e-6)
    cqb = (cqn * qn_ref[...]).astype(BF16)
    q = _dot(cqb, wuq_ref[...])
    qs = _dot(cqb, wuqs_ref[...])
    ct4 = jnp.concatenate([ct] * MLA_HEADS, axis=1)
    st4 = jnp.concatenate([st] * MLA_HEADS, axis=1)
    q_scale = LOG2_E * (MLA_NOPE + MLA_ROPE) ** -0.5
    qm = ((q * ct4 + qs * st4) * q_scale).astype(BF16)
    qm_ref[0] = qm
    ckv = pb[:, 256:384]
    ckvn = ckv * lax.rsqrt(jnp.mean(ckv * ckv, axis=-1, keepdims=True) + 1e-6)
    kv = _dot((ckvn * kvn_ref[...]).astype(BF16), wukv_ref[...])
    kr = pb[:, 384:512] * ct + pb[:, 512:640] * st
    km = (kv[:, 0:512] + jnp.concatenate([kr] * MLA_HEADS, axis=1)).astype(BF16)
    km_ref[0] = km
    vm_ref[0] = kv[:, 512:768].astype(BF16)

    qd = (pc[:, 0:256] * (LOG2_E * DIFF_HALF ** -0.5)).astype(BF16)
    kd = pc[:, 256:512].astype(BF16)
    qd_ref[0] = qd
    kd_ref[0] = kd
    stats_ref[0, 0] = jnp.concatenate(
        [max_sq_norm(qm, gm_ref), max_sq_norm(km, gm_ref),
         max_sq_norm(qd, gd_ref), max_sq_norm(kd, gd_ref),
         jnp.zeros((SUBLANES - 4, LANES), F32)], axis=0)
    vd_ref[0] = pc[:, 512:768].astype(BF16)

    _fill_window(_to_group_order(ud, perm_d, groups), ext_d, prev_d, POOL_HALO)
    lane = lax.broadcasted_iota(jnp.int32, (ROW_CHUNK, 256), 1)
    grp = lane // POOL_CH
    win_len = jnp.where(grp == 0, POOL_WINDOWS[0],
                        jnp.where(grp == 1, POOL_WINDOWS[1],
                                  jnp.where(grp == 2, POOL_WINDOWS[2], POOL_WINDOWS[3])))
    for c in range(tm // ROW_CHUNK):
        r0 = c * ROW_CHUNK
        base = r0 + POOL_HALO * SUBLANES
        u = ext_d[base:base + ROW_CHUNK, :]
        run = u
        sums = {}
        for j in range(1, POOL_WINDOWS[-1]):
            run = run + ext_d[base - j * SUBLANES:base - j * SUBLANES + ROW_CHUNK, :]
            if j + 1 in POOL_WINDOWS:
                sums[j + 1] = run
        win = jnp.where(grp == 0, sums[2],
                        jnp.where(grp == 1, sums[4], jnp.where(grp == 2, sums[8], sums[16])))
        r = r0 + lax.broadcasted_iota(jnp.int32, (ROW_CHUNK, 256), 0)
        t_glob = s * tm + (r % SUBLANES) * groups + r // SUBLANES
        cnt = jnp.minimum(t_glob + 1, win_len).astype(F32)
        dbuf[r0:r0 + ROW_CHUNK, :] = (win / cnt - u).astype(BF16)
    yd = _to_row_order(_dot(dbuf[...], poolw_ref[...]), perm_yd, groups)
    yd_ref[0] = (yd * pools_ref[...]).astype(BF16)


def _mix_in(layer, x, per_layer, ctab, stab, group_mla, group_diff):
    b, s, d = x.shape
    tm = min(MIX_TILE, s)
    tok = lambda w: pl.BlockSpec((1, tm, w), lambda bi, si: (bi, si, 0))
    head, tail = per_layer[:13], per_layer[13:]
    in_specs = ([tok(d)] + [_layer_spec(a, layer) for a in head]
                + [pl.BlockSpec((tm, LANES), lambda bi, si: (si, 0))] * 2
                + [_layer_spec(a, layer) for a in tail]
                + [_const_spec(group_mla.shape), _const_spec(group_diff.shape)])
    widths = (256, 512, 512, 256, 256, 256, 256, 256)
    stats_spec = pl.BlockSpec((1, 1, SUBLANES, LANES), lambda bi, si: (bi, si, 0, 0))
    return pl.pallas_call(
        functools.partial(_mix_in_kernel, tm=tm),
        grid=(b, s // tm),
        in_specs=in_specs,
        out_specs=[tok(w) for w in widths] + [stats_spec],
        out_shape=([jax.ShapeDtypeStruct((b, s, w), BF16) for w in widths]
                   + [jax.ShapeDtypeStruct((b, s // tm, SUBLANES, LANES), F32)]),
        scratch_shapes=[pltpu.VMEM((CONV_HALO * SUBLANES + tm, 256), F32),
                        pltpu.VMEM((CONV_HALO * SUBLANES, 256), F32),
                        pltpu.VMEM((tm, 256), BF16),
                        pltpu.VMEM((POOL_HALO * SUBLANES + tm, 256), F32),
                        pltpu.VMEM((POOL_HALO * SUBLANES, 256), F32),
                        pltpu.VMEM((tm, 256), BF16)]
                       + [pltpu.VMEM((2, tm + SUBLANES * SUBLANES, LANES), F32)] * 4,
        compiler_params=pltpu.CompilerParams(
            dimension_semantics=("arbitrary", "arbitrary"),
            vmem_limit_bytes=VMEM_LIMIT_BYTES),
        name="mix_in",
    )(x, *head, ctab, stab, *tail, group_mla, group_diff)


def _head_lanes(vals, rows):
    lane = lax.broadcasted_iota(jnp.int32, (rows, LANES), 1)
    return jnp.concatenate([jnp.where(lane < 64, vals[0], vals[1]),
                            jnp.where(lane < 64, vals[2], vals[3])], axis=1)


def _row_total(l_ref, idx):
    return jnp.sum(l_ref[idx], axis=-1, keepdims=True)


def _stack_masked_v(v_blk):
    n = v_blk.shape[0]
    lane = lax.broadcasted_iota(jnp.int32, (n, 256), 1)
    parts = []
    for h in range(4):
        keep = (lane >= 64 * h) & (lane < 64 * (h + 1))
        parts.append(jnp.where(keep, v_blk, jnp.zeros_like(v_blk)))
    return jnp.concatenate(parts, axis=0)


def _softmax_step(sc, m_ref, l_ref, idx, online, rows=slice(None)):
    if online:
        m_old = m_ref[idx, rows]
        m_new = jnp.maximum(m_old, jnp.max(sc, axis=-1, keepdims=True))
        alpha = jnp.exp2(m_old - m_new)
        m_ref[idx, rows] = m_new
    else:
        m_new = m_ref[idx, rows]
        alpha = None
    reps = sc.shape[1] // LANES
    p = jnp.exp2(sc - jnp.concatenate([m_new] * reps, axis=1))
    psum = p[:, 0:LANES]
    for r in range(1, reps):
        psum = psum + p[:, r * LANES:(r + 1) * LANES]
    l_ref[idx, rows] = (alpha * l_ref[idx, rows] if online else l_ref[idx, rows]) + psum
    return p, alpha


def _logit_bounds(q, group_ref, kmax_sq):
    qf = q.astype(F32)
    qn_sq = _dot((qf * qf).astype(BF16), group_ref[...])
    return NORM_MARGIN * jnp.sqrt(qn_sq * kmax_sq)


def _mla_attn_kernel(safe_ref, q_ref, k_ref, v_ref, stats_ref, group_ref, o_ref,
                     m_ref, l_ref, acc_ref, *, t, far):
    qi = pl.program_id(1)
    l_ref[...] = jnp.zeros(l_ref.shape, F32)
    acc_ref[...] = jnp.zeros(acc_ref.shape, F32)

    def run(online):
        if online:
            m_ref[...] = jnp.full(m_ref.shape, NEG_INF, F32)
        else:
            bounds = _logit_bounds(q_ref[0], group_ref, stats_ref[0, 1:2, :])
            for h in range(MLA_HEADS):
                m_ref[h] = jnp.broadcast_to(bounds[:, h:h + 1], (t, LANES))

        def step(k0, n, r0=0, diagonal=False):
            nr = t - r0
            rows = slice(r0, t)
            vst = _stack_masked_v(v_ref[0, pl.ds(k0, n), :])
            probs, alphas = [], []
            for h in range(MLA_HEADS):
                cols = slice(LANES * h, LANES * (h + 1))
                sc = _dot_nt(q_ref[0, rows, cols], k_ref[0, pl.ds(k0, n), cols])
                if diagonal:
                    row = lax.broadcasted_iota(jnp.int32, (nr, n), 0)
                    col = lax.broadcasted_iota(jnp.int32, (nr, n), 1)
                    sc = jnp.where(row >= col, sc, NEG_INF)
                p, alpha = _softmax_step(sc, m_ref, l_ref, h, online, rows)
                probs.append(p.astype(BF16))
                alphas.append(alpha)
            pv = _dot(jnp.concatenate(probs, axis=1), vst)
            old_acc = acc_ref[rows, :] * _head_lanes(alphas, nr) if online else acc_ref[rows, :]
            acc_ref[rows, :] = old_acc + pv

        def far_body(j, carry):
            step(pl.multiple_of(j * far, far), far)
            return carry

        def near_body(j, carry):
            step(pl.multiple_of(j * t, t), t)
            return carry

        per_far = far // t
        n_far = qi // per_far
        lax.fori_loop(0, n_far, far_body, 0)
        lax.fori_loop(n_far * per_far, qi, near_body, 0)
        strip = t // MLA_DIAG_SPLIT
        for j in range(MLA_DIAG_SPLIT):
            step(pl.multiple_of(qi * t + j * strip, strip), strip, j * strip, True)

    safe = safe_ref[0] != 0
    pl.when(safe)(functools.partial(run, False))
    pl.when(safe_ref[0] == 0)(functools.partial(run, True))
    inv_l = _head_lanes([1.0 / _row_total(l_ref, h) for h in range(MLA_HEADS)], t)
    o_ref[0] = (acc_ref[...] * inv_l).astype(BF16)


def _mla_attn(safe, q, k, v, stats, group):
    b, s, _ = q.shape
    t = min(MLA_TILE, s)
    far = min(MLA_FAR_KEYS, s)
    return pl.pallas_call(
        functools.partial(_mla_attn_kernel, t=t, far=far),
        grid=(b, s // t),
        in_specs=[pl.BlockSpec(memory_space=pltpu.SMEM),
                  pl.BlockSpec((1, t, 512), lambda bi, qi: (bi, qi, 0)),
                  pl.BlockSpec((1, s, 512), lambda bi, qi: (bi, 0, 0)),
                  pl.BlockSpec((1, s, 256), lambda bi, qi: (bi, 0, 0)),
                  pl.BlockSpec((1, SUBLANES, LANES), lambda bi, qi: (bi, 0, 0)),
                  _const_spec(group.shape)],
        out_specs=pl.BlockSpec((1, t, 256), lambda bi, qi: (bi, qi, 0)),
        out_shape=jax.ShapeDtypeStruct((b, s, 256), BF16),
        scratch_shapes=[pltpu.VMEM((MLA_HEADS, t, LANES), F32),
                        pltpu.VMEM((MLA_HEADS, t, LANES), F32),
                        pltpu.VMEM((t, 256), F32)],
        compiler_params=pltpu.CompilerParams(
            dimension_semantics=("arbitrary", "arbitrary"),
            vmem_limit_bytes=VMEM_LIMIT_BYTES),
        name="mla_attn",
    )(safe, q, k, v, stats, group)


def _bias_tiles_kernel(rb_ref, out_ref, *, t, tk):
    row = lax.broadcasted_iota(jnp.int32, (t, tk), 0)
    col = lax.broadcasted_iota(jnp.int32, (t, tk), 1)
    max_exact = REL_BUCKETS // 2
    for j in range(t // tk + 1):
        rel = (1 - j) * tk + row - col
        n = jnp.maximum(rel, 0)
        nf = jnp.maximum(n, 1).astype(F32)
        large = max_exact + (jnp.log(nf / max_exact) / math.log(REL_MAX_DIST / max_exact)
                             * (REL_BUCKETS - max_exact)).astype(jnp.int32)
        large = jnp.minimum(large, REL_BUCKETS - 1)
        bucket = jnp.where(n < max_exact, n, large)
        for h in range(DIFF_HEADS):
            far = rb_ref[REL_BUCKETS - 1, h]
            val = jnp.zeros((t, tk), F32)
            for bkt in range(REL_BUCKETS - 1):
                val = jnp.where(bucket == bkt, (rb_ref[bkt, h] - far) * LOG2_E, val)
            out_ref[h, j] = jnp.where(rel >= 0, val, NEG_INF)


def _bias_tiles(rel_bias, t, tk):
    return pl.pallas_call(
        functools.partial(_bias_tiles_kernel, t=t, tk=tk),
        in_specs=[pl.BlockSpec(memory_space=pltpu.SMEM)],
        out_specs=pl.BlockSpec(memory_space=pltpu.VMEM),
        out_shape=jax.ShapeDtypeStruct((DIFF_HEADS, t // tk + 1, t, tk), F32),
        name="bias_tiles",
    )(rel_bias)


def _diff_attn_kernel(safe_ref, q_ref, k_ref, v_ref, stats_ref, group_ref, bias_hi_ref, bias_ref,
                      lq1_ref, lk1_ref, lq2_ref, lk2_ref, subln_ref, o_ref,
                      qst_ref, m_ref, l_ref, acc_ref, s0_ref, s1_ref, *, t, tk, lambda_init):
    qi = pl.program_id(1)
    nmaps = 2 * DIFF_HEADS
    per_tile = t // tk
    l_ref[...] = jnp.zeros(l_ref.shape, F32)
    acc_ref[...] = jnp.zeros(acc_ref.shape, F32)
    q = q_ref[0]
    lane = lax.broadcasted_iota(jnp.int32, (t, 256), 1)
    for idx in range(nmaps):
        keep = (lane >= DIFF_HALF * idx) & (lane < DIFF_HALF * (idx + 1))
        qst_ref[idx * t:(idx + 1) * t, :] = jnp.where(keep, q, jnp.zeros_like(q))

    def scores(kb, r0=0):
        lhs = qst_ref[...] if r0 == 0 else jnp.concatenate(
            [qst_ref[idx * t + r0:(idx + 1) * t, :] for idx in range(nmaps)], axis=0)
        return _dot_nt(lhs, k_ref[0, pl.ds(pl.multiple_of(kb * tk, tk), tk), :])

    def consume(sc_of, kb, bias_tile, online, r0=0):
        nr = t - r0
        rows = slice(r0, t)
        vst = _stack_masked_v(v_ref[0, pl.ds(pl.multiple_of(kb * tk, tk), tk), :])
        probs = [[], []]
        alphas = [[], []]
        for h in range(DIFF_HEADS):
            for which in range(2):
                idx = 2 * h + which
                sc = sc_of(idx)
                if bias_tile is not None:
                    sc = sc + bias_ref[h, bias_tile, rows, :]
                p, alpha = _softmax_step(sc, m_ref, l_ref, idx, online, rows)
                probs[which].append(p.astype(BF16))
                alphas[which].append(alpha)
        p_all = jnp.concatenate([jnp.concatenate(probs[0], axis=1),
                                 jnp.concatenate(probs[1], axis=1)], axis=0)
        pv = _dot(p_all, vst)
        for which in range(2):
            arows = slice(which * t + r0, (which + 1) * t)
            old = acc_ref[arows, :] * _head_lanes(alphas[which], nr) if online else acc_ref[arows, :]
            acc_ref[arows, :] = old + pv[which * nr:(which + 1) * nr, :]

    def from_ref(ref):
        return lambda idx: ref[idx * t:(idx + 1) * t, :]

    def from_val(sc_all):
        nr = sc_all.shape[0] // nmaps
        return lambda idx: sc_all[idx * nr:(idx + 1) * nr, :]

    hidden = lambda j: j * tk

    first_near = per_tile * qi - 1
    n_plain = jnp.maximum(first_near, 0)

    def run_online():
        m_ref[...] = jnp.full(m_ref.shape, NEG_INF, F32)

        def step(kb, bias_tile, r0=0):
            consume(from_val(scores(kb, r0)), kb, bias_tile, True, r0)

        def body(kb, carry):
            step(kb, None)
            return carry

        lax.fori_loop(0, n_plain, body, 0)
        pl.when(qi > 0)(lambda: step(first_near, 0))
        for j in range(per_tile):
            step(first_near + 1 + j, 1 + j, hidden(j))

    def run_fixed():
        bounds = _logit_bounds(q, group_ref, stats_ref[0, 3:4, :]) + bias_hi_ref[...]
        for idx in range(nmaps):
            m_ref[idx] = jnp.broadcast_to(bounds[:, idx:idx + 1], (t, LANES))

        def pair_body(i, carry):
            s1_ref[...] = scores(2 * i + 1)
            consume(from_ref(s0_ref), 2 * i, None, False)
            s0_ref[...] = scores(2 * i + 2)
            consume(from_ref(s1_ref), 2 * i + 1, None, False)
            return carry

        s0_ref[...] = scores(0)
        n_pairs = n_plain // 2
        lax.fori_loop(0, n_pairs, pair_body, 0)

        @pl.when(qi == 0)
        def _():
            sc1 = scores(1, hidden(1))
            consume(from_ref(s0_ref), 0, 1, False)
            consume(from_val(sc1), 1, 2, False, hidden(1))

        @pl.when(qi > 0)
        def _():
            b0 = 2 * n_pairs
            s1_ref[...] = scores(b0 + 1)
            consume(from_ref(s0_ref), b0, None, False)
            sc2 = scores(b0 + 2)
            consume(from_ref(s1_ref), b0 + 1, 0, False)
            sc3 = scores(b0 + 3, hidden(1))
            consume(from_val(sc2), b0 + 2, 1, False)
            consume(from_val(sc3), b0 + 3, 2, False, hidden(1))

    pl.when(safe_ref[0] != 0)(run_fixed)
    pl.when(safe_ref[0] == 0)(run_online)

    lam = (jnp.exp(jnp.sum(lq1_ref[...] * lk1_ref[...], axis=-1, keepdims=True))
           - jnp.exp(jnp.sum(lq2_ref[...] * lk2_ref[...], axis=-1, keepdims=True))
           + lambda_init)
    o1 = acc_ref[0:t, :] * _head_lanes([1.0 / _row_total(l_ref, 2 * h) for h in range(DIFF_HEADS)], t)
    o2 = acc_ref[t:2 * t, :] * _head_lanes(
        [1.0 / _row_total(l_ref, 2 * h + 1) for h in range(DIFF_HEADS)], t)
    o = o1 - lam * o2
    osq = o * o
    ms = []
    for h in range(DIFF_HEADS):
        keep = (lane >= DIFF_V * h) & (lane < DIFF_V * (h + 1))
        ms.append(jnp.sum(jnp.where(keep, osq, 0.0), axis=-1, keepdims=True) * (1.0 / DIFF_V))
    on = o * lax.rsqrt(_head_lanes(ms, t) + 1e-5) * subln_ref[...]
    o_ref[0] = (on * (1.0 - lambda_init)).astype(BF16)


def _diff_attn(layer, safe, q, k, v, stats, group, bias_hi, bias_tiles, lq1, lk1, lq2, lk2, subln,
               lambda_init):
    b, s, _ = q.shape
    t = min(DIFF_TILE, s)
    tk = t // 2
    return pl.pallas_call(
        functools.partial(_diff_attn_kernel, t=t, tk=tk, lambda_init=lambda_init),
        grid=(b, s // t),
        in_specs=[pl.BlockSpec(memory_space=pltpu.SMEM),
                  pl.BlockSpec((1, t, 256), lambda bi, qi: (bi, qi, 0)),
                  pl.BlockSpec((1, s, 256), lambda bi, qi: (bi, 0, 0)),
                  pl.BlockSpec((1, s, 256), lambda bi, qi: (bi, 0, 0)),
                  pl.BlockSpec((1, SUBLANES, LANES), lambda bi, qi: (bi, 0, 0)),
                  _const_spec(group.shape), _const_spec(bias_hi.shape),
                  _const_spec(bias_tiles.shape),
                  _layer_spec(lq1, layer), _layer_spec(lk1, layer),
                  _layer_spec(lq2, layer), _layer_spec(lk2, layer),
                  _layer_spec(subln, layer)],
        out_specs=pl.BlockSpec((1, t, 256), lambda bi, qi: (bi, qi, 0)),
        out_shape=jax.ShapeDtypeStruct((b, s, 256), BF16),
        scratch_shapes=[pltpu.VMEM((8 * t, 256), BF16),
                        pltpu.VMEM((2 * DIFF_HEADS, t, LANES), F32),
                        pltpu.VMEM((2 * DIFF_HEADS, t, LANES), F32),
                        pltpu.VMEM((2 * t, 256), F32),
                        pltpu.VMEM((8 * t, tk), F32),
                        pltpu.VMEM((8 * t, tk), F32)],
        compiler_params=pltpu.CompilerParams(
            dimension_semantics=("arbitrary", "arbitrary"),
            vmem_limit_bytes=VMEM_LIMIT_BYTES),
        name="diff_attn",
    )(safe, q, k, v, stats, group, bias_hi, bias_tiles, lq1, lk1, lq2, lk2, subln)


def _mem_kv_kernel(mem_ref, g_ref, wk_ref, wv_ref, k_ref, v_ref):
    m = mem_ref[0]
    mn = m * lax.rsqrt(jnp.mean(m * m, axis=-1, keepdims=True) + 1e-6) * g_ref[0]
    mb = mn.astype(BF16)
    k_ref[0, 0] = _dot(mb, wk_ref[0]).astype(BF16)
    v_ref[0, 0] = _dot(mb, wv_ref[0]).astype(BF16)


def _mem_kv(mem, g, wk, wv):
    b, m, d = mem.shape
    nl = wk.shape[0]
    out = jax.ShapeDtypeStruct((nl, b, m, d), BF16)
    return pl.pallas_call(
        _mem_kv_kernel,
        grid=(nl, b),
        in_specs=[pl.BlockSpec((1, m, d), lambda li, bi: (bi, 0, 0)),
                  pl.BlockSpec((1, 1, d), lambda li, bi: (li, 0, 0)),
                  pl.BlockSpec((1, d, d), lambda li, bi: (li, 0, 0)),
                  pl.BlockSpec((1, d, d), lambda li, bi: (li, 0, 0))],
        out_specs=[pl.BlockSpec((1, 1, m, d), lambda li, bi: (li, bi, 0, 0))] * 2,
        out_shape=[out, out],
        compiler_params=pltpu.CompilerParams(
            dimension_semantics=("arbitrary", "arbitrary"),
            vmem_limit_bytes=VMEM_LIMIT_BYTES),
        name="mem_kv",
    )(mem, g, wk, wv)


def _xa_kernel(x_ref, ya_ref, yb_ref, yc_ref, yd_ref, wout_ref, g_ref, wq_ref, km_ref, vm_ref,
               wo_ref, o_ref, obuf):
    x1 = x_ref[0]
    for i, y_ref in enumerate((ya_ref, yb_ref, yc_ref, yd_ref)):
        x1 = x1 + _dot(y_ref[0], wout_ref[256 * i:256 * (i + 1), :])
    hx = x1 * lax.rsqrt(jnp.mean(x1 * x1, axis=-1, keepdims=True) + 1e-6) * g_ref[...]
    hd = wq_ref.shape[1] // XA_HEADS
    q = (_dot(hx.astype(BF16), wq_ref[...]) * (hd ** -0.5)).astype(BF16)
    for h in range(XA_HEADS):
        cols = slice(hd * h, hd * (h + 1))
        sc = _dot_nt(q[:, cols], km_ref[0, :, cols])
        p = jnp.exp(sc - jnp.max(sc, axis=-1, keepdims=True))
        oh = _dot(p.astype(BF16), vm_ref[0, :, cols]) / jnp.sum(p, axis=-1, keepdims=True)
        obuf[:, cols] = oh.astype(BF16)
    o_ref[0] = x1 + _dot(obuf[...], wo_ref[...])


def _xa(layer, x, ya, yb, yc, yd, wout, g, wq, kmem, vmem, wo):
    b, s, d = x.shape
    tm = min(TOKEN_TILE, s)
    m = kmem.shape[2]
    tok = lambda w: pl.BlockSpec((1, tm, w), lambda bi, si: (bi, si, 0))
    mem_spec = pl.BlockSpec((None, 1, m, d), lambda bi, si: (layer, bi, 0, 0))
    return pl.pallas_call(
        _xa_kernel,
        grid=(b, s // tm),
        in_specs=[tok(d), tok(256), tok(256), tok(256), tok(256),
                  _layer_spec(wout, layer), _layer_spec(g, layer), _layer_spec(wq, layer),
                  mem_spec, mem_spec, _layer_spec(wo, layer)],
        out_specs=tok(d),
        out_shape=jax.ShapeDtypeStruct((b, s, d), F32),
        scratch_shapes=[pltpu.VMEM((tm, d), BF16)],
        compiler_params=pltpu.CompilerParams(
            dimension_semantics=("arbitrary", "arbitrary"),
            vmem_limit_bytes=VMEM_LIMIT_BYTES),
        name="xa",
    )(x, ya, yb, yc, yd, wout, g, wq, kmem, vmem, wo)


def _ffn_kernel(x_ref, g_ref, wup_ref, dw_ref, dwb_ref, wdown_ref, gfin_ref, o_ref,
                carry, acc_ref, perm_in, perm_out, *, tm, dff, final_norm):
    s = pl.program_id(1)
    groups = tm // SUBLANES

    @pl.when(s == 0)
    def _():
        carry[...] = jnp.zeros(carry.shape, F32)

    x = x_ref[0]
    h = x * lax.rsqrt(jnp.mean(x * x, axis=-1, keepdims=True) + 1e-6) * g_ref[...]
    hb = jnp.concatenate(_to_group_order(h, perm_in, groups), axis=0).astype(BF16)
    bounds = list(range(0, dff, FFN_CHUNK)) + [dff]
    nchunk = len(bounds) - 1

    def conv(u, slot, col0):
        width = u.shape[1]
        first_sublane = lax.broadcasted_iota(jnp.int32, (SUBLANES, width), 0) == 0
        cols = slice(col0, col0 + width)
        wcols = slice(slot * dff + col0, slot * dff + col0 + width)
        tail = u[tm - 2 * SUBLANES:tm, :]
        prev = carry[slot, :, cols]
        carry[slot, :, cols] = tail

        def wrap(k):
            lo, hi = k * SUBLANES, (k + 1) * SUBLANES
            return jnp.where(first_sublane, pltpu.roll(prev[lo:hi, :], 1, 0),
                             pltpu.roll(tail[lo:hi, :], 1, 0))

        back1 = wrap(1)
        u1 = jnp.concatenate([back1, u[0:tm - SUBLANES, :]], axis=0)
        u2 = jnp.concatenate([wrap(0), back1, u[0:tm - 2 * SUBLANES, :]], axis=0)
        w = dw_ref[:, wcols]
        return dwb_ref[:, wcols] + w[2:3, :] * u + w[1:2, :] * u1 + w[0:1, :] * u2

    def up(j):
        c0, c1 = bounds[j], bounds[j + 1]
        return (_dot(hb, wup_ref[:, c0:c1]), _dot(hb, wup_ref[:, dff + c0:dff + c1]))

    ua, ug = up(0)
    for j in range(nchunk):
        col0 = bounds[j]
        nxt = up(j + 1) if j + 1 < nchunk else None
        a = conv(ua, 0, col0)
        gt = conv(ug, 1, col0)
        act = (gt * _sigmoid(gt) * a).astype(BF16)
        part = _dot(act, wdown_ref[bounds[j]:bounds[j + 1], :])
        if j == 0:
            acc_ref[...] = part
        else:
            acc_ref[...] += part
        if nxt is not None:
            ua, ug = nxt
    y = x + _to_row_order(acc_ref[...], perm_out, groups)
    if final_norm:
        y = y * lax.rsqrt(jnp.mean(y * y, axis=-1, keepdims=True) + 1e-6) * gfin_ref[...]
    o_ref[0] = y


def _ffn(layer, x, g, wup, dw, dwb, wdown, gfin, final_norm):
    b, s, d = x.shape
    tm = min(TOKEN_TILE, s)
    dff = wdown.shape[1]
    tok = pl.BlockSpec((1, tm, d), lambda bi, si: (bi, si, 0))
    return pl.pallas_call(
        functools.partial(_ffn_kernel, tm=tm, dff=dff, final_norm=final_norm),
        grid=(b, s // tm),
        in_specs=[tok, _layer_spec(g, layer), _layer_spec(wup, layer), _layer_spec(dw, layer),
                  _layer_spec(dwb, layer), _layer_spec(wdown, layer), _const_spec(gfin.shape)],
        out_specs=tok,
        out_shape=jax.ShapeDtypeStruct((b, s, d), F32),
        scratch_shapes=[pltpu.VMEM((2, 2 * SUBLANES, dff), F32),
                        pltpu.VMEM((tm, d), F32),
                        pltpu.VMEM((d // LANES, tm + SUBLANES * SUBLANES, LANES), F32),
                        pltpu.VMEM((d // LANES, tm + SUBLANES * SUBLANES, LANES), F32)],
        compiler_params=pltpu.CompilerParams(
            dimension_semantics=("arbitrary", "arbitrary"),
            vmem_limit_bytes=VMEM_LIMIT_BYTES),
        name="ffn",
    )(x, g, wup, dw, dwb, wdown, gfin)


def _pack_w_in(w):
    ua, cq, ckv, kr, qc, kc, vc, ud = jnp.split(
        w, [512, 704, 832, 864, 1120, 1376, 1632], axis=-1)
    z = lambda n: jnp.zeros(w.shape[:-1] + (n,), w.dtype)
    half = MLA_ROPE // 2
    kr_sw = jnp.concatenate([kr[..., half:], kr[..., :half]], axis=-1)
    kr_blk = jnp.concatenate([z(MLA_NOPE), kr, z(LANES - MLA_NOPE - MLA_ROPE)], axis=-1)
    kr_sw_blk = jnp.concatenate([z(MLA_NOPE), kr_sw, z(LANES - MLA_NOPE - MLA_ROPE)], axis=-1)
    packed = jnp.concatenate(
        [ua, cq, z(256 - MLA_Q_RANK), ckv, kr_blk, kr_sw_blk, z(LANES), qc, kc, vc, ud], axis=-1)
    return packed


def _pack_w_uq(w):
    nl = w.shape[0]
    w = w.reshape(nl, MLA_Q_RANK, MLA_HEADS, MLA_NOPE + MLA_ROPE)
    nope, rp = w[..., :MLA_NOPE], w[..., MLA_NOPE:]
    half = MLA_ROPE // 2
    rp_sw = jnp.concatenate([rp[..., half:], rp[..., :half]], axis=-1)
    zn = jnp.zeros_like(nope)
    zp = jnp.zeros((nl, MLA_Q_RANK, MLA_HEADS, LANES - MLA_NOPE - MLA_ROPE), w.dtype)

    def fin(a):
        a = a.reshape(nl, MLA_Q_RANK, MLA_HEADS * LANES)
        return jnp.pad(a, ((0, 0), (0, 256 - MLA_Q_RANK), (0, 0))).astype(BF16)

    return (fin(jnp.concatenate([nope, rp, zp], axis=-1)),
            fin(jnp.concatenate([zn, rp_sw, zp], axis=-1)))


def _pack_w_ukv(w):
    nl = w.shape[0]
    w = w.reshape(nl, MLA_KV_RANK, MLA_HEADS, MLA_NOPE + MLA_V)
    kn, v = w[..., :MLA_NOPE], w[..., MLA_NOPE:]
    kn = jnp.concatenate([kn, jnp.zeros_like(kn)], axis=-1).reshape(nl, MLA_KV_RANK, MLA_HEADS * LANES)
    v = v.reshape(nl, MLA_KV_RANK, MLA_HEADS * MLA_V)
    return jnp.concatenate([kn, v], axis=-1).astype(BF16)


def _rope_tables(positions):
    inv_freq = ROPE_BASE ** (-jnp.arange(0, MLA_ROPE, 2, dtype=F32) / MLA_ROPE)
    ang = positions.astype(F32)[:, None] * inv_freq[None, :]
    cos, sin = jnp.cos(ang), jnp.sin(ang)
    s = positions.shape[0]
    pad = jnp.zeros((s, LANES - MLA_NOPE - MLA_ROPE), F32)
    ctab = jnp.concatenate([jnp.ones((s, MLA_NOPE), F32), cos, cos, pad], axis=1)
    stab = jnp.concatenate([jnp.zeros((s, MLA_NOPE), F32), -sin, sin, pad], axis=1)
    return ctab, stab


def _lane_groups(n_lanes, width):
    lane = jnp.arange(n_lanes)[:, None] // width
    return (lane == jnp.arange(LANES)[None, :]).astype(BF16)


def _bias_range(rel_bias):
    shifted = (rel_bias - rel_bias[REL_BUCKETS - 1]) * LOG2_E
    hi = jnp.repeat(jnp.max(shifted, axis=0), 2)
    lo = jnp.repeat(jnp.min(shifted, axis=0), 2)
    pad = (0, LANES - 2 * DIFF_HEADS)
    return jnp.pad(hi, pad).reshape(1, LANES), jnp.pad(hi - lo, pad).reshape(1, LANES)


def _fixed_stabiliser_ok(qmax_sq, kmax_sq, extra_span):
    bound = NORM_MARGIN * jnp.sqrt(qmax_sq * kmax_sq)
    ok = jnp.all(2.0 * bound + extra_span <= SAFE_LOGIT_SPAN)
    return ok.astype(jnp.int32).reshape(1)


def _block_diag(w):
    nl, g, c, _ = w.shape
    out = jnp.zeros((nl, g * c, g * c), w.dtype)
    for i in range(g):
        out = out.at[:, i * c:(i + 1) * c, i * c:(i + 1) * c].set(w[:, i])
    return out


def kernel(x, mem, positions, rel_bias, norm_mix, w_in, w_out, conv_dw, conv_dw_b, conv_ln_g, conv_ln_b, conv_pw, conv_pw_b, mla_q_norm, mla_w_uq, mla_kv_norm, mla_w_ukv, diff_lq1, diff_lk1, diff_lq2, diff_lk2, diff_subln, pool_w, pool_scale, norm_xa, mem_norm, xa_wq, xa_wk, xa_wv, xa_wo, norm_ffn, ffn_up, ffn_dw, ffn_dw_b, ffn_down, norm_final):
    depth = w_in.shape[0]
    s = x.shape[1]
    rows = lambda a: a[:, None, :]
    bf = lambda a: a.astype(BF16)
    ctab, stab = _rope_tables(positions)
    bias_tiles = _bias_tiles(rel_bias, min(DIFF_TILE, s), min(DIFF_TILE, s) // 2)
    kmem, vmem = _mem_kv(mem, rows(mem_norm), bf(xa_wk), bf(xa_wv))
    group_mla = _lane_groups(MLA_HEADS * LANES, LANES)
    group_diff = _lane_groups(2 * DIFF_HEADS * DIFF_HALF, DIFF_HALF)
    bias_hi, bias_span = _bias_range(rel_bias)
    wuq, wuqs = _pack_w_uq(mla_w_uq)
    mix_params = [rows(norm_mix), _pack_w_in(bf(w_in)), conv_dw, rows(conv_dw_b), rows(conv_ln_g),
                  rows(conv_ln_b), bf(conv_pw), rows(conv_pw_b),
                  rows(jnp.pad(mla_q_norm, ((0, 0), (0, 256 - MLA_Q_RANK)))), wuq, wuqs,
                  rows(mla_kv_norm), _pack_w_ukv(mla_w_ukv),
                  bf(_block_diag(pool_w)), rows(pool_scale)]
    lq1, lk1, lq2, lk2 = rows(diff_lq1), rows(diff_lk1), rows(diff_lq2), rows(diff_lk2)
    subln = rows(jnp.tile(diff_subln, (1, DIFF_HEADS)))
    w_out, xa_wq, xa_wo, ffn_up, ffn_down = bf(w_out), bf(xa_wq), bf(xa_wo), bf(ffn_up), bf(ffn_down)
    g_xa, g_ffn, ffn_dw_b = rows(norm_xa), rows(norm_ffn), rows(ffn_dw_b)
    for l in range(depth):
        ya, qm, km, vm, qd, kd, vd, yd, stats = _mix_in(l, x, mix_params, ctab, stab, group_mla,
                                                         group_diff)
        stats = jnp.max(stats, axis=1)
        yb = _mla_attn(_fixed_stabiliser_ok(stats[:, 0], stats[:, 1], 0.0), qm, km, vm, stats,
                       group_mla)
        lambda_init = 0.8 - 0.6 * math.exp(-0.3 * l)
        yc = _diff_attn(l, _fixed_stabiliser_ok(stats[:, 2], stats[:, 3], bias_span), qd, kd, vd,
                        stats, group_diff, bias_hi, bias_tiles, lq1, lk1, lq2, lk2, subln,
                        lambda_init)
        x = _xa(l, x, ya, yb, yc, yd, w_out, g_xa, xa_wq, kmem, vmem, xa_wo)
        x = _ffn(l, x, g_ffn, ffn_up, ffn_dw, ffn_dw_b, ffn_down, norm_final.reshape(1, -1),
                 l == depth - 1)
    return x
```

```python
import functools
import math

import jax
import jax.numpy as jnp
from jax import lax
from jax.experimental import pallas as pl
from jax.experimental.pallas import tpu as pltpu

N_GROUPS = 4
CONV_KERNEL = 31
MLA_HEADS = 4
MLA_NOPE = 64
MLA_ROPE = 32
MLA_V = 64
MLA_Q_RANK = 192
MLA_KV_RANK = 128
DIFF_HEADS = 4
DIFF_HALF = 32
DIFF_V = 64
POOL_WINDOWS = (2, 4, 8, 16)
POOL_CH = 64
REL_BUCKETS = 32
REL_MAX_DIST = 128
XA_HEADS = 4
FFN_CONV = 3
ROPE_BASE = 10000.0
NEG_INF = -1e30
LOG2_E = math.log2(math.e)
NORM_MARGIN = 1.02
SAFE_LOGIT_SPAN = 100.0

LANES = 128
SUBLANES = 8
VMEM_LIMIT_BYTES = 56 * 1024 * 1024

TOKEN_TILE = 512
MIX_TILE = 1024
DIFF_TILE = 512
MLA_TILE = 1024
MLA_FAR_KEYS = 1024
MLA_DIAG_SPLIT = 2
CONV_HALO = 32
POOL_HALO = 16
ROW_CHUNK = 64
FFN_CHUNK = 1024

BF16 = jnp.bfloat16
F32 = jnp.float32


def _dot(a, b):
    return jnp.dot(a, b, preferred_element_type=F32)


def _dot_nt(a, b):
    return lax.dot_general(a, b, (((1,), (1,)), ((), ())), preferred_element_type=F32)


def _sigmoid(x):
    return 1.0 / (1.0 + jnp.exp(-x))


def _const_spec(shape):
    nd = len(shape)
    return pl.BlockSpec(shape, lambda *_: (0,) * nd)


def _layer_spec(stacked, layer):
    tail = stacked.shape[1:]
    return pl.BlockSpec((None,) + tail, lambda *_: (layer,) + (0,) * len(tail))


_A0, _A1 = 0, 512
_B0, _B1 = 512, 1280
_C0, _C1 = 1280, 2048
_D0, _D1 = 2048, 2304
W_IN_COLS = 2304


def _to_group_order(val, perm_ref, groups):
    ntile = val.shape[1] // LANES
    pitch = groups + SUBLANES // 2
    for c in range(ntile):
        for i in range(SUBLANES):
            perm_ref[c, i * pitch:i * pitch + groups, :] = val[i * groups:(i + 1) * groups,
                                                                 c * LANES:(c + 1) * LANES]
    return [jnp.concatenate([perm_ref[c, pl.ds(j, SUBLANES, stride=pitch), :]
                             for c in range(ntile)], axis=1) for j in range(groups)]


def _to_row_order(val, perm_ref, groups):
    ntile = val.shape[1] // LANES
    pitch = groups + SUBLANES // 2
    for j in range(groups):
        for c in range(ntile):
            perm_ref[c, pl.ds(j, SUBLANES, stride=pitch), :] = val[
                j * SUBLANES:(j + 1) * SUBLANES, c * LANES:(c + 1) * LANES]
    return jnp.concatenate(
        [jnp.concatenate([perm_ref[c, i * pitch:i * pitch + groups, :] for i in range(SUBLANES)],
                         axis=0) for c in range(ntile)], axis=1)


def _fill_window(grps, ext_ref, prev_ref, halo):
    groups = len(grps)
    first_sublane = lax.broadcasted_iota(jnp.int32, grps[0].shape, 0) == 0
    for m in range(groups - halo, groups):
        rows = slice((m - groups + halo) * SUBLANES, (m - groups + halo + 1) * SUBLANES)
        ext_ref[rows, :] = jnp.where(first_sublane, pltpu.roll(prev_ref[rows, :], 1, 0),
                                     pltpu.roll(grps[m], 1, 0))
        prev_ref[rows, :] = grps[m]
    for j in range(groups):
        ext_ref[(halo + j) * SUBLANES:(halo + j + 1) * SUBLANES, :] = grps[j]


def _mix_in_kernel(x_ref, g_ref, win_ref, dw_ref, dwb_ref, lng_ref, lnb_ref, pw_ref, pwb_ref,
                   qn_ref, wuq_ref, wuqs_ref, kvn_ref, wukv_ref, ct_ref, st_ref,
                   poolw_ref, pools_ref, gm_ref, gd_ref,
                   ya_ref, qm_ref, km_ref, vm_ref, qd_ref, kd_ref, vd_ref, yd_ref, stats_ref,
                   ext_a, prev_a, zbuf, ext_d, prev_d, dbuf, perm_a, perm_ya, perm_d, perm_yd, *, tm):
    s = pl.program_id(1)
    groups = tm // SUBLANES

    def max_sq_norm(v, group_ref):
        vf = v.astype(F32)
        return jnp.max(_dot((vf * vf).astype(BF16), group_ref[...]), axis=0, keepdims=True)

    @pl.when(s == 0)
    def _():
        prev_a[...] = jnp.zeros(prev_a.shape, F32)
        prev_d[...] = jnp.zeros(prev_d.shape, F32)

    x = x_ref[0]
    h = x * lax.rsqrt(jnp.mean(x * x, axis=-1, keepdims=True) + 1e-6) * g_ref[...]
    hb = h.astype(BF16)

    pa = _dot(hb, win_ref[:, _A0:_A1])
    pb = _dot(hb, win_ref[:, _B0:_B1])
    pc = _dot(hb, win_ref[:, _C0:_C1])
    ud = _dot(hb, win_ref[:, _D0:_D1])

    glu = pa[:, 0:256] * _sigmoid(pa[:, 256:512])
    _fill_window(_to_group_order(glu, perm_a, groups), ext_a, prev_a, CONV_HALO)
    for c in range(tm // ROW_CHUNK):
        r0 = c * ROW_CHUNK
        acc = jnp.broadcast_to(dwb_ref[...], (ROW_CHUNK, 256))
        for k in range(CONV_KERNEL):
            off = r0 + (CONV_HALO - (CONV_KERNEL - 1) + k) * SUBLANES
            acc = acc + dw_ref[k:k + 1, :] * ext_a[off:off + ROW_CHUNK, :]
        mu = jnp.mean(acc, axis=-1, keepdims=True)
        cen = acc - mu
        var = jnp.mean(cen * cen, axis=-1, keepdims=True)
        yn = cen * lax.rsqrt(var + 1e-5) * lng_ref[...] + lnb_ref[...]
        zbuf[r0:r0 + ROW_CHUNK, :] = (yn * _sigmoid(yn)).astype(BF16)
    ya = _to_row_order(_dot(zbuf[...], pw_ref[...]), perm_ya, groups)
    ya_ref[0] = (ya + pwb_ref[...]).astype(BF16)

    ct = ct_ref[...]
    st = st_ref[...]
    cq = pb[:, 0:256]
    cqn = cq * lax.rsqrt(jnp.sum(cq * cq, axis=-1, keepdims=True) * (1.0 / MLA_Q_RANK) + 1e-6)
    cqb = (cqn * qn_ref[...]).astype(BF16)
    q = _dot(cqb, wuq_ref[...])
    qs = _dot(cqb, wuqs_ref[...])
    ct4 = jnp.concatenate([ct] * MLA_HEADS, axis=1)
    st4 = jnp.concatenate([st] * MLA_HEADS, axis=1)
    q_scale = LOG2_E * (MLA_NOPE + MLA_ROPE) ** -0.5
    qm = ((q * ct4 + qs * st4) * q_scale).astype(BF16)
    qm_ref[0] = qm
    ckv = pb[:, 256:384]
    ckvn = ckv * lax.rsqrt(jnp.mean(ckv * ckv, axis=-1, keepdims=True) + 1e-6)
    kv = _dot((ckvn * kvn_ref[...]).astype(BF16), wukv_ref[...])
    kr = pb[:, 384:512] * ct + pb[:, 512:640] * st
    km = (kv[:, 0:512] + jnp.concatenate([kr] * MLA_HEADS, axis=1)).astype(BF16)
    km_ref[0] = km
    vm_ref[0] = kv[:, 512:768].astype(BF16)

    qd = (pc[:, 0:256] * (LOG2_E * DIFF_HALF ** -0.5)).astype(BF16)
    kd = pc[:, 256:512].astype(BF16)
    qd_ref[0] = qd
    kd_ref[0] = kd
    stats_ref[0, 0] = jnp.concatenate(
        [max_sq_norm(qm, gm_ref), max_sq_norm(km, gm_ref),
         max_sq_norm(qd, gd_ref), max_sq_norm(kd, gd_ref),
         jnp.zeros((SUBLANES - 4, LANES), F32)], axis=0)
    vd_ref[0] = pc[:, 512:768].astype(BF16)

    _fill_window(_to_group_order(ud, perm_d, groups), ext_d, prev_d, POOL_HALO)
    lane = lax.broadcasted_iota(jnp.int32, (ROW_CHUNK, 256), 1)
    grp = lane // POOL_CH
    win_len = jnp.where(grp == 0, POOL_WINDOWS[0],
                        jnp.where(grp == 1, POOL_WINDOWS[1],
                                  jnp.where(grp == 2, POOL_WINDOWS[2], POOL_WINDOWS[3])))
    for c in range(tm // ROW_CHUNK):
        r0 = c * ROW_CHUNK
        base = r0 + POOL_HALO * SUBLANES
        u = ext_d[base:base + ROW_CHUNK, :]
        run = u
        sums = {}
        for j in range(1, POOL_WINDOWS[-1]):
            run = run + ext_d[base - j * SUBLANES:base - j * SUBLANES + ROW_CHUNK, :]
            if j + 1 in POOL_WINDOWS:
                sums[j + 1] = run
        win = jnp.where(grp == 0, sums[2],
                        jnp.where(grp == 1, sums[4], jnp.where(grp == 2, sums[8], sums[16])))
        r = r0 + lax.broadcasted_iota(jnp.int32, (ROW_CHUNK, 256), 0)
        t_glob = s * tm + (r % SUBLANES) * groups + r // SUBLANES
        cnt = jnp.minimum(t_glob + 1, win_len).astype(F32)
        dbuf[r0:r0 + ROW_CHUNK, :] = (win / cnt - u).astype(BF16)
    yd = _to_row_order(_dot(dbuf[...], poolw_ref[...]), perm_yd, groups)
    yd_ref[0] = (yd * pools_ref[...]).astype(BF16)


def _mix_in(layer, x, per_layer, ctab, stab, group_mla, group_diff):
    b, s, d = x.shape
    tm = min(MIX_TILE, s)
    tok = lambda w: pl.BlockSpec((1, tm, w), lambda bi, si: (bi, si, 0))
    head, tail = per_layer[:13], per_layer[13:]
    in_specs = ([tok(d)] + [_layer_spec(a, layer) for a in head]
                + [pl.BlockSpec((tm, LANES), lambda bi, si: (si, 0))] * 2
                + [_layer_spec(a, layer) for a in tail]
                + [_const_spec(group_mla.shape), _const_spec(group_diff.shape)])
    widths = (256, 512, 512, 256, 256, 256, 256, 256)
    stats_spec = pl.BlockSpec((1, 1, SUBLANES, LANES), lambda bi, si: (bi, si, 0, 0))
    return pl.pallas_call(
        functools.partial(_mix_in_kernel, tm=tm),
        grid=(b, s // tm),
        in_specs=in_specs,
        out_specs=[tok(w) for w in widths] + [stats_spec],
        out_shape=([jax.ShapeDtypeStruct((b, s, w), BF16) for w in widths]
                   + [jax.ShapeDtypeStruct((b, s // tm, SUBLANES, LANES), F32)]),
        scratch_shapes=[pltpu.VMEM((CONV_HALO * SUBLANES + tm, 256), F32),
                        pltpu.VMEM((CONV_HALO * SUBLANES, 256), F32),
                        pltpu.VMEM((tm, 256), BF16),
                        pltpu.VMEM((POOL_HALO * SUBLANES + tm, 256), F32),
                        pltpu.VMEM((POOL_HALO * SUBLANES, 256), F32),
                        pltpu.VMEM((tm, 256), BF16)]
                       + [pltpu.VMEM((2, tm + SUBLANES * SUBLANES, LANES), F32)] * 4,
        compiler_params=pltpu.CompilerParams(
            dimension_semantics=("arbitrary", "arbitrary"),
            vmem_limit_bytes=VMEM_LIMIT_BYTES),
        name="mix_in",
    )(x, *head, ctab, stab, *tail, group_mla, group_diff)


def _head_lanes(vals, rows):
    lane = lax.broadcasted_iota(jnp.int32, (rows, LANES), 1)
    return jnp.concatenate([jnp.where(lane < 64, vals[0], vals[1]),
                            jnp.where(lane < 64, vals[2], vals[3])], axis=1)


def _row_total(l_ref, idx):
    return jnp.sum(l_ref[idx], axis=-1, keepdims=True)


def _stack_masked_v(v_blk):
    n = v_blk.shape[0]
    lane = lax.broadcasted_iota(jnp.int32, (n, 256), 1)
    parts = []
    for h in range(4):
        keep = (lane >= 64 * h) & (lane < 64 * (h + 1))
        parts.append(jnp.where(keep, v_blk, jnp.zeros_like(v_blk)))
    return jnp.concatenate(parts, axis=0)


def _softmax_step(sc, m_ref, l_ref, idx, online, rows=slice(None)):
    if online:
        m_old = m_ref[idx, rows]
        m_new = jnp.maximum(m_old, jnp.max(sc, axis=-1, keepdims=True))
        alpha = jnp.exp2(m_old - m_new)
        m_ref[idx, rows] = m_new
    else:
        m_new = m_ref[idx, rows]
        alpha = None
    reps = sc.shape[1] // LANES
    p = jnp.exp2(sc - jnp.concatenate([m_new] * reps, axis=1))
    psum = p[:, 0:LANES]
    for r in range(1, reps):
        psum = psum + p[:, r * LANES:(r + 1) * LANES]
    l_ref[idx, rows] = (alpha * l_ref[idx, rows] if online else l_ref[idx, rows]) + psum
    return p, alpha


def _logit_bounds(q, group_ref, kmax_sq):
    qf = q.astype(F32)
    qn_sq = _dot((qf * qf).astype(BF16), group_ref[...])
    return NORM_MARGIN * jnp.sqrt(qn_sq * kmax_sq)


def _mla_attn_kernel(safe_ref, q_ref, k_ref, v_ref, stats_ref, group_ref, o_ref,
                     m_ref, l_ref, acc_ref, *, t, far):
    qi = pl.program_id(1)
    l_ref[...] = jnp.zeros(l_ref.shape, F32)
    acc_ref[...] = jnp.zeros(acc_ref.shape, F32)

    def run(online):
        if online:
            m_ref[...] = jnp.full(m_ref.shape, NEG_INF, F32)
        else:
            bounds = _logit_bounds(q_ref[0], group_ref, stats_ref[0, 1:2, :])
            for h in range(MLA_HEADS):
                m_ref[h] = jnp.broadcast_to(bounds[:, h:h + 1], (t, LANES))

        def step(k0, n, r0=0, diagonal=False):
            nr = t - r0
            rows = slice(r0, t)
            vst = _stack_masked_v(v_ref[0, pl.ds(k0, n), :])
            probs, alphas = [], []
            for h in range(MLA_HEADS):
                cols = slice(LANES * h, LANES * (h + 1))
                sc = _dot_nt(q_ref[0, rows, cols], k_ref[0, pl.ds(k0, n), cols])
                if diagonal:
                    row = lax.broadcasted_iota(jnp.int32, (nr, n), 0)
                    col = lax.broadcasted_iota(jnp.int32, (nr, n), 1)
                    sc = jnp.where(row >= col, sc, NEG_INF)
                p, alpha = _softmax_step(sc, m_ref, l_ref, h, online, rows)
                probs.append(p.astype(BF16))
                alphas.append(alpha)
            pv = _dot(jnp.concatenate(probs, axis=1), vst)
            old_acc = acc_ref[rows, :] * _head_lanes(alphas, nr) if online else acc_ref[rows, :]
            acc_ref[rows, :] = old_acc + pv

        def far_body(j, carry):
            step(pl.multiple_of(j * far, far), far)
            return carry

        def near_body(j, carry):
            step(pl.multiple_of(j * t, t), t)
            return carry

        per_far = far // t
        n_far = qi // per_far
        lax.fori_loop(0, n_far, far_body, 0)
        lax.fori_loop(n_far * per_far, qi, near_body, 0)
        strip = t // MLA_DIAG_SPLIT
        for j in range(MLA_DIAG_SPLIT):
            step(pl.multiple_of(qi * t + j * strip, strip), strip, j * strip, True)

    safe = safe_ref[0] != 0
    pl.when(safe)(functools.partial(run, False))
    pl.when(safe_ref[0] == 0)(functools.partial(run, True))
    inv_l = _head_lanes([1.0 / _row_total(l_ref, h) for h in range(MLA_HEADS)], t)
    o_ref[0] = (acc_ref[...] * inv_l).astype(BF16)


def _mla_attn(safe, q, k, v, stats, group):
    b, s, _ = q.shape
    t = min(MLA_TILE, s)
    far = min(MLA_FAR_KEYS, s)
    return pl.pallas_call(
        functools.partial(_mla_attn_kernel, t=t, far=far),
        grid=(b, s // t),
        in_specs=[pl.BlockSpec(memory_space=pltpu.SMEM),
                  pl.BlockSpec((1, t, 512), lambda bi, qi: (bi, qi, 0)),
                  pl.BlockSpec((1, s, 512), lambda bi, qi: (bi, 0, 0)),
                  pl.BlockSpec((1, s, 256), lambda bi, qi: (bi, 0, 0)),
                  pl.BlockSpec((1, SUBLANES, LANES), lambda bi, qi: (bi, 0, 0)),
                  _const_spec(group.shape)],
        out_specs=pl.BlockSpec((1, t, 256), lambda bi, qi: (bi, qi, 0)),
        out_shape=jax.ShapeDtypeStruct((b, s, 256), BF16),
        scratch_shapes=[pltpu.VMEM((MLA_HEADS, t, LANES), F32),
                        pltpu.VMEM((MLA_HEADS, t, LANES), F32),
                        pltpu.VMEM((t, 256), F32)],
        compiler_params=pltpu.CompilerParams(
            dimension_semantics=("arbitrary", "arbitrary"),
            vmem_limit_bytes=VMEM_LIMIT_BYTES),
        name="mla_attn",
    )(safe, q, k, v, stats, group)


def _bias_tiles_kernel(rb_ref, out_ref, *, t, tk):
    row = lax.broadcasted_iota(jnp.int32, (t, tk), 0)
    col = lax.broadcasted_iota(jnp.int32, (t, tk), 1)
    max_exact = REL_BUCKETS // 2
    for j in range(t // tk + 1):
        rel = (1 - j) * tk + row - col
        n = jnp.maximum(rel, 0)
        nf = jnp.maximum(n, 1).astype(F32)
        large = max_exact + (jnp.log(nf / max_exact) / math.log(REL_MAX_DIST / max_exact)
                             * (REL_BUCKETS - max_exact)).astype(jnp.int32)
        large = jnp.minimum(large, REL_BUCKETS - 1)
        bucket = jnp.where(n < max_exact, n, large)
        for h in range(DIFF_HEADS):
            far = rb_ref[REL_BUCKETS - 1, h]
            val = jnp.zeros((t, tk), F32)
            for bkt in range(REL_BUCKETS - 1):
                val = jnp.where(bucket == bkt, (rb_ref[bkt, h] - far) * LOG2_E, val)
            out_ref[h, j] = jnp.where(rel >= 0, val, NEG_INF)


def _bias_tiles(rel_bias, t, tk):
    return pl.pallas_call(
        functools.partial(_bias_tiles_kernel, t=t, tk=tk),
        in_specs=[pl.BlockSpec(memory_space=pltpu.SMEM)],
        out_specs=pl.BlockSpec(memory_space=pltpu.VMEM),
        out_shape=jax.ShapeDtypeStruct((DIFF_HEADS, t // tk + 1, t, tk), F32),
        name="bias_tiles",
    )(rel_bias)


def _diff_attn_kernel(safe_ref, q_ref, k_ref, v_ref, stats_ref, group_ref, bias_hi_ref, bias_ref,
                      lq1_ref, lk1_ref, lq2_ref, lk2_ref, subln_ref, o_ref,
                      qst_ref, m_ref, l_ref, acc_ref, s0_ref, s1_ref, *, t, tk, lambda_init):
    qi = pl.program_id(1)
    nmaps = 2 * DIFF_HEADS
    per_tile = t // tk
    l_ref[...] = jnp.zeros(l_ref.shape, F32)
    acc_ref[...] = jnp.zeros(acc_ref.shape, F32)
    q = q_ref[0]
    lane = lax.broadcasted_iota(jnp.int32, (t, 256), 1)
    for idx in range(nmaps):
        keep = (lane >= DIFF_HALF * idx) & (lane < DIFF_HALF * (idx + 1))
        qst_ref[idx * t:(idx + 1) * t, :] = jnp.where(keep, q, jnp.zeros_like(q))

    def scores(kb, r0=0):
        lhs = qst_ref[...] if r0 == 0 else jnp.concatenate(
            [qst_ref[idx * t + r0:(idx + 1) * t, :] for idx in range(nmaps)], axis=0)
        return _dot_nt(lhs, k_ref[0, pl.ds(pl.multiple_of(kb * tk, tk), tk), :])

    def consume(sc_of, kb, bias_tile, online, r0=0):
        nr = t - r0
        rows = slice(r0, t)
        vst = _stack_masked_v(v_ref[0, pl.ds(pl.multiple_of(kb * tk, tk), tk), :])
        probs = [[], []]
        alphas = [[], []]
        for h in range(DIFF_HEADS):
            for which in range(2):
                idx = 2 * h + which
                sc = sc_of(idx)
                if bias_tile is not None:
                    sc = sc + bias_ref[h, bias_tile, rows, :]
                p, alpha = _softmax_step(sc, m_ref, l_ref, idx, online, rows)
                probs[which].append(p.astype(BF16))
                alphas[which].append(alpha)
        p_all = jnp.concatenate([jnp.concatenate(probs[0], axis=1),
                                 jnp.concatenate(probs[1], axis=1)], axis=0)
        pv = _dot(p_all, vst)
        for which in range(2):
            arows = slice(which * t + r0, (which + 1) * t)
            old = acc_ref[arows, :] * _head_lanes(alphas[which], nr) if online else acc_ref[arows, :]
            acc_ref[arows, :] = old + pv[which * nr:(which + 1) * nr, :]

    def from_ref(ref):
        return lambda idx: ref[idx * t:(idx + 1) * t, :]

    def from_val(sc_all):
        nr = sc_all.shape[0] // nmaps
        return lambda idx: sc_all[idx * nr:(idx + 1) * nr, :]

    hidden = lambda j: j * tk

    first_near = per_tile * qi - 1
    n_plain = jnp.maximum(first_near, 0)

    def run_online():
        m_ref[...] = jnp.full(m_ref.shape, NEG_INF, F32)

        def step(kb, bias_tile, r0=0):
            consume(from_val(scores(kb, r0)), kb, bias_tile, True, r0)

        def body(kb, carry):
            step(kb, None)
            return carry

        lax.fori_loop(0, n_plain, body, 0)
        pl.when(qi > 0)(lambda: step(first_near, 0))
        for j in range(per_tile):
            step(first_near + 1 + j, 1 + j, hidden(j))

    def run_fixed():
        bounds = _logit_bounds(q, group_ref, stats_ref[0, 3:4, :]) + bias_hi_ref[...]
        for idx in range(nmaps):
            m_ref[idx] = jnp.broadcast_to(bounds[:, idx:idx + 1], (t, LANES))

        def pair_body(i, carry):
            s1_ref[...] = scores(2 * i + 1)
            consume(from_ref(s0_ref), 2 * i, None, False)
            s0_ref[...] = scores(2 * i + 2)
            consume(from_ref(s1_ref), 2 * i + 1, None, False)
            return carry

        s0_ref[...] = scores(0)
        n_pairs = n_plain // 2
        lax.fori_loop(0, n_pairs, pair_body, 0)

        @pl.when(qi == 0)
        def _():
            sc1 = scores(1, hidden(1))
            consume(from_ref(s0_ref), 0, 1, False)
            consume(from_val(sc1), 1, 2, False, hidden(1))

        @pl.when(qi > 0)
        def _():
            b0 = 2 * n_pairs
            s1_ref[...] = scores(b0 + 1)
            consume(from_ref(s0_ref), b0, None, False)
            sc2 = scores(b0 + 2)
            consume(from_ref(s1_ref), b0 + 1, 0, False)
            sc3 = scores(b0 + 3, hidden(1))
            consume(from_val(sc2), b0 + 2, 1, False)
            consume(from_val(sc3), b0 + 3, 2, False, hidden(1))

    pl.when(safe_ref[0] != 0)(run_fixed)
    pl.when(safe_ref[0] == 0)(run_online)

    lam = (jnp.exp(jnp.sum(lq1_ref[...] * lk1_ref[...], axis=-1, keepdims=True))
           - jnp.exp(jnp.sum(lq2_ref[...] * lk2_ref[...], axis=-1, keepdims=True))
           + lambda_init)
    o1 = acc_ref[0:t, :] * _head_lanes([1.0 / _row_total(l_ref, 2 * h) for h in range(DIFF_HEADS)], t)
    o2 = acc_ref[t:2 * t, :] * _head_lanes(
        [1.0 / _row_total(l_ref, 2 * h + 1) for h in range(DIFF_HEADS)], t)
    o = o1 - lam * o2
    osq = o * o
    ms = []
    for h in range(DIFF_HEADS):
        keep = (lane >= DIFF_V * h) & (lane < DIFF_V * (h + 1))
        ms.append(jnp.sum(jnp.where(keep, osq, 0.0), axis=-1, keepdims=True) * (1.0 / DIFF_V))
    on = o * lax.rsqrt(_head_lanes(ms, t) + 1e-5) * subln_ref[...]
    o_ref[0] = (on * (1.0 - lambda_init)).astype(BF16)


def _diff_attn(layer, safe, q, k, v, stats, group, bias_hi, bias_tiles, lq1, lk1, lq2, lk2, subln,
               lambda_init):
    b, s, _ = q.shape
    t = min(DIFF_TILE, s)
    tk = t // 2
    return pl.pallas_call(
        functools.partial(_diff_attn_kernel, t=t, tk=tk, lambda_init=lambda_init),
        grid=(b, s // t),
        in_specs=[pl.BlockSpec(memory_space=pltpu.SMEM),
                  pl.BlockSpec((1, t, 256), lambda bi, qi: (bi, qi, 0)),
                  pl.BlockSpec((1, s, 256), lambda bi, qi: (bi, 0, 0)),
                  pl.BlockSpec((1, s, 256), lambda bi, qi: (bi, 0, 0)),
                  pl.BlockSpec((1, SUBLANES, LANES), lambda bi, qi: (bi, 0, 0)),
                  _const_spec(group.shape), _const_spec(bias_hi.shape),
                  _const_spec(bias_tiles.shape),
                  _layer_spec(lq1, layer), _layer_spec(lk1, layer),
                  _layer_spec(lq2, layer), _layer_spec(lk2, layer),
                  _layer_spec(subln, layer)],
        out_specs=pl.BlockSpec((1, t, 256), lambda bi, qi: (bi, qi, 0)),
        out_shape=jax.ShapeDtypeStruct((b, s, 256), BF16),
        scratch_shapes=[pltpu.VMEM((8 * t, 256), BF16),
                        pltpu.VMEM((2 * DIFF_HEADS, t, LANES), F32),
                        pltpu.VMEM((2 * DIFF_HEADS, t, LANES), F32),
                        pltpu.VMEM((2 * t, 256), F32),
                        pltpu.VMEM((8 * t, tk), F32),
                        pltpu.VMEM((8 * t, tk), F32)],
        compiler_params=pltpu.CompilerParams(
            dimension_semantics=("arbitrary", "arbitrary"),
            vmem_limit_bytes=VMEM_LIMIT_BYTES),
        name="diff_attn",
    )(safe, q, k, v, stats, group, bias_hi, bias_tiles, lq1, lk1, lq2, lk2, subln)


def _mem_kv_kernel(mem_ref, g_ref, wk_ref, wv_ref, k_ref, v_ref):
    m = mem_ref[0]
    mn = m * lax.rsqrt(jnp.mean(m * m, axis=-1, keepdims=True) + 1e-6) * g_ref[0]
    mb = mn.astype(BF16)
    k_ref[0, 0] = _dot(mb, wk_ref[0]).astype(BF16)
    v_ref[0, 0] = _dot(mb, wv_ref[0]).astype(BF16)


def _mem_kv(mem, g, wk, wv):
    b, m, d = mem.shape
    nl = wk.shape[0]
    out = jax.ShapeDtypeStruct((nl, b, m, d), BF16)
    return pl.pallas_call(
        _mem_kv_kernel,
        grid=(nl, b),
        in_specs=[pl.BlockSpec((1, m, d), lambda li, bi: (bi, 0, 0)),
                  pl.BlockSpec((1, 1, d), lambda li, bi: (li, 0, 0)),
                  pl.BlockSpec((1, d, d), lambda li, bi: (li, 0, 0)),
                  pl.BlockSpec((1, d, d), lambda li, bi: (li, 0, 0))],
        out_specs=[pl.BlockSpec((1, 1, m, d), lambda li, bi: (li, bi, 0, 0))] * 2,
        out_shape=[out, out],
        compiler_params=pltpu.CompilerParams(
            dimension_semantics=("arbitrary", "arbitrary"),
            vmem_limit_bytes=VMEM_LIMIT_BYTES),
        name="mem_kv",
    )(mem, g, wk, wv)


def _xa_compute(x_ref, ya_ref, yb_ref, yc_ref, yd_ref, wout_ref, g_ref, wq_ref, km_ref, vm_ref,
                wo_ref, obuf):
    x1 = x_ref[0]
    for i, y_ref in enumerate((ya_ref, yb_ref, yc_ref, yd_ref)):
        x1 = x1 + _dot(y_ref[0], wout_ref[256 * i:256 * (i + 1), :])
    hx = x1 * lax.rsqrt(jnp.mean(x1 * x1, axis=-1, keepdims=True) + 1e-6) * g_ref[...]
    hd = wq_ref.shape[1] // XA_HEADS
    q = (_dot(hx.astype(BF16), wq_ref[...]) * (hd ** -0.5)).astype(BF16)
    for h in range(XA_HEADS):
        cols = slice(hd * h, hd * (h + 1))
        sc = _dot_nt(q[:, cols], km_ref[0, :, cols])
        p = jnp.exp(sc - jnp.max(sc, axis=-1, keepdims=True))
        oh = _dot(p.astype(BF16), vm_ref[0, :, cols]) / jnp.sum(p, axis=-1, keepdims=True)
        obuf[:, cols] = oh.astype(BF16)
    return x1 + _dot(obuf[...], wo_ref[...])


def _ffn_compute(x, g_ref, wup_ref, dw_ref, dwb_ref, wdown_ref, gfin_ref,
                 carry, acc_ref, perm_in, perm_out, *, tm, dff, final_norm):
    s = pl.program_id(1)
    groups = tm // SUBLANES

    @pl.when(s == 0)
    def _():
        carry[...] = jnp.zeros(carry.shape, F32)

    h = x * lax.rsqrt(jnp.mean(x * x, axis=-1, keepdims=True) + 1e-6) * g_ref[...]
    hb = jnp.concatenate(_to_group_order(h, perm_in, groups), axis=0).astype(BF16)
    bounds = list(range(0, dff, FFN_CHUNK)) + [dff]
    nchunk = len(bounds) - 1

    def conv(u, slot, col0):
        width = u.shape[1]
        first_sublane = lax.broadcasted_iota(jnp.int32, (SUBLANES, width), 0) == 0
        cols = slice(col0, col0 + width)
        wcols = slice(slot * dff + col0, slot * dff + col0 + width)
        tail = u[tm - 2 * SUBLANES:tm, :]
        prev = carry[slot, :, cols]
        carry[slot, :, cols] = tail

        def wrap(k):
            lo, hi = k * SUBLANES, (k + 1) * SUBLANES
            return jnp.where(first_sublane, pltpu.roll(prev[lo:hi, :], 1, 0),
                             pltpu.roll(tail[lo:hi, :], 1, 0))

        back1 = wrap(1)
        u1 = jnp.concatenate([back1, u[0:tm - SUBLANES, :]], axis=0)
        u2 = jnp.concatenate([wrap(0), back1, u[0:tm - 2 * SUBLANES, :]], axis=0)
        w = dw_ref[:, wcols]
        return dwb_ref[:, wcols] + w[2:3, :] * u + w[1:2, :] * u1 + w[0:1, :] * u2

    def up(j):
        c0, c1 = bounds[j], bounds[j + 1]
        return (_dot(hb, wup_ref[:, c0:c1]), _dot(hb, wup_ref[:, dff + c0:dff + c1]))

    ua, ug = up(0)
    for j in range(nchunk):
        col0 = bounds[j]
        nxt = up(j + 1) if j + 1 < nchunk else None
        a = conv(ua, 0, col0)
        gt = conv(ug, 1, col0)
        act = (gt * _sigmoid(gt) * a).astype(BF16)
        part = _dot(act, wdown_ref[bounds[j]:bounds[j + 1], :])
        if j == 0:
            acc_ref[...] = part
        else:
            acc_ref[...] += part
        if nxt is not None:
            ua, ug = nxt
    y = x + _to_row_order(acc_ref[...], perm_out, groups)
    if final_norm:
        y = y * lax.rsqrt(jnp.mean(y * y, axis=-1, keepdims=True) + 1e-6) * gfin_ref[...]
    return y


def _xa_ffn_kernel(x_ref, ya_ref, yb_ref, yc_ref, yd_ref, wout_ref, gxa_ref, wq_ref, km_ref, vm_ref,
                   wo_ref, gffn_ref, wup_ref, dw_ref, dwb_ref, wdown_ref, gfin_ref, o_ref,
                   obuf, carry, acc_ref, perm_in, perm_out, *, tm, dff, final_norm):
    x2 = _xa_compute(x_ref, ya_ref, yb_ref, yc_ref, yd_ref, wout_ref, gxa_ref, wq_ref, km_ref,
                     vm_ref, wo_ref, obuf)
    o_ref[0] = _ffn_compute(x2, gffn_ref, wup_ref, dw_ref, dwb_ref, wdown_ref, gfin_ref,
                            carry, acc_ref, perm_in, perm_out, tm=tm, dff=dff,
                            final_norm=final_norm)


def _xa_ffn(layer, x, ya, yb, yc, yd, wout, gxa, wq, kmem, vmem, wo, gffn, wup, dw, dwb, wdown,
            gfin, final_norm):
    b, s, d = x.shape
    tm = min(TOKEN_TILE, s)
    m = kmem.shape[2]
    dff = wdown.shape[1]
    tok = lambda w: pl.BlockSpec((1, tm, w), lambda bi, si: (bi, si, 0))
    mem_spec = pl.BlockSpec((None, 1, m, d), lambda bi, si: (layer, bi, 0, 0))
    per_layer = lambda a: _layer_spec(a, layer)
    return pl.pallas_call(
        functools.partial(_xa_ffn_kernel, tm=tm, dff=dff, final_norm=final_norm),
        grid=(b, s // tm),
        in_specs=[tok(d), tok(256), tok(256), tok(256), tok(256),
                  per_layer(wout), per_layer(gxa), per_layer(wq), mem_spec, mem_spec, per_layer(wo),
                  per_layer(gffn), per_layer(wup), per_layer(dw), per_layer(dwb), per_layer(wdown),
                  _const_spec(gfin.shape)],
        out_specs=tok(d),
        out_shape=jax.ShapeDtypeStruct((b, s, d), F32),
        scratch_shapes=[pltpu.VMEM((tm, d), BF16),
                        pltpu.VMEM((2, 2 * SUBLANES, dff), F32),
                        pltpu.VMEM((tm, d), F32),
                        pltpu.VMEM((d // LANES, tm + SUBLANES * SUBLANES, LANES), F32),
                        pltpu.VMEM((d // LANES, tm + SUBLANES * SUBLANES, LANES), F32)],
        compiler_params=pltpu.CompilerParams(
            dimension_semantics=("arbitrary", "arbitrary"),
            vmem_limit_bytes=VMEM_LIMIT_BYTES),
        name="xa_ffn",
    )(x, ya, yb, yc, yd, wout, gxa, wq, kmem, vmem, wo, gffn, wup, dw, dwb, wdown, gfin)


def _pack_w_in(w):
    ua, cq, ckv, kr, qc, kc, vc, ud = jnp.split(
        w, [512, 704, 832, 864, 1120, 1376, 1632], axis=-1)
    z = lambda n: jnp.zeros(w.shape[:-1] + (n,), w.dtype)
    half = MLA_ROPE // 2
    kr_sw = jnp.concatenate([kr[..., half:], kr[..., :half]], axis=-1)
    kr_blk = jnp.concatenate([z(MLA_NOPE), kr, z(LANES - MLA_NOPE - MLA_ROPE)], axis=-1)
    kr_sw_blk = jnp.concatenate([z(MLA_NOPE), kr_sw, z(LANES - MLA_NOPE - MLA_ROPE)], axis=-1)
    packed = jnp.concatenate(
        [ua, cq, z(256 - MLA_Q_RANK), ckv, kr_blk, kr_sw_blk, z(LANES), qc, kc, vc, ud], axis=-1)
    return packed


def _pack_w_uq(w):
    nl = w.shape[0]
    w = w.reshape(nl, MLA_Q_RANK, MLA_HEADS, MLA_NOPE + MLA_ROPE)
    nope, rp = w[..., :MLA_NOPE], w[..., MLA_NOPE:]
    half = MLA_ROPE // 2
    rp_sw = jnp.concatenate([rp[..., half:], rp[..., :half]], axis=-1)
    zn = jnp.zeros_like(nope)
    zp = jnp.zeros((nl, MLA_Q_RANK, MLA_HEADS, LANES - MLA_NOPE - MLA_ROPE), w.dtype)

    def fin(a):
        a = a.reshape(nl, MLA_Q_RANK, MLA_HEADS * LANES)
        return jnp.pad(a, ((0, 0), (0, 256 - MLA_Q_RANK), (0, 0))).astype(BF16)

    return (fin(jnp.concatenate([nope, rp, zp], axis=-1)),
            fin(jnp.concatenate([zn, rp_sw, zp], axis=-1)))


def _pack_w_ukv(w):
    nl = w.shape[0]
    w = w.reshape(nl, MLA_KV_RANK, MLA_HEADS, MLA_NOPE + MLA_V)
    kn, v = w[..., :MLA_NOPE], w[..., MLA_NOPE:]
    kn = jnp.concatenate([kn, jnp.zeros_like(kn)], axis=-1).reshape(nl, MLA_KV_RANK, MLA_HEADS * LANES)
    v = v.reshape(nl, MLA_KV_RANK, MLA_HEADS * MLA_V)
    return jnp.concatenate([kn, v], axis=-1).astype(BF16)


def _rope_tables(positions):
    inv_freq = ROPE_BASE ** (-jnp.arange(0, MLA_ROPE, 2, dtype=F32) / MLA_ROPE)
    ang = positions.astype(F32)[:, None] * inv_freq[None, :]
    cos, sin = jnp.cos(ang), jnp.sin(ang)
    s = positions.shape[0]
    pad = jnp.zeros((s, LANES - MLA_NOPE - MLA_ROPE), F32)
    ctab = jnp.concatenate([jnp.ones((s, MLA_NOPE), F32), cos, cos, pad], axis=1)
    stab = jnp.concatenate([jnp.zeros((s, MLA_NOPE), F32), -sin, sin, pad], axis=1)
    return ctab, stab


def _lane_groups(n_lanes, width):
    lane = jnp.arange(n_lanes)[:, None] // width
    return (lane == jnp.arange(LANES)[None, :]).astype(BF16)


def _bias_range(rel_bias):
    shifted = (rel_bias - rel_bias[REL_BUCKETS - 1]) * LOG2_E
    hi = jnp.repeat(jnp.max(shifted, axis=0), 2)
    lo = jnp.repeat(jnp.min(shifted, axis=0), 2)
    pad = (0, LANES - 2 * DIFF_HEADS)
    return jnp.pad(hi, pad).reshape(1, LANES), jnp.pad(hi - lo, pad).reshape(1, LANES)


def _fixed_stabiliser_ok(qmax_sq, kmax_sq, extra_span):
    bound = NORM_MARGIN * jnp.sqrt(qmax_sq * kmax_sq)
    ok = jnp.all(2.0 * bound + extra_span <= SAFE_LOGIT_SPAN)
    return ok.astype(jnp.int32).reshape(1)


def _block_diag(w):
    nl, g, c, _ = w.shape
    out = jnp.zeros((nl, g * c, g * c), w.dtype)
    for i in range(g):
        out = out.at[:, i * c:(i + 1) * c, i * c:(i + 1) * c].set(w[:, i])
    return out


def kernel(x, mem, positions, rel_bias, norm_mix, w_in, w_out, conv_dw, conv_dw_b, conv_ln_g, conv_ln_b, conv_pw, conv_pw_b, mla_q_norm, mla_w_uq, mla_kv_norm, mla_w_ukv, diff_lq1, diff_lk1, diff_lq2, diff_lk2, diff_subln, pool_w, pool_scale, norm_xa, mem_norm, xa_wq, xa_wk, xa_wv, xa_wo, norm_ffn, ffn_up, ffn_dw, ffn_dw_b, ffn_down, norm_final):
    depth = w_in.shape[0]
    s = x.shape[1]
    rows = lambda a: a[:, None, :]
    bf = lambda a: a.astype(BF16)
    ctab, stab = _rope_tables(positions)
    bias_tiles = _bias_tiles(rel_bias, min(DIFF_TILE, s), min(DIFF_TILE, s) // 2)
    kmem, vmem = _mem_kv(mem, rows(mem_norm), bf(xa_wk), bf(xa_wv))
    group_mla = _lane_groups(MLA_HEADS * LANES, LANES)
    group_diff = _lane_groups(2 * DIFF_HEADS * DIFF_HALF, DIFF_HALF)
    bias_hi, bias_span = _bias_range(rel_bias)
    wuq, wuqs = _pack_w_uq(mla_w_uq)
    mix_params = [rows(norm_mix), _pack_w_in(bf(w_in)), conv_dw, rows(conv_dw_b), rows(conv_ln_g),
                  rows(conv_ln_b), bf(conv_pw), rows(conv_pw_b),
                  rows(jnp.pad(mla_q_norm, ((0, 0), (0, 256 - MLA_Q_RANK)))), wuq, wuqs,
                  rows(mla_kv_norm), _pack_w_ukv(mla_w_ukv),
                  bf(_block_diag(pool_w)), rows(pool_scale)]
    lq1, lk1, lq2, lk2 = rows(diff_lq1), rows(diff_lk1), rows(diff_lq2), rows(diff_lk2)
    subln = rows(jnp.tile(diff_subln, (1, DIFF_HEADS)))
    w_out, xa_wq, xa_wo, ffn_up, ffn_down = bf(w_out), bf(xa_wq), bf(xa_wo), bf(ffn_up), bf(ffn_down)
    g_xa, g_ffn, ffn_dw_b = rows(norm_xa), rows(norm_ffn), rows(ffn_dw_b)
    for l in range(depth):
        ya, qm, km, vm, qd, kd, vd, yd, stats = _mix_in(l, x, mix_params, ctab, stab, group_mla,
                                                         group_diff)
        stats = jnp.max(stats, axis=1)
        yb = _mla_attn(_fixed_stabiliser_ok(stats[:, 0], stats[:, 1], 0.0), qm, km, vm, stats,
                       group_mla)
        lambda_init = 0.8 - 0.6 * math.exp(-0.3 * l)
        yc = _diff_attn(l, _fixed_stabiliser_ok(stats[:, 2], stats[:, 3], bias_span), qd, kd, vd,
                        stats, group_diff, bias_hi, bias_tiles, lq1, lk1, lq2, lk2, subln,
                        lambda_init)
        x = _xa_ffn(l, x, ya, yb, yc, yd, w_out, g_xa, xa_wq, kmem, vmem, xa_wo, g_ffn, ffn_up,
                    ffn_dw, ffn_dw_b, ffn_down, norm_final.reshape(1, -1), l == depth - 1)
    return x
```

```python
import functools
import math

import jax
import jax.numpy as jnp
from jax import lax
from jax.experimental import pallas as pl
from jax.experimental.pallas import tpu as pltpu

N_GROUPS = 4
CONV_KERNEL = 31
MLA_HEADS = 4
MLA_NOPE = 64
MLA_ROPE = 32
MLA_V = 64
MLA_Q_RANK = 192
MLA_KV_RANK = 128
DIFF_HEADS = 4
DIFF_HALF = 32
DIFF_V = 64
POOL_WINDOWS = (2, 4, 8, 16)
POOL_CH = 64
REL_BUCKETS = 32
REL_MAX_DIST = 128
XA_HEADS = 4
FFN_CONV = 3
ROPE_BASE = 10000.0
NEG_INF = -1e30
LOG2_E = math.log2(math.e)
NORM_MARGIN = 1.02
SAFE_LOGIT_SPAN = 100.0

LANES = 128
SUBLANES = 8
VMEM_LIMIT_BYTES = 56 * 1024 * 1024

TOKEN_TILE = 512
MIX_TILE = 1024
DIFF_TILE = 512
MLA_TILE = 1024
MLA_FAR_KEYS = 1024
MLA_DIAG_SPLIT = 2
CONV_HALO = 32
POOL_HALO = 16
ROW_CHUNK = 64
FFN_CHUNK = 1024

BF16 = jnp.bfloat16
F32 = jnp.float32


def _dot(a, b):
    return jnp.dot(a, b, preferred_element_type=F32)


def _dot_nt(a, b):
    return lax.dot_general(a, b, (((1,), (1,)), ((), ())), preferred_element_type=F32)


def _sigmoid(x):
    return 1.0 / (1.0 + jnp.exp(-x))


def _const_spec(shape):
    nd = len(shape)
    return pl.BlockSpec(shape, lambda *_: (0,) * nd)


def _layer_spec(stacked, layer):
    tail = stacked.shape[1:]
    return pl.BlockSpec((None,) + tail, lambda *_: (layer,) + (0,) * len(tail))


_A0, _A1 = 0, 512
_B0, _B1 = 512, 1280
_C0, _C1 = 1280, 2048
_D0, _D1 = 2048, 2304
W_IN_COLS = 2304


def _to_group_order(val, perm_ref, groups):
    ntile = val.shape[1] // LANES
    pitch = groups + SUBLANES // 2
    for c in range(ntile):
        for i in range(SUBLANES):
            perm_ref[c, i * pitch:i * pitch + groups, :] = val[i * groups:(i + 1) * groups,
                                                                 c * LANES:(c + 1) * LANES]
    return [jnp.concatenate([perm_ref[c, pl.ds(j, SUBLANES, stride=pitch), :]
                             for c in range(ntile)], axis=1) for j in range(groups)]


def _to_row_order(val, perm_ref, groups):
    ntile = val.shape[1] // LANES
    pitch = groups + SUBLANES // 2
    for j in range(groups):
        for c in range(ntile):
            perm_ref[c, pl.ds(j, SUBLANES, stride=pitch), :] = val[
                j * SUBLANES:(j + 1) * SUBLANES, c * LANES:(c + 1) * LANES]
    return jnp.concatenate(
        [jnp.concatenate([perm_ref[c, i * pitch:i * pitch + groups, :] for i in range(SUBLANES)],
                         axis=0) for c in range(ntile)], axis=1)


def _fill_window(grps, ext_ref, prev_ref, halo):
    groups = len(grps)
    first_sublane = lax.broadcasted_iota(jnp.int32, grps[0].shape, 0) == 0
    for m in range(groups - halo, groups):
        rows = slice((m - groups + halo) * SUBLANES, (m - groups + halo + 1) * SUBLANES)
        ext_ref[rows, :] = jnp.where(first_sublane, pltpu.roll(prev_ref[rows, :], 1, 0),
                                     pltpu.roll(grps[m], 1, 0))
        prev_ref[rows, :] = grps[m]
    for j in range(groups):
        ext_ref[(halo + j) * SUBLANES:(halo + j + 1) * SUBLANES, :] = grps[j]


def _mix_in_kernel(x_ref, g_ref, win_ref, dw_ref, dwb_ref, lng_ref, lnb_ref, pw_ref, pwb_ref,
                   qn_ref, wuq_ref, wuqs_ref, kvn_ref, wukv_ref, ct_ref, st_ref,
                   poolw_ref, pools_ref, gm_ref, gd_ref,
                   ya_ref, qm_ref, km_ref, vm_ref, qd_ref, kd_ref, vd_ref, yd_ref, stats_ref,
                   ext_a, prev_a, zbuf, ext_d, prev_d, dbuf, perm_a, perm_ya, perm_d, perm_yd, *, tm):
    s = pl.program_id(1)
    groups = tm // SUBLANES

    def max_sq_norm(v, group_ref):
        vf = v.astype(F32)
        return jnp.max(_dot((vf * vf).astype(BF16), group_ref[...]), axis=0, keepdims=True)

    @pl.when(s == 0)
    def _():
        prev_a[...] = jnp.zeros(prev_a.shape, F32)
        prev_d[...] = jnp.zeros(prev_d.shape, F32)

    x = x_ref[0]
    h = x * lax.rsqrt(jnp.mean(x * x, axis=-1, keepdims=True) + 1e-6) * g_ref[...]
    hb = h.astype(BF16)

    pa = _dot(hb, win_ref[:, _A0:_A1])
    pb = _dot(hb, win_ref[:, _B0:_B1])
    pc = _dot(hb, win_ref[:, _C0:_C1])
    ud = _dot(hb, win_ref[:, _D0:_D1])

    glu = pa[:, 0:256] * _sigmoid(pa[:, 256:512])
    _fill_window(_to_group_order(glu, perm_a, groups), ext_a, prev_a, CONV_HALO)
    for c in range(tm // ROW_CHUNK):
        r0 = c * ROW_CHUNK
        acc = jnp.broadcast_to(dwb_ref[...], (ROW_CHUNK, 256))
        for k in range(CONV_KERNEL):
            off = r0 + (CONV_HALO - (CONV_KERNEL - 1) + k) * SUBLANES
            acc = acc + dw_ref[k:k + 1, :] * ext_a[off:off + ROW_CHUNK, :]
        mu = jnp.mean(acc, axis=-1, keepdims=True)
        cen = acc - mu
        var = jnp.mean(cen * cen, axis=-1, keepdims=True)
        yn = cen * lax.rsqrt(var + 1e-5) * lng_ref[...] + lnb_ref[...]
        zbuf[r0:r0 + ROW_CHUNK, :] = (yn * _sigmoid(yn)).astype(BF16)
    ya = _to_row_order(_dot(zbuf[...], pw_ref[...]), perm_ya, groups)
    ya_ref[0] = (ya + pwb_ref[...]).astype(BF16)

    ct = ct_ref[...]
    st = st_ref[...]
    cq = pb[:, 0:256]
    cqn = cq * lax.rsqrt(jnp.sum(cq * cq, axis=-1, keepdims=True) * (1.0 / MLA_Q_RANK) + 1e-6)
    cqb = (cqn * qn_ref[...]).astype(BF16)
    q = _dot(cqb, wuq_ref[...])
    qs = _dot(cqb, wuqs_ref[...])
    ct4 = jnp.concatenate([ct] * MLA_HEADS, axis=1)
    st4 = jnp.concatenate([st] * MLA_HEADS, axis=1)
    q_scale = LOG2_E * (MLA_NOPE + MLA_ROPE) ** -0.5
    qm = ((q * ct4 + qs * st4) * q_scale).astype(BF16)
    qm_ref[0] = qm
    ckv = pb[:, 256:384]
    ckvn = ckv * lax.rsqrt(jnp.mean(ckv * ckv, axis=-1, keepdims=True) + 1e-6)
    kv = _dot((ckvn * kvn_ref[...]).astype(BF16), wukv_ref[...])
    kr = pb[:, 384:512] * ct + pb[:, 512:640] * st
    km = (kv[:, 0:512] + jnp.concatenate([kr] * MLA_HEADS, axis=1)).astype(BF16)
    km_ref[0] = km
    vm_ref[0] = kv[:, 512:768].astype(BF16)

    qd = (pc[:, 0:256] * (LOG2_E * DIFF_HALF ** -0.5)).astype(BF16)
    kd = pc[:, 256:512].astype(BF16)
    qd_ref[0] = qd
    kd_ref[0] = kd
    stats_ref[0, 0] = jnp.concatenate(
        [max_sq_norm(qm, gm_ref), max_sq_norm(km, gm_ref),
         max_sq_norm(qd, gd_ref), max_sq_norm(kd, gd_ref),
         jnp.zeros((SUBLANES - 4, LANES), F32)], axis=0)
    vd_ref[0] = pc[:, 512:768].astype(BF16)

    _fill_window(_to_group_order(ud, perm_d, groups), ext_d, prev_d, POOL_HALO)
    lane = lax.broadcasted_iota(jnp.int32, (ROW_CHUNK, 256), 1)
    grp = lane // POOL_CH
    win_len = jnp.where(grp == 0, POOL_WINDOWS[0],
                        jnp.where(grp == 1, POOL_WINDOWS[1],
                                  jnp.where(grp == 2, POOL_WINDOWS[2], POOL_WINDOWS[3])))
    for c in range(tm // ROW_CHUNK):
        r0 = c * ROW_CHUNK
        base = r0 + POOL_HALO * SUBLANES
        u = ext_d[base:base + ROW_CHUNK, :]
        run = u
        sums = {}
        for j in range(1, POOL_WINDOWS[-1]):
            run = run + ext_d[base - j * SUBLANES:base - j * SUBLANES + ROW_CHUNK, :]
            if j + 1 in POOL_WINDOWS:
                sums[j + 1] = run
        win = jnp.where(grp == 0, sums[2],
                        jnp.where(grp == 1, sums[4], jnp.where(grp == 2, sums[8], sums[16])))
        r = r0 + lax.broadcasted_iota(jnp.int32, (ROW_CHUNK, 256), 0)
        t_glob = s * tm + (r % SUBLANES) * groups + r // SUBLANES
        cnt = jnp.minimum(t_glob + 1, win_len).astype(F32)
        dbuf[r0:r0 + ROW_CHUNK, :] = (win / cnt - u).astype(BF16)
    yd = _to_row_order(_dot(dbuf[...], poolw_ref[...]), perm_yd, groups)
    yd_ref[0] = (yd * pools_ref[...]).astype(BF16)


def _mix_in(layer, x, per_layer, ctab, stab, group_mla, group_diff):
    b, s, d = x.shape
    tm = min(MIX_TILE, s)
    tok = lambda w: pl.BlockSpec((1, tm, w), lambda bi, si: (bi, si, 0))
    head, tail = per_layer[:13], per_layer[13:]
    in_specs = ([tok(d)] + [_layer_spec(a, layer) for a in head]
                + [pl.BlockSpec((tm, LANES), lambda bi, si: (si, 0))] * 2
                + [_layer_spec(a, layer) for a in tail]
                + [_const_spec(group_mla.shape), _const_spec(group_diff.shape)])
    widths = (256, 512, 512, 256, 256, 256, 256, 256)
    stats_spec = pl.BlockSpec((1, 1, SUBLANES, LANES), lambda bi, si: (bi, si, 0, 0))
    return pl.pallas_call(
        functools.partial(_mix_in_kernel, tm=tm),
        grid=(b, s // tm),
        in_specs=in_specs,
        out_specs=[tok(w) for w in widths] + [stats_spec],
        out_shape=([jax.ShapeDtypeStruct((b, s, w), BF16) for w in widths]
                   + [jax.ShapeDtypeStruct((b, s // tm, SUBLANES, LANES), F32)]),
        scratch_shapes=[pltpu.VMEM((CONV_HALO * SUBLANES + tm, 256), F32),
                        pltpu.VMEM((CONV_HALO * SUBLANES, 256), F32),
                        pltpu.VMEM((tm, 256), BF16),
                        pltpu.VMEM((POOL_HALO * SUBLANES + tm, 256), F32),
                        pltpu.VMEM((POOL_HALO * SUBLANES, 256), F32),
                        pltpu.VMEM((tm, 256), BF16)]
                       + [pltpu.VMEM((2, tm + SUBLANES * SUBLANES, LANES), F32)] * 4,
        compiler_params=pltpu.CompilerParams(
            dimension_semantics=("arbitrary", "arbitrary"),
            vmem_limit_bytes=VMEM_LIMIT_BYTES),
        name="mix_in",
    )(x, *head, ctab, stab, *tail, group_mla, group_diff)


def _head_lanes(vals, rows):
    lane = lax.broadcasted_iota(jnp.int32, (rows, LANES), 1)
    return jnp.concatenate([jnp.where(lane < 64, vals[0], vals[1]),
                            jnp.where(lane < 64, vals[2], vals[3])], axis=1)


def _row_total(l_ref, idx):
    return jnp.sum(l_ref[idx], axis=-1, keepdims=True)


def _stack_masked_v(v_blk):
    n = v_blk.shape[0]
    lane = lax.broadcasted_iota(jnp.int32, (n, 256), 1)
    parts = []
    for h in range(4):
        keep = (lane >= 64 * h) & (lane < 64 * (h + 1))
        parts.append(jnp.where(keep, v_blk, jnp.zeros_like(v_blk)))
    return jnp.concatenate(parts, axis=0)


def _softmax_step(sc, m_ref, l_ref, idx, online, rows=slice(None)):
    if online:
        m_old = m_ref[idx, rows]
        m_new = jnp.maximum(m_old, jnp.max(sc, axis=-1, keepdims=True))
        alpha = jnp.exp2(m_old - m_new)
        m_ref[idx, rows] = m_new
    else:
        m_new = m_ref[idx, rows]
        alpha = None
    reps = sc.shape[1] // LANES
    p = jnp.exp2(sc - jnp.concatenate([m_new] * reps, axis=1))
    psum = p[:, 0:LANES]
    for r in range(1, reps):
        psum = psum + p[:, r * LANES:(r + 1) * LANES]
    l_ref[idx, rows] = (alpha * l_ref[idx, rows] if online else l_ref[idx, rows]) + psum
    return p, alpha


def _logit_bounds(q, group_ref, kmax_sq):
    qf = q.astype(F32)
    qn_sq = _dot((qf * qf).astype(BF16), group_ref[...])
    return NORM_MARGIN * jnp.sqrt(qn_sq * kmax_sq)


def _mla_attn_kernel(safe_ref, q_ref, k_ref, v_ref, stats_ref, group_ref, o_ref,
                     m_ref, l_ref, acc_ref, *, t, far):
    qi = pl.program_id(1)
    l_ref[...] = jnp.zeros(l_ref.shape, F32)
    acc_ref[...] = jnp.zeros(acc_ref.shape, F32)

    def run(online):
        if online:
            m_ref[...] = jnp.full(m_ref.shape, NEG_INF, F32)
        else:
            bounds = _logit_bounds(q_ref[0], group_ref, stats_ref[0, 1:2, :])
            for h in range(MLA_HEADS):
                m_ref[h] = jnp.broadcast_to(bounds[:, h:h + 1], (t, LANES))

        def step(k0, n, r0=0, diagonal=False):
            nr = t - r0
            rows = slice(r0, t)
            vst = _stack_masked_v(v_ref[0, pl.ds(k0, n), :])
            probs, alphas = [], []
            for h in range(MLA_HEADS):
                cols = slice(LANES * h, LANES * (h + 1))
                sc = _dot_nt(q_ref[0, rows, cols], k_ref[0, pl.ds(k0, n), cols])
                if diagonal:
                    row = lax.broadcasted_iota(jnp.int32, (nr, n), 0)
                    col = lax.broadcasted_iota(jnp.int32, (nr, n), 1)
                    sc = jnp.where(row >= col, sc, NEG_INF)
                p, alpha = _softmax_step(sc, m_ref, l_ref, h, online, rows)
                probs.append(p.astype(BF16))
                alphas.append(alpha)
            pv = _dot(jnp.concatenate(probs, axis=1), vst)
            old_acc = acc_ref[rows, :] * _head_lanes(alphas, nr) if online else acc_ref[rows, :]
            acc_ref[rows, :] = old_acc + pv

        def far_body(j, carry):
            step(pl.multiple_of(j * far, far), far)
            return carry

        def near_body(j, carry):
            step(pl.multiple_of(j * t, t), t)
            return carry

        per_far = far // t
        n_far = qi // per_far
        lax.fori_loop(0, n_far, far_body, 0)
        lax.fori_loop(n_far * per_far, qi, near_body, 0)
        strip = t // MLA_DIAG_SPLIT
        for j in range(MLA_DIAG_SPLIT):
            step(pl.multiple_of(qi * t + j * strip, strip), strip, j * strip, True)

    safe = safe_ref[0] != 0
    pl.when(safe)(functools.partial(run, False))
    pl.when(safe_ref[0] == 0)(functools.partial(run, True))
    inv_l = _head_lanes([1.0 / _row_total(l_ref, h) for h in range(MLA_HEADS)], t)
    o_ref[0] = (acc_ref[...] * inv_l).astype(BF16)


def _mla_attn(safe, q, k, v, stats, group):
    b, s, _ = q.shape
    t = min(MLA_TILE, s)
    far = min(MLA_FAR_KEYS, s)
    return pl.pallas_call(
        functools.partial(_mla_attn_kernel, t=t, far=far),
        grid=(b, s // t),
        in_specs=[pl.BlockSpec(memory_space=pltpu.SMEM),
                  pl.BlockSpec((1, t, 512), lambda bi, qi: (bi, qi, 0)),
                  pl.BlockSpec((1, s, 512), lambda bi, qi: (bi, 0, 0), pipeline_mode=pl.Buffered(1)),
                  pl.BlockSpec((1, s, 256), lambda bi, qi: (bi, 0, 0), pipeline_mode=pl.Buffered(1)),
                  pl.BlockSpec((1, SUBLANES, LANES), lambda bi, qi: (bi, 0, 0)),
                  _const_spec(group.shape)],
        out_specs=pl.BlockSpec((1, t, 256), lambda bi, qi: (bi, qi, 0)),
        out_shape=jax.ShapeDtypeStruct((b, s, 256), BF16),
        scratch_shapes=[pltpu.VMEM((MLA_HEADS, t, LANES), F32),
                        pltpu.VMEM((MLA_HEADS, t, LANES), F32),
                        pltpu.VMEM((t, 256), F32)],
        compiler_params=pltpu.CompilerParams(
            dimension_semantics=("arbitrary", "arbitrary"),
            vmem_limit_bytes=VMEM_LIMIT_BYTES),
        name="mla_attn",
    )(safe, q, k, v, stats, group)


def _bias_tiles_kernel(rb_ref, out_ref, *, t, tk):
    row = lax.broadcasted_iota(jnp.int32, (t, tk), 0)
    col = lax.broadcasted_iota(jnp.int32, (t, tk), 1)
    max_exact = REL_BUCKETS // 2
    for j in range(t // tk + 1):
        rel = (1 - j) * tk + row - col
        n = jnp.maximum(rel, 0)
        nf = jnp.maximum(n, 1).astype(F32)
        large = max_exact + (jnp.log(nf / max_exact) / math.log(REL_MAX_DIST / max_exact)
                             * (REL_BUCKETS - max_exact)).astype(jnp.int32)
        large = jnp.minimum(large, REL_BUCKETS - 1)
        bucket = jnp.where(n < max_exact, n, large)
        for h in range(DIFF_HEADS):
            far = rb_ref[REL_BUCKETS - 1, h]
            val = jnp.zeros((t, tk), F32)
            for bkt in range(REL_BUCKETS - 1):
                val = jnp.where(bucket == bkt, (rb_ref[bkt, h] - far) * LOG2_E, val)
            out_ref[h, j] = jnp.where(rel >= 0, val, NEG_INF)


def _bias_tiles(rel_bias, t, tk):
    return pl.pallas_call(
        functools.partial(_bias_tiles_kernel, t=t, tk=tk),
        in_specs=[pl.BlockSpec(memory_space=pltpu.SMEM)],
        out_specs=pl.BlockSpec(memory_space=pltpu.VMEM),
        out_shape=jax.ShapeDtypeStruct((DIFF_HEADS, t // tk + 1, t, tk), F32),
        name="bias_tiles",
    )(rel_bias)


def _diff_attn_kernel(safe_ref, q_ref, k_ref, v_ref, stats_ref, group_ref, bias_hi_ref, bias_ref,
                      lq1_ref, lk1_ref, lq2_ref, lk2_ref, subln_ref, o_ref,
                      qst_ref, m_ref, l_ref, acc_ref, s0_ref, s1_ref, *, t, tk, lambda_init):
    qi = pl.program_id(1)
    nmaps = 2 * DIFF_HEADS
    per_tile = t // tk
    l_ref[...] = jnp.zeros(l_ref.shape, F32)
    acc_ref[...] = jnp.zeros(acc_ref.shape, F32)
    q = q_ref[0]
    lane = lax.broadcasted_iota(jnp.int32, (t, 256), 1)
    for idx in range(nmaps):
        keep = (lane >= DIFF_HALF * idx) & (lane < DIFF_HALF * (idx + 1))
        qst_ref[idx * t:(idx + 1) * t, :] = jnp.where(keep, q, jnp.zeros_like(q))

    def scores(kb, r0=0):
        lhs = qst_ref[...] if r0 == 0 else jnp.concatenate(
            [qst_ref[idx * t + r0:(idx + 1) * t, :] for idx in range(nmaps)], axis=0)
        return _dot_nt(lhs, k_ref[0, pl.ds(pl.multiple_of(kb * tk, tk), tk), :])

    def consume(sc_of, kb, bias_tile, online, r0=0):
        nr = t - r0
        rows = slice(r0, t)
        vst = _stack_masked_v(v_ref[0, pl.ds(pl.multiple_of(kb * tk, tk), tk), :])
        probs = [[], []]
        alphas = [[], []]
        for h in range(DIFF_HEADS):
            for which in range(2):
                idx = 2 * h + which
                sc = sc_of(idx)
                if bias_tile is not None:
                    sc = sc + bias_ref[h, bias_tile, rows, :]
                p, alpha = _softmax_step(sc, m_ref, l_ref, idx, online, rows)
                probs[which].append(p.astype(BF16))
                alphas[which].append(alpha)
        p_all = jnp.concatenate([jnp.concatenate(probs[0], axis=1),
                                 jnp.concatenate(probs[1], axis=1)], axis=0)
        pv = _dot(p_all, vst)
        for which in range(2):
            arows = slice(which * t + r0, (which + 1) * t)
            old = acc_ref[arows, :] * _head_lanes(alphas[which], nr) if online else acc_ref[arows, :]
            acc_ref[arows, :] = old + pv[which * nr:(which + 1) * nr, :]

    def from_ref(ref):
        return lambda idx: ref[idx * t:(idx + 1) * t, :]

    def from_val(sc_all):
        nr = sc_all.shape[0] // nmaps
        return lambda idx: sc_all[idx * nr:(idx + 1) * nr, :]

    hidden = lambda j: j * tk

    first_near = per_tile * qi - 1
    n_plain = jnp.maximum(first_near, 0)

    def run_online():
        m_ref[...] = jnp.full(m_ref.shape, NEG_INF, F32)

        def step(kb, bias_tile, r0=0):
            consume(from_val(scores(kb, r0)), kb, bias_tile, True, r0)

        def body(kb, carry):
            step(kb, None)
            return carry

        lax.fori_loop(0, n_plain, body, 0)
        pl.when(qi > 0)(lambda: step(first_near, 0))
        for j in range(per_tile):
            step(first_near + 1 + j, 1 + j, hidden(j))

    def run_fixed():
        bounds = _logit_bounds(q, group_ref, stats_ref[0, 3:4, :]) + bias_hi_ref[...]
        for idx in range(nmaps):
            m_ref[idx] = jnp.broadcast_to(bounds[:, idx:idx + 1], (t, LANES))

        def pair_body(i, carry):
            s1_ref[...] = scores(2 * i + 1)
            consume(from_ref(s0_ref), 2 * i, None, False)
            s0_ref[...] = scores(2 * i + 2)
            consume(from_ref(s1_ref), 2 * i + 1, None, False)
            return carry

        s0_ref[...] = scores(0)
        n_pairs = n_plain // 2
        lax.fori_loop(0, n_pairs, pair_body, 0)

        @pl.when(qi == 0)
        def _():
            sc1 = scores(1, hidden(1))
            consume(from_ref(s0_ref), 0, 1, False)
            consume(from_val(sc1), 1, 2, False, hidden(1))

        @pl.when(qi > 0)
        def _():
            b0 = 2 * n_pairs
            s1_ref[...] = scores(b0 + 1)
            consume(from_ref(s0_ref), b0, None, False)
            sc2 = scores(b0 + 2)
            consume(from_ref(s1_ref), b0 + 1, 0, False)
            sc3 = scores(b0 + 3, hidden(1))
            consume(from_val(sc2), b0 + 2, 1, False)
            consume(from_val(sc3), b0 + 3, 2, False, hidden(1))

    pl.when(safe_ref[0] != 0)(run_fixed)
    pl.when(safe_ref[0] == 0)(run_online)

    lam = (jnp.exp(jnp.sum(lq1_ref[...] * lk1_ref[...], axis=-1, keepdims=True))
           - jnp.exp(jnp.sum(lq2_ref[...] * lk2_ref[...], axis=-1, keepdims=True))
           + lambda_init)
    o1 = acc_ref[0:t, :] * _head_lanes([1.0 / _row_total(l_ref, 2 * h) for h in range(DIFF_HEADS)], t)
    o2 = acc_ref[t:2 * t, :] * _head_lanes(
        [1.0 / _row_total(l_ref, 2 * h + 1) for h in range(DIFF_HEADS)], t)
    o = o1 - lam * o2
    osq = o * o
    ms = []
    for h in range(DIFF_HEADS):
        keep = (lane >= DIFF_V * h) & (lane < DIFF_V * (h + 1))
        ms.append(jnp.sum(jnp.where(keep, osq, 0.0), axis=-1, keepdims=True) * (1.0 / DIFF_V))
    on = o * lax.rsqrt(_head_lanes(ms, t) + 1e-5) * subln_ref[...]
    o_ref[0] = (on * (1.0 - lambda_init)).astype(BF16)


def _diff_attn(layer, safe, q, k, v, stats, group, bias_hi, bias_tiles, lq1, lk1, lq2, lk2, subln,
               lambda_init):
    b, s, _ = q.shape
    t = min(DIFF_TILE, s)
    tk = t // 2
    return pl.pallas_call(
        functools.partial(_diff_attn_kernel, t=t, tk=tk, lambda_init=lambda_init),
        grid=(b, s // t),
        in_specs=[pl.BlockSpec(memory_space=pltpu.SMEM),
                  pl.BlockSpec((1, t, 256), lambda bi, qi: (bi, qi, 0)),
                  pl.BlockSpec((1, s, 256), lambda bi, qi: (bi, 0, 0), pipeline_mode=pl.Buffered(1)),
                  pl.BlockSpec((1, s, 256), lambda bi, qi: (bi, 0, 0), pipeline_mode=pl.Buffered(1)),
                  pl.BlockSpec((1, SUBLANES, LANES), lambda bi, qi: (bi, 0, 0)),
                  _const_spec(group.shape), _const_spec(bias_hi.shape),
                  _const_spec(bias_tiles.shape),
                  _layer_spec(lq1, layer), _layer_spec(lk1, layer),
                  _layer_spec(lq2, layer), _layer_spec(lk2, layer),
                  _layer_spec(subln, layer)],
        out_specs=pl.BlockSpec((1, t, 256), lambda bi, qi: (bi, qi, 0)),
        out_shape=jax.ShapeDtypeStruct((b, s, 256), BF16),
        scratch_shapes=[pltpu.VMEM((8 * t, 256), BF16),
                        pltpu.VMEM((2 * DIFF_HEADS, t, LANES), F32),
                        pltpu.VMEM((2 * DIFF_HEADS, t, LANES), F32),
                        pltpu.VMEM((2 * t, 256), F32),
                        pltpu.VMEM((8 * t, tk), F32),
                        pltpu.VMEM((8 * t, tk), F32)],
        compiler_params=pltpu.CompilerParams(
            dimension_semantics=("arbitrary", "arbitrary"),
            vmem_limit_bytes=VMEM_LIMIT_BYTES),
        name="diff_attn",
    )(safe, q, k, v, stats, group, bias_hi, bias_tiles, lq1, lk1, lq2, lk2, subln)


def _mem_kv_kernel(mem_ref, g_ref, wk_ref, wv_ref, k_ref, v_ref):
    m = mem_ref[0]
    mn = m * lax.rsqrt(jnp.mean(m * m, axis=-1, keepdims=True) + 1e-6) * g_ref[0]
    mb = mn.astype(BF16)
    k_ref[0, 0] = _dot(mb, wk_ref[0]).astype(BF16)
    v_ref[0, 0] = _dot(mb, wv_ref[0]).astype(BF16)


def _mem_kv(mem, g, wk, wv):
    b, m, d = mem.shape
    nl = wk.shape[0]
    out = jax.ShapeDtypeStruct((nl, b, m, d), BF16)
    return pl.pallas_call(
        _mem_kv_kernel,
        grid=(nl, b),
        in_specs=[pl.BlockSpec((1, m, d), lambda li, bi: (bi, 0, 0)),
                  pl.BlockSpec((1, 1, d), lambda li, bi: (li, 0, 0)),
                  pl.BlockSpec((1, d, d), lambda li, bi: (li, 0, 0)),
                  pl.BlockSpec((1, d, d), lambda li, bi: (li, 0, 0))],
        out_specs=[pl.BlockSpec((1, 1, m, d), lambda li, bi: (li, bi, 0, 0))] * 2,
        out_shape=[out, out],
        compiler_params=pltpu.CompilerParams(
            dimension_semantics=("arbitrary", "arbitrary"),
            vmem_limit_bytes=VMEM_LIMIT_BYTES),
        name="mem_kv",
    )(mem, g, wk, wv)


def _xa_compute(x_ref, ya_ref, yb_ref, yc_ref, yd_ref, wout_ref, g_ref, wq_ref, km_ref, vm_ref,
                wo_ref, obuf):
    x1 = x_ref[0]
    for i, y_ref in enumerate((ya_ref, yb_ref, yc_ref, yd_ref)):
        x1 = x1 + _dot(y_ref[0], wout_ref[256 * i:256 * (i + 1), :])
    hx = x1 * lax.rsqrt(jnp.mean(x1 * x1, axis=-1, keepdims=True) + 1e-6) * g_ref[...]
    hd = wq_ref.shape[1] // XA_HEADS
    q = (_dot(hx.astype(BF16), wq_ref[...]) * (hd ** -0.5)).astype(BF16)
    for h in range(XA_HEADS):
        cols = slice(hd * h, hd * (h + 1))
        sc = _dot_nt(q[:, cols], km_ref[0, :, cols])
        p = jnp.exp(sc - jnp.max(sc, axis=-1, keepdims=True))
        oh = _dot(p.astype(BF16), vm_ref[0, :, cols]) / jnp.sum(p, axis=-1, keepdims=True)
        obuf[:, cols] = oh.astype(BF16)
    return x1 + _dot(obuf[...], wo_ref[...])


def _ffn_compute(x, g_ref, wup_ref, dw_ref, dwb_ref, wdown_ref, gfin_ref,
                 carry, acc_ref, perm_in, perm_out, *, tm, dff, final_norm):
    s = pl.program_id(1)
    groups = tm // SUBLANES

    @pl.when(s == 0)
    def _():
        carry[...] = jnp.zeros(carry.shape, F32)

    h = x * lax.rsqrt(jnp.mean(x * x, axis=-1, keepdims=True) + 1e-6) * g_ref[...]
    hb = jnp.concatenate(_to_group_order(h, perm_in, groups), axis=0).astype(BF16)
    bounds = list(range(0, dff, FFN_CHUNK)) + [dff]
    nchunk = len(bounds) - 1

    def conv(u, slot, col0):
        width = u.shape[1]
        first_sublane = lax.broadcasted_iota(jnp.int32, (SUBLANES, width), 0) == 0
        cols = slice(col0, col0 + width)
        wcols = slice(slot * dff + col0, slot * dff + col0 + width)
        tail = u[tm - 2 * SUBLANES:tm, :]
        prev = carry[slot, :, cols]
        carry[slot, :, cols] = tail

        def wrap(k):
            lo, hi = k * SUBLANES, (k + 1) * SUBLANES
            return jnp.where(first_sublane, pltpu.roll(prev[lo:hi, :], 1, 0),
                             pltpu.roll(tail[lo:hi, :], 1, 0))

        back1 = wrap(1)
        u1 = jnp.concatenate([back1, u[0:tm - SUBLANES, :]], axis=0)
        u2 = jnp.concatenate([wrap(0), back1, u[0:tm - 2 * SUBLANES, :]], axis=0)
        w = dw_ref[:, wcols]
        return dwb_ref[:, wcols] + w[2:3, :] * u + w[1:2, :] * u1 + w[0:1, :] * u2

    def up(j):
        c0, c1 = bounds[j], bounds[j + 1]
        return (_dot(hb, wup_ref[:, c0:c1]), _dot(hb, wup_ref[:, dff + c0:dff + c1]))

    ua, ug = up(0)
    for j in range(nchunk):
        col0 = bounds[j]
        nxt = up(j + 1) if j + 1 < nchunk else None
        a = conv(ua, 0, col0)
        gt = conv(ug, 1, col0)
        act = (gt * _sigmoid(gt) * a).astype(BF16)
        part = _dot(act, wdown_ref[bounds[j]:bounds[j + 1], :])
        if j == 0:
            acc_ref[...] = part
        else:
            acc_ref[...] += part
        if nxt is not None:
            ua, ug = nxt
    y = x + _to_row_order(acc_ref[...], perm_out, groups)
    if final_norm:
        y = y * lax.rsqrt(jnp.mean(y * y, axis=-1, keepdims=True) + 1e-6) * gfin_ref[...]
    return y


def _xa_ffn_kernel(x_ref, ya_ref, yb_ref, yc_ref, yd_ref, wout_ref, gxa_ref, wq_ref, km_ref, vm_ref,
                   wo_ref, gffn_ref, wup_ref, dw_ref, dwb_ref, wdown_ref, gfin_ref, o_ref,
                   obuf, carry, acc_ref, perm_in, perm_out, *, tm, dff, final_norm):
    x2 = _xa_compute(x_ref, ya_ref, yb_ref, yc_ref, yd_ref, wout_ref, gxa_ref, wq_ref, km_ref,
                     vm_ref, wo_ref, obuf)
    o_ref[0] = _ffn_compute(x2, gffn_ref, wup_ref, dw_ref, dwb_ref, wdown_ref, gfin_ref,
                            carry, acc_ref, perm_in, perm_out, tm=tm, dff=dff,
                            final_norm=final_norm)


def _xa_ffn(layer, x, ya, yb, yc, yd, wout, gxa, wq, kmem, vmem, wo, gffn, wup, dw, dwb, wdown,
            gfin, final_norm):
    b, s, d = x.shape
    tm = min(TOKEN_TILE, s)
    m = kmem.shape[2]
    dff = wdown.shape[1]
    tok = lambda w: pl.BlockSpec((1, tm, w), lambda bi, si: (bi, si, 0))
    mem_spec = pl.BlockSpec((None, 1, m, d), lambda bi, si: (layer, bi, 0, 0))
    per_layer = lambda a: _layer_spec(a, layer)
    return pl.pallas_call(
        functools.partial(_xa_ffn_kernel, tm=tm, dff=dff, final_norm=final_norm),
        grid=(b, s // tm),
        in_specs=[tok(d), tok(256), tok(256), tok(256), tok(256),
                  per_layer(wout), per_layer(gxa), per_layer(wq), mem_spec, mem_spec, per_layer(wo),
                  per_layer(gffn), per_layer(wup), per_layer(dw), per_layer(dwb), per_layer(wdown),
                  _const_spec(gfin.shape)],
        out_specs=tok(d),
        out_shape=jax.ShapeDtypeStruct((b, s, d), F32),
        scratch_shapes=[pltpu.VMEM((tm, d), BF16),
                        pltpu.VMEM((2, 2 * SUBLANES, dff), F32),
                        pltpu.VMEM((tm, d), F32),
                        pltpu.VMEM((d // LANES, tm + SUBLANES * SUBLANES, LANES), F32),
                        pltpu.VMEM((d // LANES, tm + SUBLANES * SUBLANES, LANES), F32)],
        compiler_params=pltpu.CompilerParams(
            dimension_semantics=("arbitrary", "arbitrary"),
            vmem_limit_bytes=VMEM_LIMIT_BYTES),
        name="xa_ffn",
    )(x, ya, yb, yc, yd, wout, gxa, wq, kmem, vmem, wo, gffn, wup, dw, dwb, wdown, gfin)


def _pack_w_in(w):
    ua, cq, ckv, kr, qc, kc, vc, ud = jnp.split(
        w, [512, 704, 832, 864, 1120, 1376, 1632], axis=-1)
    z = lambda n: jnp.zeros(w.shape[:-1] + (n,), w.dtype)
    half = MLA_ROPE // 2
    kr_sw = jnp.concatenate([kr[..., half:], kr[..., :half]], axis=-1)
    kr_blk = jnp.concatenate([z(MLA_NOPE), kr, z(LANES - MLA_NOPE - MLA_ROPE)], axis=-1)
    kr_sw_blk = jnp.concatenate([z(MLA_NOPE), kr_sw, z(LANES - MLA_NOPE - MLA_ROPE)], axis=-1)
    packed = jnp.concatenate(
        [ua, cq, z(256 - MLA_Q_RANK), ckv, kr_blk, kr_sw_blk, z(LANES), qc, kc, vc, ud], axis=-1)
    return packed


def _pack_w_uq(w):
    nl = w.shape[0]
    w = w.reshape(nl, MLA_Q_RANK, MLA_HEADS, MLA_NOPE + MLA_ROPE)
    nope, rp = w[..., :MLA_NOPE], w[..., MLA_NOPE:]
    half = MLA_ROPE // 2
    rp_sw = jnp.concatenate([rp[..., half:], rp[..., :half]], axis=-1)
    zn = jnp.zeros_like(nope)
    zp = jnp.zeros((nl, MLA_Q_RANK, MLA_HEADS, LANES - MLA_NOPE - MLA_ROPE), w.dtype)

    def fin(a):
        a = a.reshape(nl, MLA_Q_RANK, MLA_HEADS * LANES)
        return jnp.pad(a, ((0, 0), (0, 256 - MLA_Q_RANK), (0, 0))).astype(BF16)

    return (fin(jnp.concatenate([nope, rp, zp], axis=-1)),
            fin(jnp.concatenate([zn, rp_sw, zp], axis=-1)))


def _pack_w_ukv(w):
    nl = w.shape[0]
    w = w.reshape(nl, MLA_KV_RANK, MLA_HEADS, MLA_NOPE + MLA_V)
    kn, v = w[..., :MLA_NOPE], w[..., MLA_NOPE:]
    kn = jnp.concatenate([kn, jnp.zeros_like(kn)], axis=-1).reshape(nl, MLA_KV_RANK, MLA_HEADS * LANES)
    v = v.reshape(nl, MLA_KV_RANK, MLA_HEADS * MLA_V)
    return jnp.concatenate([kn, v], axis=-1).astype(BF16)


def _rope_tables(positions):
    inv_freq = ROPE_BASE ** (-jnp.arange(0, MLA_ROPE, 2, dtype=F32) / MLA_ROPE)
    ang = positions.astype(F32)[:, None] * inv_freq[None, :]
    cos, sin = jnp.cos(ang), jnp.sin(ang)
    s = positions.shape[0]
    pad = jnp.zeros((s, LANES - MLA_NOPE - MLA_ROPE), F32)
    ctab = jnp.concatenate([jnp.ones((s, MLA_NOPE), F32), cos, cos, pad], axis=1)
    stab = jnp.concatenate([jnp.zeros((s, MLA_NOPE), F32), -sin, sin, pad], axis=1)
    return ctab, stab


def _lane_groups(n_lanes, width):
    lane = jnp.arange(n_lanes)[:, None] // width
    return (lane == jnp.arange(LANES)[None, :]).astype(BF16)


def _bias_range(rel_bias):
    shifted = (rel_bias - rel_bias[REL_BUCKETS - 1]) * LOG2_E
    hi = jnp.repeat(jnp.max(shifted, axis=0), 2)
    lo = jnp.repeat(jnp.min(shifted, axis=0), 2)
    pad = (0, LANES - 2 * DIFF_HEADS)
    return jnp.pad(hi, pad).reshape(1, LANES), jnp.pad(hi - lo, pad).reshape(1, LANES)


def _fixed_stabiliser_ok(qmax_sq, kmax_sq, extra_span):
    bound = NORM_MARGIN * jnp.sqrt(qmax_sq * kmax_sq)
    ok = jnp.all(2.0 * bound + extra_span <= SAFE_LOGIT_SPAN)
    return ok.astype(jnp.int32).reshape(1)


def _block_diag(w):
    nl, g, c, _ = w.shape
    out = jnp.zeros((nl, g * c, g * c), w.dtype)
    for i in range(g):
        out = out.at[:, i * c:(i + 1) * c, i * c:(i + 1) * c].set(w[:, i])
    return out


def kernel(x, mem, positions, rel_bias, norm_mix, w_in, w_out, conv_dw, conv_dw_b, conv_ln_g, conv_ln_b, conv_pw, conv_pw_b, mla_q_norm, mla_w_uq, mla_kv_norm, mla_w_ukv, diff_lq1, diff_lk1, diff_lq2, diff_lk2, diff_subln, pool_w, pool_scale, norm_xa, mem_norm, xa_wq, xa_wk, xa_wv, xa_wo, norm_ffn, ffn_up, ffn_dw, ffn_dw_b, ffn_down, norm_final):
    depth = w_in.shape[0]
    s = x.shape[1]
    rows = lambda a: a[:, None, :]
    bf = lambda a: a.astype(BF16)
    ctab, stab = _rope_tables(positions)
    bias_tiles = _bias_tiles(rel_bias, min(DIFF_TILE, s), min(DIFF_TILE, s) // 2)
    kmem, vmem = _mem_kv(mem, rows(mem_norm), bf(xa_wk), bf(xa_wv))
    group_mla = _lane_groups(MLA_HEADS * LANES, LANES)
    group_diff = _lane_groups(2 * DIFF_HEADS * DIFF_HALF, DIFF_HALF)
    bias_hi, bias_span = _bias_range(rel_bias)
    wuq, wuqs = _pack_w_uq(mla_w_uq)
    mix_params = [rows(norm_mix), _pack_w_in(bf(w_in)), conv_dw, rows(conv_dw_b), rows(conv_ln_g),
                  rows(conv_ln_b), bf(conv_pw), rows(conv_pw_b),
                  rows(jnp.pad(mla_q_norm, ((0, 0), (0, 256 - MLA_Q_RANK)))), wuq, wuqs,
                  rows(mla_kv_norm), _pack_w_ukv(mla_w_ukv),
                  bf(_block_diag(pool_w)), rows(pool_scale)]
    lq1, lk1, lq2, lk2 = rows(diff_lq1), rows(diff_lk1), rows(diff_lq2), rows(diff_lk2)
    subln = rows(jnp.tile(diff_subln, (1, DIFF_HEADS)))
    w_out, xa_wq, xa_wo, ffn_up, ffn_down = bf(w_out), bf(xa_wq), bf(xa_wo), bf(ffn_up), bf(ffn_down)
    g_xa, g_ffn, ffn_dw_b = rows(norm_xa), rows(norm_ffn), rows(ffn_dw_b)
    for l in range(depth):
        ya, qm, km, vm, qd, kd, vd, yd, stats = _mix_in(l, x, mix_params, ctab, stab, group_mla,
                                                         group_diff)
        stats = jnp.max(stats, axis=1)
        yb = _mla_attn(_fixed_stabiliser_ok(stats[:, 0], stats[:, 1], 0.0), qm, km, vm, stats,
                       group_mla)
        lambda_init = 0.8 - 0.6 * math.exp(-0.3 * l)
        yc = _diff_attn(l, _fixed_stabiliser_ok(stats[:, 2], stats[:, 3], bias_span), qd, kd, vd,
                        stats, group_diff, bias_hi, bias_tiles, lq1, lk1, lq2, lk2, subln,
                        lambda_init)
        x = _xa_ffn(l, x, ya, yb, yc, yd, w_out, g_xa, xa_wq, kmem, vmem, xa_wo, g_ffn, ffn_up,
                    ffn_dw, ffn_dw_b, ffn_down, norm_final.reshape(1, -1), l == depth - 1)
    return x
```
